```python
import math
import functools
import jax
import jax.numpy as jnp
from jax import lax
import numpy as np

D_MODEL = 2048
BATCH = 1
SEQ = 8192
DEPTH = 2
DEC_BATCH = 128
DEC_SEQ = 4
PAST_LEN = 2048
PAGE_SIZE = 128

MIX_W = D_MODEL // 2
M_HEAD_DIM = 64
M_INNER = MIX_W
M_HEADS = M_INNER // M_HEAD_DIM
M_GROUPS = 4
M_STATE = 128
CONV_W = 4
CONV_DIM = M_INNER + 2 * M_GROUPS * M_STATE
SSD_CHUNK = 128
HEAD_DIM = 64
ATT_HEADS = MIX_W // HEAD_DIM
KV_HEADS = 4
GQA = ATT_HEADS // KV_HEADS
IDX_HEADS = 16
IDX_DIM = 64
TOPK_MAX = 256
Q_BLOCK = 128
T5_BUCKETS = 32
T5_MAX_DIST = 128
R_HEAD = 64
R_WIDTH = MIX_W
R_HEADS = R_WIDTH // R_HEAD
R_DECAY_LORA = 64
R_A_LORA = 64
R_G_LORA = 160
R_PROJ = 3 * R_WIDTH + R_DECAY_LORA + R_A_LORA + R_G_LORA
GN_EPS = 64e-5
D_FF = 256 * ((8 * D_MODEL // 3 + 255) // 256)
NORM_EPS = 1e-6
IN_WIDTHS = (M_INNER, CONV_DIM, M_HEADS,
             ATT_HEADS * HEAD_DIM, KV_HEADS * HEAD_DIM, KV_HEADS * HEAD_DIM,
             IDX_HEADS * IDX_DIM, IDX_DIM, IDX_HEADS,
             R_PROJ, 3 * D_MODEL)
IN_W = sum(IN_WIDTHS)

kernel_name = 'hybrid_ssd_dsa_rwkv7_macaron_step'


def rms_norm(x, g):
    xf = x.astype(jnp.float32)
    y = xf * lax.rsqrt(jnp.mean(xf * xf, axis=-1, keepdims=True) + NORM_EPS)
    return (y * g.astype(jnp.float32)).astype(x.dtype)


def layer_norm(x, g, b):
    xf = x.astype(jnp.float32)
    mu = jnp.mean(xf, axis=-1, keepdims=True)
    var = jnp.mean(jnp.square(xf - mu), axis=-1, keepdims=True)
    y = (xf - mu) * lax.rsqrt(var + NORM_EPS) * g.astype(jnp.float32) + b.astype(jnp.float32)
    return y.astype(x.dtype)


def swiglu(x, wg, wu, wd):
    return (jax.nn.silu(x @ wg) * (x @ wu)) @ wd


def split_cols(p, widths):
    cuts, acc = [], 0
    for w in widths[:-1]:
        acc += w
        cuts.append(acc)
    return jnp.split(p, cuts, axis=-1)


def causal_conv(xbc, conv_state, w, b):
    L = xbc.shape[1]
    xp = jnp.concatenate([conv_state.astype(xbc.dtype), xbc], axis=1)
    y = b + sum(xp[:, j:j + L] * w[j] for j in range(CONV_W))
    return jax.nn.silu(y), xp[:, L:]


def segsum(a):
    T = a.shape[-1]
    x = jnp.broadcast_to(a[..., :, None], a.shape + (T,))
    cs = jnp.cumsum(jnp.where(jnp.tril(jnp.ones((T, T), bool), -1), x, 0.0), axis=-2)
    return jnp.where(jnp.tril(jnp.ones((T, T), bool)), cs, -jnp.inf)


def ssd(x, dt, A, B, C, init_state, chunk):
    b, l, h, p = x.shape
    n = B.shape[-1]
    c = l // chunk
    xd = (x * dt[..., None]).reshape(b, c, chunk, h, p)
    Ad = (dt * A).reshape(b, c, chunk, h).transpose(0, 3, 1, 2)
    Bc = B.reshape(b, c, chunk, h, n)
    Cc = C.reshape(b, c, chunk, h, n)
    A_cs = jnp.cumsum(Ad, axis=-1)
    scores = jnp.einsum('bclhn,bcshn->bhcls', Cc, Bc) * jnp.exp(segsum(Ad))
    y_diag = jnp.einsum('bhcls,bcshp->bclhp', scores, xd)
    decay_states = jnp.exp(A_cs[..., -1:] - A_cs)
    states = jnp.einsum('bclhn,bhcl,bclhp->bchpn', Bc, decay_states, xd)
    states = jnp.concatenate([init_state[:, None], states], axis=1)
    chunk_tot = jnp.pad(A_cs[..., -1], ((0, 0), (0, 0), (1, 0)))
    new_states = jnp.einsum('bhzc,bchpn->bzhpn', jnp.exp(segsum(chunk_tot)), states)
    prev_states, final = new_states[:, :-1], new_states[:, -1]
    y_off = jnp.einsum('bclhn,bchpn,bhcl->bclhp', Cc, prev_states, jnp.exp(A_cs))
    return (y_diag + y_off).reshape(b, l, h, p), final


def mamba_branch(z, xbc, dt_raw, conv_state, ssm_state, lw):
    b, L, _ = z.shape
    xbc, conv_new = causal_conv(xbc, conv_state, lw['conv_w'], lw['conv_b'])
    xs, bm, cm = split_cols(xbc.astype(jnp.float32), (M_INNER, M_GROUPS * M_STATE, M_GROUPS * M_STATE))
    xs = xs.reshape(b, L, M_HEADS, M_HEAD_DIM)
    rep = M_HEADS // M_GROUPS
    bm = jnp.repeat(bm.reshape(b, L, M_GROUPS, M_STATE), rep, axis=2)
    cm = jnp.repeat(cm.reshape(b, L, M_GROUPS, M_STATE), rep, axis=2)
    dt = jax.nn.softplus(dt_raw.astype(jnp.float32) + lw['dt_bias'].astype(jnp.float32))
    A = -jnp.exp(lw['a_log'].astype(jnp.float32))
    chunk = SSD_CHUNK if L % SSD_CHUNK == 0 else L
    y, ssm_new = ssd(xs, dt, A, bm, cm, ssm_state.astype(jnp.float32), chunk)
    y = y + xs * lw['d_skip'].astype(jnp.float32)[:, None]
    y = y.reshape(b, L, M_INNER) * jax.nn.silu(z.astype(jnp.float32))
    y = y.reshape(b, L, M_GROUPS, M_INNER // M_GROUPS)
    y = y * lax.rsqrt(jnp.mean(y * y, axis=-1, keepdims=True) + NORM_EPS)
    y = y.reshape(b, L, M_INNER) * lw['ssm_norm'].astype(jnp.float32)
    return y.astype(z.dtype), conv_new, ssm_new


def t5_bias(rel, table):
    n = jnp.maximum(rel, 0)
    exact = T5_BUCKETS // 2
    nf = jnp.maximum(n, exact).astype(jnp.float32)
    large = exact + (jnp.log(nf / exact) / math.log(T5_MAX_DIST / exact) * (T5_BUCKETS - exact)).astype(jnp.int32)
    bucket = jnp.where(n < exact, n, jnp.minimum(large, T5_BUCKETS - 1))
    return table[bucket]


def indexer_scores(qi, wi, ki):
    s = jax.nn.relu(jnp.einsum('bqhd,bsd->bqhs', qi, ki).astype(jnp.float32))
    return jnp.einsum('bqhs,bqh->bqs', s, wi.astype(jnp.float32))


def sparse_attend(q, k_sel, v_sel, sel, qpos, t5_table):
    b, Q = q.shape[:2]
    qg = q.reshape(b, Q, KV_HEADS, GQA, HEAD_DIM)
    logits = jnp.einsum('bqgrd,bqkgd->bqgrk', qg, k_sel).astype(jnp.float32) * HEAD_DIM ** -0.5
    rel = qpos[..., None] - sel
    bias = t5_bias(rel, t5_table).astype(jnp.float32)
    bias = bias.reshape(bias.shape[:3] + (KV_HEADS, GQA)).transpose(0, 1, 3, 4, 2)
    logits = jnp.where((rel >= 0)[:, :, None, None, :], logits + bias, -jnp.inf)
    p = jax.nn.softmax(logits, axis=-1).astype(v_sel.dtype)
    o = jnp.einsum('bqgrk,bqkgd->bqgrd', p, v_sel)
    return o.reshape(b, Q, ATT_HEADS * HEAD_DIM)


def dsa_prompt(q, k, v, qi, ki, wi, t5_table):
    b, S = q.shape[:2]
    top = min(TOPK_MAX, S // 4)
    key_pos = jnp.arange(S, dtype=jnp.int32)
    bidx = jnp.arange(b)[:, None, None]

    def block(i):
        q0 = i * Q_BLOCK
        sl = lambda t: lax.dynamic_slice_in_dim(t, q0, Q_BLOCK, axis=1)
        qpos = q0 + jnp.arange(Q_BLOCK, dtype=jnp.int32)
        sc = indexer_scores(sl(qi), sl(wi), ki)
        sc = jnp.where(key_pos[None, None, :] <= qpos[None, :, None], sc, -jnp.inf)
        sel = lax.top_k(sc, top)[1]
        return sparse_attend(sl(q), k[bidx, sel], v[bidx, sel], sel, qpos[None], t5_table)

    out = lax.map(block, jnp.arange(S // Q_BLOCK, dtype=jnp.int32))
    return jnp.swapaxes(out, 0, 1).reshape(b, S, ATT_HEADS * HEAD_DIM)


def dsa_sample(q, k, v, qi, ki, wi, t5_table, cache_k, cache_v, cache_kidx, page_table):
    b, T = q.shape[:2]
    past = page_table.shape[1] * PAGE_SIZE
    top = min(TOPK_MAX, (past + T) // 4)
    ki_past = cache_kidx[page_table].reshape(b, past, IDX_DIM).astype(ki.dtype)
    ki_all = jnp.concatenate([ki_past, ki], axis=1)
    qpos = past + jnp.arange(T, dtype=jnp.int32)
    key_pos = jnp.arange(past + T, dtype=jnp.int32)
    sc = indexer_scores(qi, wi, ki_all)
    sc = jnp.where(key_pos[None, None, :] <= qpos[None, :, None], sc, -jnp.inf)
    sel = lax.top_k(sc, top)[1]
    bidx = jnp.arange(b)[:, None, None]
    s_past = jnp.minimum(sel, past - 1)
    phys = page_table[bidx, s_past // PAGE_SIZE] * PAGE_SIZE + s_past % PAGE_SIZE
    s_new = jnp.clip(sel - past, 0, T - 1)
    is_past = (sel < past)[..., None, None]
    k_sel = jnp.where(is_past, cache_k.reshape(-1, KV_HEADS, HEAD_DIM)[phys].astype(k.dtype), k[bidx, s_new])
    v_sel = jnp.where(is_past, cache_v.reshape(-1, KV_HEADS, HEAD_DIM)[phys].astype(v.dtype), v[bidx, s_new])
    return sparse_attend(q, k_sel, v_sel, sel, qpos[None], t5_table)


def rwkv_branch(pr, shift_state, wkv_state, lw):
    b, L, _ = pr.shape
    f32 = jnp.float32
    prev = jnp.concatenate([shift_state[:, None].astype(pr.dtype), pr[:, :-1]], axis=1)
    pm = pr + (prev - pr) * lw['rwkv_mu']
    r, k, v, xw, xa, xg = split_cols(pm, (R_WIDTH, R_WIDTH, R_WIDTH, R_DECAY_LORA, R_A_LORA, R_G_LORA))
    w = -jax.nn.softplus(-(lw['rwkv_w0'] + jnp.tanh(xw) @ lw['rwkv_w2']).astype(f32)) - 0.5
    decay = jnp.exp(-jnp.exp(w))
    a = jax.nn.sigmoid((lw['rwkv_a0'] + xa @ lw['rwkv_a2']).astype(f32))
    g = jax.nn.sigmoid(xg) @ lw['rwkv_g2']
    heads = lambda t: t.astype(f32).reshape(b, L, R_HEADS, R_HEAD)
    r, k, v, decay, a = heads(r), heads(k), heads(v), heads(decay), heads(a)
    kk = k * lw['rwkv_kk'].astype(f32).reshape(R_HEADS, R_HEAD)
    kk = kk / jnp.maximum(jnp.sqrt(jnp.sum(kk * kk, axis=-1, keepdims=True)), 1e-12)
    k = k * (1.0 + (a - 1.0) * lw['rwkv_ka'].astype(f32).reshape(R_HEADS, R_HEAD))

    def step(S, inp):
        r_t, w_t, k_t, v_t, kk_t, a_t = inp
        S = (S * w_t[:, :, None, :]
             - jnp.einsum('bhij,bhj->bhi', S, kk_t)[..., None] * (kk_t * a_t)[:, :, None, :]
             + v_t[..., :, None] * k_t[:, :, None, :])
        return S, jnp.einsum('bhij,bhj->bhi', S, r_t)

    tm = lambda t: jnp.swapaxes(t, 0, 1)
    S_fin, ys = lax.scan(step, wkv_state.astype(f32), (tm(r), tm(decay), tm(k), tm(v), tm(kk), tm(a)))
    y = tm(ys)
    mu = jnp.mean(y, axis=-1, keepdims=True)
    var = jnp.mean(jnp.square(y - mu), axis=-1, keepdims=True)
    y = ((y - mu) * lax.rsqrt(var + GN_EPS) * lw['rwkv_gn_g'].astype(f32).reshape(R_HEADS, R_HEAD)
         + lw['rwkv_gn_b'].astype(f32).reshape(R_HEADS, R_HEAD))
    y = y + jnp.sum(r * k * lw['rwkv_rk'].astype(f32), axis=-1, keepdims=True) * v
    y = y.reshape(b, L, R_WIDTH).astype(pr.dtype) * g
    return y, S_fin, pr[:, -1]


def trunk_layer(x, lw, t5_table, conv_st, ssm_st, wkv_st, shift_st, attend):
    b, L, _ = x.shape
    h = x + 0.5 * swiglu(rms_norm(x, lw['ffn1_norm']), lw['ffn1_gate'], lw['ffn1_up'], lw['ffn1_down'])
    u = rms_norm(h, lw['mix_norm'])
    z, xbc, dt_raw, q, k, v, qi, ki, wi, pr, gates = split_cols(u @ lw['w_in'], IN_WIDTHS)
    y_a, conv_new, ssm_new = mamba_branch(z, xbc, dt_raw, conv_st, ssm_st, lw)
    q = q.reshape(b, L, ATT_HEADS, HEAD_DIM)
    k = k.reshape(b, L, KV_HEADS, HEAD_DIM)
    v = v.reshape(b, L, KV_HEADS, HEAD_DIM)
    qi = qi.reshape(b, L, IDX_HEADS, IDX_DIM)
    ki = layer_norm(ki, lw['kidx_ln_g'], lw['kidx_ln_b'])
    wi = wi * (IDX_HEADS ** -0.5 * IDX_DIM ** -0.5)
    y_b = attend(q, k, v, qi, ki, wi, t5_table)
    y_c, wkv_new, shift_new = rwkv_branch(pr, shift_st, wkv_st, lw)
    g_a, g_b, g_c = jnp.split(jax.nn.sigmoid(gates), 3, axis=-1)
    merged = g_a * (y_a @ lw['w_br_a']) + g_b * (y_b @ lw['w_br_b']) + g_c * (y_c @ lw['w_br_c'])
    h = h + merged @ lw['w_out']
    h = h + 0.5 * swiglu(rms_norm(h, lw['ffn2_norm']), lw['ffn2_gate'], lw['ffn2_up'], lw['ffn2_down'])
    return h, (k, v, ki, ssm_new, conv_new, wkv_new, shift_new)


def setup_inputs(seed: int = 0) -> dict:
    key = jax.random.key(seed)
    keys = jax.random.split(key, 64)
    cnt = [0]

    def nk():
        cnt[0] += 1
        return keys[cnt[0] - 1]

    nrm = lambda shape, scale=1.0: jax.random.normal(nk(), shape, jnp.float32) * scale
    uni = lambda shape, lo, hi: jax.random.uniform(nk(), shape, jnp.float32, lo, hi)
    n_pages = PAST_LEN // PAGE_SIZE
    used = DEC_BATCH * n_pages
    n_pool = used + max(1, used // 4)
    d = D_MODEL
    inp = {}
    inp['x_prompt'] = nrm((BATCH, SEQ, d))
    inp['x_sample'] = nrm((DEC_BATCH, DEC_SEQ, d))
    inp['cache_k'] = nrm((DEPTH, n_pool, PAGE_SIZE, KV_HEADS, HEAD_DIM))
    inp['cache_v'] = nrm((DEPTH, n_pool, PAGE_SIZE, KV_HEADS, HEAD_DIM))
    inp['cache_kidx'] = nrm((DEPTH, n_pool, PAGE_SIZE, IDX_DIM))
    inp['page_table'] = jax.random.permutation(nk(), n_pool)[:used].reshape(DEC_BATCH, n_pages).astype(jnp.int32)
    inp['state_ssm'] = nrm((DEPTH, DEC_BATCH, M_HEADS, M_HEAD_DIM, M_STATE), 0.5)
    inp['state_conv'] = nrm((DEPTH, DEC_BATCH, CONV_W - 1, CONV_DIM))
    inp['state_wkv'] = nrm((DEPTH, DEC_BATCH, R_HEADS, R_HEAD, R_HEAD), 0.1)
    inp['state_shift'] = nrm((DEPTH, DEC_BATCH, R_PROJ))
    inp['ffn1_norm'] = 1.0 + nrm((DEPTH, d), 0.01)
    inp['ffn1_gate'] = nrm((DEPTH, d, D_FF), d ** -0.5)
    inp['ffn1_up'] = nrm((DEPTH, d, D_FF), d ** -0.5)
    inp['ffn1_down'] = nrm((DEPTH, D_FF, d), D_FF ** -0.5)
    inp['mix_norm'] = 1.0 + nrm((DEPTH, d), 0.01)
    inp['w_in'] = nrm((DEPTH, d, IN_W), d ** -0.5)
    inp['conv_w'] = nrm((DEPTH, CONV_W, CONV_DIM), CONV_W ** -0.5)
    inp['conv_b'] = nrm((DEPTH, CONV_DIM), 0.02)
    dt0 = jnp.exp(uni((DEPTH, M_HEADS), math.log(1e-3), math.log(1e-1)))
    inp['dt_bias'] = dt0 + jnp.log(-jnp.expm1(-dt0))
    inp['a_log'] = jnp.log(uni((DEPTH, M_HEADS), 1.0, 16.0))
    inp['d_skip'] = 1.0 + nrm((DEPTH, M_HEADS), 0.01)
    inp['ssm_norm'] = 1.0 + nrm((DEPTH, M_INNER), 0.01)
    inp['kidx_ln_g'] = 1.0 + nrm((DEPTH, IDX_DIM), 0.01)
    inp['kidx_ln_b'] = nrm((DEPTH, IDX_DIM), 0.02)
    inp['t5_table'] = nrm((T5_BUCKETS, ATT_HEADS), 0.5)
    inp['rwkv_mu'] = uni((DEPTH, R_PROJ), 0.0, 1.0)
    inp['rwkv_w0'] = uni((DEPTH, R_WIDTH), -6.0, -1.0)
    inp['rwkv_w2'] = nrm((DEPTH, R_DECAY_LORA, R_WIDTH), 0.1)
    inp['rwkv_a0'] = nrm((DEPTH, R_WIDTH), 0.1)
    inp['rwkv_a2'] = nrm((DEPTH, R_A_LORA, R_WIDTH), R_A_LORA ** -0.5)
    inp['rwkv_g2'] = nrm((DEPTH, R_G_LORA, R_WIDTH), R_G_LORA ** -0.5)
    inp['rwkv_kk'] = 0.85 + nrm((DEPTH, R_WIDTH), 0.02)
    inp['rwkv_ka'] = 1.0 + nrm((DEPTH, R_WIDTH), 0.02)
    inp['rwkv_rk'] = nrm((DEPTH, R_HEADS, R_HEAD), 0.1)
    inp['rwkv_gn_g'] = 1.0 + nrm((DEPTH, R_WIDTH), 0.01)
    inp['rwkv_gn_b'] = nrm((DEPTH, R_WIDTH), 0.02)
    inp['w_br_a'] = nrm((DEPTH, M_INNER, d), M_INNER ** -0.5)
    inp['w_br_b'] = nrm((DEPTH, ATT_HEADS * HEAD_DIM, d), (ATT_HEADS * HEAD_DIM) ** -0.5)
    inp['w_br_c'] = nrm((DEPTH, R_WIDTH, d), R_WIDTH ** -0.5)
    inp['w_out'] = nrm((DEPTH, d, d), d ** -0.5)
    inp['ffn2_norm'] = 1.0 + nrm((DEPTH, d), 0.01)
    inp['ffn2_gate'] = nrm((DEPTH, d, D_FF), d ** -0.5)
    inp['ffn2_up'] = nrm((DEPTH, d, D_FF), d ** -0.5)
    inp['ffn2_down'] = nrm((DEPTH, D_FF, d), D_FF ** -0.5)
    inp['final_norm'] = 1.0 + nrm((d,), 0.01)
    return inp


def reference(x_prompt, x_sample, cache_k, cache_v, cache_kidx, page_table, state_ssm, state_conv,
              state_wkv, state_shift, ffn1_norm, ffn1_gate, ffn1_up, ffn1_down, mix_norm, w_in, conv_w,
              conv_b, dt_bias, a_log, d_skip, ssm_norm, kidx_ln_g, kidx_ln_b, t5_table, rwkv_mu, rwkv_w0,
              rwkv_w2, rwkv_a0, rwkv_a2, rwkv_g2, rwkv_kk, rwkv_ka, rwkv_rk, rwkv_gn_g, rwkv_gn_b, w_br_a,
              w_br_b, w_br_c, w_out, ffn2_norm, ffn2_gate, ffn2_up, ffn2_down, final_norm):
    dtype = x_prompt.dtype
    bp = x_prompt.shape[0]
    conv0 = jnp.zeros((bp, CONV_W - 1, CONV_DIM), dtype)
    ssm0 = jnp.zeros((bp, M_HEADS, M_HEAD_DIM, M_STATE), dtype)
    wkv0 = jnp.zeros((bp, R_HEADS, R_HEAD, R_HEAD), dtype)
    shift0 = jnp.zeros((bp, R_PROJ), dtype)
    hp, hs = x_prompt, x_sample
    st_prompt, st_sample = [], []
    for l in range(DEPTH):
        lw = {
            'ffn1_norm': ffn1_norm[l], 'ffn1_gate': ffn1_gate[l], 'ffn1_up': ffn1_up[l], 'ffn1_down': ffn1_down[l],
            'mix_norm': mix_norm[l], 'w_in': w_in[l], 'conv_w': conv_w[l], 'conv_b': conv_b[l],
            'dt_bias': dt_bias[l], 'a_log': a_log[l], 'd_skip': d_skip[l], 'ssm_norm': ssm_norm[l],
            'kidx_ln_g': kidx_ln_g[l], 'kidx_ln_b': kidx_ln_b[l],
            'rwkv_mu': rwkv_mu[l], 'rwkv_w0': rwkv_w0[l], 'rwkv_w2': rwkv_w2[l], 'rwkv_a0': rwkv_a0[l],
            'rwkv_a2': rwkv_a2[l], 'rwkv_g2': rwkv_g2[l], 'rwkv_kk': rwkv_kk[l], 'rwkv_ka': rwkv_ka[l],
            'rwkv_rk': rwkv_rk[l], 'rwkv_gn_g': rwkv_gn_g[l], 'rwkv_gn_b': rwkv_gn_b[l],
            'w_br_a': w_br_a[l], 'w_br_b': w_br_b[l], 'w_br_c': w_br_c[l], 'w_out': w_out[l],
            'ffn2_norm': ffn2_norm[l], 'ffn2_gate': ffn2_gate[l], 'ffn2_up': ffn2_up[l], 'ffn2_down': ffn2_down[l],
        }
        hp, sp = trunk_layer(hp, lw, t5_table, conv0, ssm0, wkv0, shift0, dsa_prompt)
        attend_s = functools.partial(dsa_sample, cache_k=cache_k[l], cache_v=cache_v[l],
                                     cache_kidx=cache_kidx[l], page_table=page_table)
        hs, ss = trunk_layer(hs, lw, t5_table, state_conv[l], state_ssm[l], state_wkv[l], state_shift[l], attend_s)
        st_prompt.append(sp)
        st_sample.append(ss)
    y_prompt = rms_norm(hp, final_norm)
    y_sample = rms_norm(hs, final_norm)
    stk = lambda sts, i: jnp.stack([s[i] for s in sts], axis=0).astype(dtype)
    k_p, v_p, kidx_p = stk(st_prompt, 0), stk(st_prompt, 1), stk(st_prompt, 2)
    ssm_p, conv_p, wkv_p, shift_p = stk(st_prompt, 3), stk(st_prompt, 4), stk(st_prompt, 5), stk(st_prompt, 6)
    k_s, v_s, kidx_s = stk(st_sample, 0), stk(st_sample, 1), stk(st_sample, 2)
    ssm_s, conv_s, wkv_s, shift_s = stk(st_sample, 3), stk(st_sample, 4), stk(st_sample, 5), stk(st_sample, 6)
    return (y_prompt, y_sample, k_p, v_p, kidx_p, ssm_p, conv_p, wkv_p, shift_p,
            k_s, v_s, kidx_s, ssm_s, conv_s, wkv_s, shift_s)
```

```python
import functools
import math

import jax
import jax.numpy as jnp
from jax import lax
from jax.experimental import pallas as pl
from jax.experimental.pallas import tpu as pltpu

F32 = jnp.float32
BF16 = jnp.bfloat16
I32 = jnp.int32

D_MODEL = 2048
MIX_W = D_MODEL // 2
M_HEAD_DIM = 64
M_INNER = MIX_W
M_HEADS = M_INNER // M_HEAD_DIM
M_GROUPS = 4
M_STATE = 128
CONV_W = 4
CONV_DIM = M_INNER + 2 * M_GROUPS * M_STATE
SSD_CHUNK = 128
HEAD_DIM = 64
ATT_HEADS = MIX_W // HEAD_DIM
KV_HEADS = 4
GQA = ATT_HEADS // KV_HEADS
IDX_HEADS = 16
IDX_DIM = 64
TOPK_MAX = 256
T5_BUCKETS = 32
T5_MAX_DIST = 128
R_HEAD = 64
R_WIDTH = MIX_W
R_HEADS = R_WIDTH // R_HEAD
R_DECAY_LORA = 64
R_A_LORA = 64
R_G_LORA = 160
R_PROJ = 3 * R_WIDTH + R_DECAY_LORA + R_A_LORA + R_G_LORA
GN_EPS = 64e-5
D_FF = 256 * ((8 * D_MODEL // 3 + 255) // 256)
NORM_EPS = 1e-6
PAGE_SIZE = 128

SEQ_PAD = 8
LANE = 128
VMEM_LIMIT = 56 * 2**20
INT_MIN = -2**31
NEG_BIG = -1e30

C_Z, C_XS, C_BC, C_Q, C_QI, C_RR, C_RK, C_RV = (i * 1024 for i in range(8))
C_GATE = 8192
C_LORA = C_GATE + 3 * D_MODEL
LORA_W = 512
C_K = C_LORA + LORA_W
C_V = C_K + 256
C_MISC = C_V + 256
P_W = C_MISC + LANE
MISC_DT = IDX_DIM
MISC_WI = IDX_DIM + M_HEADS
RWKV_RC_CHUNK = 64

HI = lax.Precision.HIGHEST


def _cparams(sem):
    return pltpu.CompilerParams(dimension_semantics=sem, vmem_limit_bytes=VMEM_LIMIT)


def _mm(a, b):
    return jnp.dot(a.astype(BF16), b.astype(BF16), preferred_element_type=F32)


def _mm_nt(a, b):
    return lax.dot_general(a.astype(BF16), b.astype(BF16), (((1,), (1,)), ((), ())),
                           preferred_element_type=F32)


def _mm_tn(a, b):
    return lax.dot_general(a.astype(BF16), b.astype(BF16), (((0,), (0,)), ((), ())),
                           preferred_element_type=F32)


def _mm_hi(a, b):
    return jnp.dot(a, b, precision=HI, preferred_element_type=F32)


def _mm_nt_hi(a, b):
    return lax.dot_general(a, b, (((1,), (1,)), ((), ())), precision=HI, preferred_element_type=F32)


def _mm_tn_hi(a, b):
    return lax.dot_general(a, b, (((0,), (0,)), ((), ())), precision=HI, preferred_element_type=F32)


def _sigmoid(x):
    return jax.nn.sigmoid(x)


def _silu(x):
    return x * jax.nn.sigmoid(x)


def _full(shape):
    nd = len(shape)
    return pl.BlockSpec(shape, lambda *_: (0,) * nd)


def _ffn_kernel(x_ref, g_ref, wg_ref, wu_ref, wd_ref, o_ref, n_ref):
    f = pl.program_id(1)

    @pl.when(f == 0)
    def _():
        x = x_ref[...]
        ms = jnp.mean(x * x, axis=-1, keepdims=True)
        n_ref[...] = (x * lax.rsqrt(ms + NORM_EPS) * g_ref[...]).astype(BF16)
        o_ref[...] = x

    n = n_ref[...]
    a = jnp.dot(n, wg_ref[...], preferred_element_type=F32)
    b = jnp.dot(n, wu_ref[...], preferred_element_type=F32)
    hid = (0.5 * _silu(a) * b).astype(BF16)
    o_ref[...] += jnp.dot(hid, wd_ref[...], preferred_element_type=F32)


def _ffn(x, g, wg, wu, wd, tm=512, tf=512):
    m = x.shape[0]
    tm = min(tm, m)
    return pl.pallas_call(
        _ffn_kernel,
        grid=(m // tm, D_FF // tf),
        in_specs=[pl.BlockSpec((tm, D_MODEL), lambda i, f: (i, 0)),
                  pl.BlockSpec((1, D_MODEL), lambda i, f: (0, 0)),
                  pl.BlockSpec((D_MODEL, tf), lambda i, f: (0, f)),
                  pl.BlockSpec((D_MODEL, tf), lambda i, f: (0, f)),
                  pl.BlockSpec((tf, D_MODEL), lambda i, f: (f, 0))],
        out_specs=pl.BlockSpec((tm, D_MODEL), lambda i, f: (i, 0)),
        out_shape=jax.ShapeDtypeStruct((m, D_MODEL), F32),
        scratch_shapes=[pltpu.VMEM((tm, D_MODEL), BF16)],
        compiler_params=_cparams(("parallel", "arbitrary")),
        name="ffn",
    )(x, g, wg, wu, wd)


def _inproj_kernel(x_ref, g_ref, w_ref, o_ref, n_ref):
    j = pl.program_id(1)

    @pl.when(j == 0)
    def _():
        x = x_ref[...]
        ms = jnp.mean(x * x, axis=-1, keepdims=True)
        n_ref[...] = (x * lax.rsqrt(ms + NORM_EPS) * g_ref[...]).astype(BF16)

    o_ref[...] = jnp.dot(n_ref[...], w_ref[...], preferred_element_type=F32)


def _inproj(x, g, w, tm=512, tn=1408):
    m = x.shape[0]
    tm = min(tm, m)
    return pl.pallas_call(
        _inproj_kernel,
        grid=(m // tm, P_W // tn),
        in_specs=[pl.BlockSpec((tm, D_MODEL), lambda i, j: (i, 0)),
                  pl.BlockSpec((1, D_MODEL), lambda i, j: (0, 0)),
                  pl.BlockSpec((D_MODEL, tn), lambda i, j: (0, j))],
        out_specs=pl.BlockSpec((tm, tn), lambda i, j: (i, j)),
        out_shape=jax.ShapeDtypeStruct((m, P_W), F32),
        scratch_shapes=[pltpu.VMEM((tm, D_MODEL), BF16)],
        compiler_params=_cparams(("parallel", "arbitrary")),
        name="inproj",
    )(x, g, w)


def _merge_kernel(h_ref, ya_ref, yb_ref, yc_ref, ga_ref, gb_ref, gc_ref,
                  wa_ref, wb_ref, wc_ref, wo_ref, o_ref):
    j = pl.program_id(1)

    @pl.when(j == 0)
    def _():
        o_ref[...] = h_ref[...]

    merged = (_sigmoid(ga_ref[...]) * jnp.dot(ya_ref[...], wa_ref[...], preferred_element_type=F32)
              + _sigmoid(gb_ref[...]) * jnp.dot(yb_ref[...], wb_ref[...], preferred_element_type=F32)
              + _sigmoid(gc_ref[...]) * jnp.dot(yc_ref[...], wc_ref[...], preferred_element_type=F32))
    o_ref[...] += jnp.dot(merged.astype(BF16), wo_ref[...], preferred_element_type=F32)


def _merge(h, p, ya, yb, yc, wa, wb, wc, wo, tm=512, tk=512):
    m = h.shape[0]
    tm = min(tm, m)
    gb0 = C_GATE // tk
    nk = D_MODEL // tk
    yspec = pl.BlockSpec((tm, MIX_W), lambda i, j: (i, 0))
    wspec = pl.BlockSpec((MIX_W, tk), lambda i, j: (0, j))
    return pl.pallas_call(
        _merge_kernel,
        grid=(m // tm, nk),
        in_specs=[pl.BlockSpec((tm, D_MODEL), lambda i, j: (i, 0)), yspec, yspec, yspec,
                  pl.BlockSpec((tm, tk), lambda i, j: (i, gb0 + j)),
                  pl.BlockSpec((tm, tk), lambda i, j: (i, gb0 + nk + j)),
                  pl.BlockSpec((tm, tk), lambda i, j: (i, gb0 + 2 * nk + j)),
                  wspec, wspec, wspec,
                  pl.BlockSpec((tk, D_MODEL), lambda i, j: (j, 0))],
        out_specs=pl.BlockSpec((tm, D_MODEL), lambda i, j: (i, 0)),
        out_shape=jax.ShapeDtypeStruct((m, D_MODEL), F32),
        compiler_params=_cparams(("parallel", "arbitrary")),
        name="merge",
    )(h, ya, yb, yc, p, p, p, wa, wb, wc, wo)


def _final_norm_kernel(x_ref, g_ref, o_ref):
    x = x_ref[...]
    ms = jnp.mean(x * x, axis=-1, keepdims=True)
    o_ref[...] = x * lax.rsqrt(ms + NORM_EPS) * g_ref[...]


def _final_norm(x, g, tm=512):
    m = x.shape[0]
    tm = min(tm, m)
    return pl.pallas_call(
        _final_norm_kernel,
        grid=(m // tm,),
        in_specs=[pl.BlockSpec((tm, D_MODEL), lambda i: (i, 0)), _full((1, D_MODEL))],
        out_specs=pl.BlockSpec((tm, D_MODEL), lambda i: (i, 0)),
        out_shape=jax.ShapeDtypeStruct((m, D_MODEL), F32),
        compiler_params=_cparams(("parallel",)),
        name="final_norm",
    )(x, g)


def _kidx_ln_kernel(m_ref, g_ref, b_ref, o_ref):
    x = m_ref[:, 0:IDX_DIM]
    mu = jnp.mean(x, axis=-1, keepdims=True)
    var = jnp.mean(jnp.square(x - mu), axis=-1, keepdims=True)
    o_ref[...] = (x - mu) * lax.rsqrt(var + NORM_EPS) * g_ref[...] + b_ref[...]


def _kidx_ln(p, g, b, tm=512):
    m = p.shape[0]
    tm = min(tm, m)
    return pl.pallas_call(
        _kidx_ln_kernel,
        grid=(m // tm,),
        in_specs=[pl.BlockSpec((tm, LANE), lambda i: (i, C_MISC // LANE)),
                  _full((1, IDX_DIM)), _full((1, IDX_DIM))],
        out_specs=pl.BlockSpec((tm, IDX_DIM), lambda i: (i, 0)),
        out_shape=jax.ShapeDtypeStruct((m, IDX_DIM), F32),
        compiler_params=_cparams(("parallel",)),
        name="kidx_ln",
    )(p, g, b)


def _shifted(x, k, prev, rows):
    y = pltpu.roll(x, k, 0)
    for r in range(k):
        y = jnp.where(rows == r, prev[3 - k + r:4 - k + r, :], y)
    return y


def _ssd_kernel(z_ref, xs_ref, bc_ref, misc_ref, cst_ref, st0_ref, cw_ref, cb_ref, dtb_ref,
                alog_ref, dexp_ref, norm_ref, e16_ref, y_ref, stout_ref,
                prev_ref, st_ref, yd_ref, *, T, n_valid):
    c = pl.program_id(1)
    nc = pl.num_programs(1)

    @pl.when(c == 0)
    def _():
        prev_ref[...] = cst_ref[0]
        st_ref[...] = st0_ref[0]

    rows = lax.broadcasted_iota(I32, (T, 1), 0)
    cw = cw_ref[...]
    prev = prev_ref[...]

    def conv(x, lo, hi):
        pv = prev[:, lo:hi]
        y = cb_ref[:, lo:hi] + x * cw[3:4, lo:hi]
        for k in (1, 2, 3):
            y = y + _shifted(x, k, pv, rows) * cw[3 - k:4 - k, lo:hi]
        return _silu(y)

    xs_raw = xs_ref[...]
    bc_raw = bc_ref[...]
    xs = conv(xs_raw, 0, M_INNER)
    bc = conv(bc_raw, M_INNER, CONV_DIM)
    prev_ref[:, 0:M_INNER] = xs_raw[T - 3:T, :]
    prev_ref[:, M_INNER:CONV_DIM] = bc_raw[T - 3:T, :]

    dt = jax.nn.softplus(misc_ref[:, MISC_DT:MISC_DT + M_HEADS] + dtb_ref[...])
    if n_valid < T:
        dt = jnp.where(rows < n_valid, dt, 0.0)
    ad = dt * (-jnp.exp(alog_ref[...]))
    ri = lax.broadcasted_iota(I32, (T, T), 0)
    ci = lax.broadcasted_iota(I32, (T, T), 1)
    lower = ri >= ci
    tril = lower.astype(F32)
    triu = (ri <= ci).astype(F32)
    eye16 = (lax.broadcasted_iota(I32, (M_HEADS, M_HEADS), 0)
             == lax.broadcasted_iota(I32, (M_HEADS, M_HEADS), 1)).astype(F32)
    cs = _mm_hi(tril, ad)
    cst = _mm_hi(_mm_nt_hi(eye16, ad), triu)
    e16 = e16_ref[...]
    dt_e = _mm_hi(dt, e16)
    ecs_e = _mm_hi(jnp.exp(cs), e16)
    wl_e = _mm_hi(jnp.exp(cs[T - 1:T, :] - cs), e16)
    xd = xs * dt_e
    xdw = (xd * wl_e).astype(BF16)
    xd = xd.astype(BF16)
    bcb = bc.astype(BF16)
    for g in range(M_GROUPS):
        bg = bcb[:, g * M_STATE:(g + 1) * M_STATE]
        cg = bcb[:, (M_GROUPS + g) * M_STATE:(M_GROUPS + g + 1) * M_STATE]
        cbm = _mm_nt(cg, bg)
        for hh in range(M_HEADS // M_GROUPS):
            h = g * (M_HEADS // M_GROUPS) + hh
            hs = slice(h * M_HEAD_DIM, (h + 1) * M_HEAD_DIM)
            diff = cs[:, h:h + 1] - cst[h:h + 1, :]
            lm = jnp.exp(jnp.where(lower, diff, -jnp.inf))
            yd = _mm(cbm * lm, xd[:, hs])
            st = st_ref[h]
            yo = _mm_nt(cg, st)
            yd_ref[:, hs] = yd + yo * ecs_e[:, hs]
            upd = _mm_tn(xdw[:, hs], bg)
            dec = jnp.exp(cst[h:h + 1, T - 1:T])
            st_ref[h] = st * dec + upd

    y = yd_ref[...] + xs * dexp_ref[...]
    y = y * _silu(z_ref[...])
    gw = M_INNER // M_GROUPS
    for g in range(M_GROUPS):
        yg = y[:, g * gw:(g + 1) * gw]
        ms = jnp.mean(yg * yg, axis=-1, keepdims=True)
        y_ref[:, g * gw:(g + 1) * gw] = (yg * lax.rsqrt(ms + NORM_EPS)
                                         * norm_ref[:, g * gw:(g + 1) * gw]).astype(y_ref.dtype)

    @pl.when(c == nc - 1)
    def _():
        stout_ref[0] = st_ref[...]


def _ssd(p, row0, n_seq, n_chunks, T, n_valid, conv_state, ssm0, cw, cb, dtb, alog, dexp, norm, e16):
    rb0 = row0 // T

    def rmap(col):
        return lambda s, c: (rb0 + s * n_chunks + c, col)

    kern = functools.partial(_ssd_kernel, T=T, n_valid=n_valid)
    return pl.pallas_call(
        kern,
        grid=(n_seq, n_chunks),
        in_specs=[pl.BlockSpec((T, 1024), rmap(C_Z // 1024)),
                  pl.BlockSpec((T, 1024), rmap(C_XS // 1024)),
                  pl.BlockSpec((T, 1024), rmap(C_BC // 1024)),
                  pl.BlockSpec((T, LANE), rmap(C_MISC // LANE)),
                  pl.BlockSpec((1, CONV_W - 1, CONV_DIM), lambda s, c: (s, 0, 0)),
                  pl.BlockSpec((1, M_HEADS, M_HEAD_DIM, M_STATE), lambda s, c: (s, 0, 0, 0)),
                  _full((CONV_W, CONV_DIM)), _full((1, CONV_DIM)), _full((1, M_HEADS)),
                  _full((1, M_HEADS)), _full((1, M_INNER)), _full((1, M_INNER)),
                  _full((M_HEADS, M_INNER))],
        out_specs=[pl.BlockSpec((T, M_INNER), lambda s, c: (s * n_chunks + c, 0)),
                   pl.BlockSpec((1, M_HEADS, M_HEAD_DIM, M_STATE), lambda s, c: (s, 0, 0, 0))],
        out_shape=[jax.ShapeDtypeStruct((n_seq * n_chunks * T, M_INNER), BF16),
                   jax.ShapeDtypeStruct((n_seq, M_HEADS, M_HEAD_DIM, M_STATE), F32)],
        scratch_shapes=[pltpu.VMEM((CONV_W - 1, CONV_DIM), F32),
                        pltpu.VMEM((M_HEADS, M_HEAD_DIM, M_STATE), F32),
                        pltpu.VMEM((T, M_INNER), F32)],
        compiler_params=_cparams(("parallel", "arbitrary")),
        name="ssd",
    )(p, p, p, p, conv_state, ssm0, cw, cb, dtb, alog, dexp, norm, e16)


def _rwkv_kernel(r_ref, k_ref, v_ref, lo_ref, sh0_ref, s0_ref, mu_ref, w0_ref, w2_ref, a0_ref,
                 a2_ref, g2_ref, kkw_ref, kaw_ref, rk_ref, gng_ref, gnb_ref, y_ref, sout_ref,
                 last_ref, s_ref, yb_ref, *, T, n_valid):
    c = pl.program_id(1)
    nc = pl.num_programs(1)

    @pl.when(c == 0)
    def _():
        last_ref[...] = sh0_ref[0]
        s_ref[...] = s0_ref[0]

    rows = lax.broadcasted_iota(I32, (T, 1), 0)

    def mix(ref, lo, hi):
        x = ref[...]
        prev = jnp.where(rows == 0, last_ref[:, lo:hi], pltpu.roll(x, 1, 0))
        last_ref[:, lo:hi] = x[T - 1:T, :]
        return x + (prev - x) * mu_ref[:, lo:hi]

    r = mix(r_ref, 0, 1024)
    k = mix(k_ref, 1024, 2048)
    v = mix(v_ref, 2048, 3072)
    lo = mix(lo_ref, 3072, 3072 + LORA_W)
    xw = lo[:, 0:R_DECAY_LORA]
    xa = lo[:, R_DECAY_LORA:R_DECAY_LORA + R_A_LORA]
    xg = lo[:, R_DECAY_LORA + R_A_LORA:LORA_W]
    wl = -jax.nn.softplus(-(w0_ref[...] + _mm_hi(jnp.tanh(xw), w2_ref[...]))) - 0.5
    ld = -jnp.exp(wl)
    a = _sigmoid(a0_ref[...] + _mm_hi(xa, a2_ref[...]))
    gate = _mm_hi(_sigmoid(xg), g2_ref[...])
    kk = k * kkw_ref[...]
    km = k * (1.0 + (a - 1.0) * kaw_ref[...])
    if n_valid < T:
        ok = rows < n_valid
        ld = jnp.where(ok, ld, 0.0)
        kk = jnp.where(ok, kk, 0.0)
        km = jnp.where(ok, km, 0.0)
        v = jnp.where(ok, v, 0.0)

    ri = lax.broadcasted_iota(I32, (T, T), 0)
    ci = lax.broadcasted_iota(I32, (T, T), 1)
    incl = ri >= ci
    strict = ri > ci
    cum = _mm_hi(incl.astype(F32), ld)
    eg = jnp.exp(cum)
    egm = jnp.exp(cum - ld)
    ei = jnp.exp(-cum)
    el = jnp.exp(cum[T - 1:T, :] - cum)
    rkw = rk_ref[...]
    gng = gng_ref[...]
    gnb = gnb_ref[...]

    for h in range(R_HEADS):
        hs = slice(h * R_HEAD, (h + 1) * R_HEAD)
        kkh = kk[:, hs]
        nrm = jnp.sqrt(jnp.sum(kkh * kkh, axis=-1, keepdims=True))
        kkh = kkh / jnp.maximum(nrm, 1e-12)
        bh = kkh * a[:, hs]
        rh = r[:, hs]
        kmh = km[:, hs]
        vh = v[:, hs]
        q2 = jnp.concatenate([kkh * egm[:, hs], rh * eg[:, hs]], axis=0)
        k2 = jnp.concatenate([kmh * ei[:, hs], bh * ei[:, hs]], axis=0)
        gm = _mm_nt_hi(q2, k2)
        s0 = s_ref[h]
        hm = _mm_nt_hi(q2, s0)
        lk = jnp.where(strict, gm[0:T, 0:T], 0.0)
        lb = jnp.where(strict, gm[0:T, T:2 * T], 0.0)
        ak = jnp.where(incl, gm[T:2 * T, 0:T], 0.0)
        ab = jnp.where(incl, gm[T:2 * T, T:2 * T], 0.0)
        x = hm[0:T] + _mm_hi(lk, vh)
        zz = x - _mm_hi(lb, x)
        lp = lb
        span = 2
        while span < T:
            lp = _mm_hi(lp, lp)
            zz = zz + _mm_hi(lp, zz)
            span *= 2
        u = -zz
        y = hm[T:2 * T] + _mm_hi(ak, vh) + _mm_hi(ab, u)
        elh = el[:, hs]
        s_ref[h] = (s0 * eg[T - 1:T, hs] + _mm_tn_hi(vh, kmh * elh) + _mm_tn_hi(u, bh * elh))
        mu = jnp.mean(y, axis=-1, keepdims=True)
        var = jnp.mean(jnp.square(y - mu), axis=-1, keepdims=True)
        yn = (y - mu) * lax.rsqrt(var + GN_EPS) * gng[:, hs] + gnb[:, hs]
        yn = yn + jnp.sum(rh * kmh * rkw[:, hs], axis=-1, keepdims=True) * vh
        yb_ref[:, hs] = yn

    y_ref[...] = (yb_ref[...] * gate).astype(y_ref.dtype)

    @pl.when(c == nc - 1)
    def _():
        sout_ref[0] = s_ref[...]


def _rwkv(p, row0, n_seq, n_chunks, T, n_valid, shift0, s0, mu, w0, w2, a0, a2, g2, kkw, kaw, rk,
          gng, gnb):
    rb0 = row0 // T

    def rmap(col):
        return lambda s, c: (rb0 + s * n_chunks + c, col)

    sw = 3072 + LORA_W
    kern = functools.partial(_rwkv_kernel, T=T, n_valid=n_valid)
    return pl.pallas_call(
        kern,
        grid=(n_seq, n_chunks),
        in_specs=[pl.BlockSpec((T, 1024), rmap(C_RR // 1024)),
                  pl.BlockSpec((T, 1024), rmap(C_RK // 1024)),
                  pl.BlockSpec((T, 1024), rmap(C_RV // 1024)),
                  pl.BlockSpec((T, LORA_W), rmap(C_LORA // LORA_W)),
                  pl.BlockSpec((1, 1, sw), lambda s, c: (s, 0, 0)),
                  pl.BlockSpec((1, R_HEADS, R_HEAD, R_HEAD), lambda s, c: (s, 0, 0, 0)),
                  _full((1, sw)), _full((1, R_WIDTH)), _full((R_DECAY_LORA, R_WIDTH)),
                  _full((1, R_WIDTH)), _full((R_A_LORA, R_WIDTH)),
                  _full((LORA_W - R_DECAY_LORA - R_A_LORA, R_WIDTH)),
                  _full((1, R_WIDTH)), _full((1, R_WIDTH)), _full((1, R_WIDTH)),
                  _full((1, R_WIDTH)), _full((1, R_WIDTH))],
        out_specs=[pl.BlockSpec((T, R_WIDTH), lambda s, c: (s * n_chunks + c, 0)),
                   pl.BlockSpec((1, R_HEADS, R_HEAD, R_HEAD), lambda s, c: (s, 0, 0, 0))],
        out_shape=[jax.ShapeDtypeStruct((n_seq * n_chunks * T, R_WIDTH), BF16),
                   jax.ShapeDtypeStruct((n_seq, R_HEADS, R_HEAD, R_HEAD), F32)],
        scratch_shapes=[pltpu.VMEM((1, sw), F32),
                        pltpu.VMEM((R_HEADS, R_HEAD, R_HEAD), F32),
                        pltpu.VMEM((T, R_WIDTH), F32)],
        compiler_params=_cparams(("parallel", "arbitrary")),
        name="rwkv",
    )(p, p, p, p, shift0, s0, mu, w0, w2, a0, a2, g2, kkw, kaw, rk, gng, gnb)


def _score_keys(acc, causal):
    bits = pltpu.bitcast(acc, I32)
    key = bits ^ ((bits >> 31) & 0x7FFFFFFF)
    return jnp.where(causal, key, INT_MIN)


def _kth_threshold(count_ge, shape, top):
    tb = jnp.zeros(shape, I32)
    for bit in range(31, -1, -1):
        inc = INT_MIN if bit == 31 else (1 << bit)
        cand_b = tb | jnp.int32(inc)
        cnt = count_ge(cand_b ^ jnp.int32(INT_MIN))
        tb = jnp.where(cnt >= top, cand_b, tb)
    return jnp.maximum(tb ^ jnp.int32(INT_MIN), INT_MIN + 1)


def _toeplitz_bias(base_row, nrows):
    return pltpu.roll(jnp.broadcast_to(base_row, (nrows, 2 * LANE)), 0, 1, stride=1, stride_axis=0)


def _dsa_prompt_kernel(q_ref, qi_ref, misc_ref, kit_ref, kt_ref, va_ref, base_ref, o_ref,
                       key_ref, m_ref, acc_ref, *, top, TA):
    i = pl.program_id(0)
    QB = LANE
    q0 = i * QB
    n_a = (q0 + QB + TA - 1) // TA
    wi = misc_ref[:, MISC_WI:MISC_WI + IDX_HEADS] * (IDX_HEADS ** -0.5 * IDX_DIM ** -0.5)
    qi = qi_ref[...]
    qis = jnp.concatenate([qi[:, h * IDX_DIM:(h + 1) * IDX_DIM] for h in range(IDX_HEADS)],
                          axis=0).astype(BF16)
    rowq = q0 + lax.broadcasted_iota(I32, (QB, 1), 0)

    def score_tile(kt, carry):
        off = pl.multiple_of(kt * TA, TA)
        s = jnp.maximum(jnp.dot(qis, kit_ref[:, pl.ds(off, TA)], preferred_element_type=F32), 0.0)
        acc = s[0:QB] * wi[:, 0:1]
        for h in range(1, IDX_HEADS):
            acc = acc + s[h * QB:(h + 1) * QB] * wi[:, h:h + 1]
        col = off + lax.broadcasted_iota(I32, (1, TA), 1)
        key_ref[:, pl.ds(off, TA)] = _score_keys(acc, col <= rowq)
        return carry

    lax.fori_loop(0, n_a, score_tile, 0)

    def count_ge(cand):
        def body(kt, cnt):
            off = pl.multiple_of(kt * TA, TA)
            ge = jnp.where(key_ref[:, pl.ds(off, TA)] >= cand, 1.0, 0.0)
            for j in range(TA // LANE):
                cnt = cnt + ge[:, j * LANE:(j + 1) * LANE]
            return cnt
        cnt = lax.fori_loop(0, n_a, body, jnp.zeros((QB, LANE), F32))
        return jnp.sum(cnt, axis=-1, keepdims=True)

    thr = _kth_threshold(count_ge, (QB, 1), float(top))

    q = q_ref[...]
    for g in range(KV_HEADS):
        heads = range(g * GQA, (g + 1) * GQA)
        qg = (jnp.concatenate([q[:, h * HEAD_DIM:(h + 1) * HEAD_DIM] for h in heads], axis=0)
              * HEAD_DIM ** -0.5).astype(BF16)
        near = jnp.concatenate([_toeplitz_bias(base_ref[h:h + 1, :], QB) for h in heads], axis=0)
        far = jnp.concatenate([jnp.broadcast_to(base_ref[h:h + 1, 2 * LANE - 1:2 * LANE], (QB, 1))
                               for h in heads], axis=0)
        m_ref[...] = jnp.full(m_ref.shape, NEG_BIG, F32)
        acc_ref[...] = jnp.zeros(acc_ref.shape, F32)

        def attend(kt, bias):
            off = pl.multiple_of(kt * LANE, LANE)
            s = jnp.dot(qg, kt_ref[g, :, pl.ds(off, LANE)], preferred_element_type=F32) + bias
            msk = jnp.where(key_ref[:, pl.ds(off, LANE)] >= thr, 0.0, NEG_BIG)
            s = s + jnp.concatenate([msk] * GQA, axis=0)
            m_old = m_ref[...]
            m_new = jnp.maximum(m_old, jnp.max(s, axis=-1, keepdims=True))
            pr = jnp.exp(s - m_new)
            acc_ref[...] = (acc_ref[...] * jnp.exp(m_old - m_new)
                            + jnp.dot(pr.astype(BF16), va_ref[g, pl.ds(off, LANE), :],
                                      preferred_element_type=F32))
            m_ref[...] = m_new

        def far_body(kt, carry):
            attend(kt, far)
            return carry

        lax.fori_loop(0, jnp.maximum(i - 1, 0), far_body, 0)

        @pl.when(i >= 1)
        def _():
            attend(i - 1, near[:, 0:LANE])

        attend(i, near[:, LANE:2 * LANE])
        acc = acc_ref[...]
        out = acc[:, 0:HEAD_DIM] / acc[:, HEAD_DIM:HEAD_DIM + 1]
        for hh, h in enumerate(heads):
            o_ref[:, h * HEAD_DIM:(h + 1) * HEAD_DIM] = out[hh * QB:(hh + 1) * QB].astype(o_ref.dtype)


def _dsa_prompt(p, kit, kt, va, base, S):
    top = min(TOPK_MAX, S // 4)
    TA = min(512, S)
    kern = functools.partial(_dsa_prompt_kernel, top=top, TA=TA)
    return pl.pallas_call(
        kern,
        grid=(S // LANE,),
        in_specs=[pl.BlockSpec((LANE, 1024), lambda i: (i, C_Q // 1024)),
                  pl.BlockSpec((LANE, 1024), lambda i: (i, C_QI // 1024)),
                  pl.BlockSpec((LANE, LANE), lambda i: (i, C_MISC // LANE)),
                  _full((IDX_DIM, S)), _full((KV_HEADS, HEAD_DIM, S)), _full((KV_HEADS, S, LANE)),
                  _full((ATT_HEADS, 2 * LANE))],
        out_specs=pl.BlockSpec((LANE, MIX_W), lambda i: (i, 0)),
        out_shape=jax.ShapeDtypeStruct((S, MIX_W), BF16),
        scratch_shapes=[pltpu.VMEM((LANE, S), I32),
                        pltpu.VMEM((GQA * LANE, 1), F32),
                        pltpu.VMEM((GQA * LANE, LANE), F32)],
        compiler_params=_cparams(("arbitrary",)),
        name="dsa_prompt",
    )(p, p, p, kit, kt, va, base)


def _dsa_sample_kernel(pt_ref, q_ref, qi_ref, misc_ref, kn_ref, vn_ref, kin_ref, base_ref, *rest,
                       n_pages, n_valid, top):
    ki_pages = rest[0:n_pages]
    k_pages = rest[n_pages:2 * n_pages]
    v_pages = rest[2 * n_pages:3 * n_pages]
    o_ref = rest[3 * n_pages]
    key_ref, lg_ref = rest[3 * n_pages + 1:]
    TQ = SEQ_PAD
    NT = n_pages + 1
    wi = misc_ref[:, MISC_WI:MISC_WI + IDX_HEADS] * (IDX_HEADS ** -0.5 * IDX_DIM ** -0.5)
    qi = qi_ref[...]
    qis = jnp.concatenate([qi[:, h * IDX_DIM:(h + 1) * IDX_DIM] for h in range(IDX_HEADS)], axis=0)
    trow = lax.broadcasted_iota(I32, (TQ, 1), 0)
    lcol = lax.broadcasted_iota(I32, (1, LANE), 1)
    pad0 = jnp.zeros((LANE - TQ, 1), F32)

    def pad_rows(x):
        return jnp.concatenate([x, jnp.broadcast_to(pad0, (LANE - TQ, x.shape[1]))], axis=0)

    ki_new = pad_rows(kin_ref[...])
    k_new = pad_rows(kn_ref[...])
    v_new = pad_rows(vn_ref[...])

    def tile_ki(t):
        return ki_pages[t][0] if t < n_pages else ki_new

    def tile_k(t):
        return k_pages[t][0] if t < n_pages else k_new

    def tile_v(t):
        return v_pages[t][0] if t < n_pages else v_new

    for t in range(NT):
        s = jnp.maximum(_mm_nt(qis, tile_ki(t)), 0.0)
        acc = s[0:TQ] * wi[:, 0:1]
        for h in range(1, IDX_HEADS):
            acc = acc + s[h * TQ:(h + 1) * TQ] * wi[:, h:h + 1]
        causal = (lcol >= 0) if t < n_pages else ((lcol <= trow) & (lcol < n_valid))
        key_ref[:, t * LANE:(t + 1) * LANE] = _score_keys(acc, jnp.broadcast_to(causal, (TQ, LANE)))

    keys = key_ref[...]

    def count_ge(cand):
        return jnp.sum(jnp.where(keys >= cand, 1.0, 0.0), axis=-1, keepdims=True)

    thr = _kth_threshold(count_ge, (TQ, 1), float(top))
    msk = jnp.where(keys >= thr, 0.0, NEG_BIG)
    msk = jnp.concatenate([msk] * GQA, axis=0)

    q = q_ref[...]
    for g in range(KV_HEADS):
        heads = range(g * GQA, (g + 1) * GQA)
        gs = slice(g * HEAD_DIM, (g + 1) * HEAD_DIM)
        qg = (jnp.concatenate([q[:, h * HEAD_DIM:(h + 1) * HEAD_DIM] for h in heads], axis=0)
              * HEAD_DIM ** -0.5)
        near = jnp.concatenate([_toeplitz_bias(base_ref[h:h + 1, :], TQ) for h in heads], axis=0)
        far = jnp.concatenate([jnp.broadcast_to(base_ref[h:h + 1, 2 * LANE - 1:2 * LANE], (TQ, 1))
                               for h in heads], axis=0)
        for t in range(NT):
            if t < n_pages - 1:
                bias = far
            elif t == n_pages - 1:
                bias = near[:, 0:LANE]
            else:
                bias = near[:, LANE:2 * LANE]
            lg_ref[:, t * LANE:(t + 1) * LANE] = _mm_nt(qg, tile_k(t)[:, gs]) + bias
        s = lg_ref[...] + msk
        mx = jnp.max(s, axis=-1, keepdims=True)
        pr = jnp.exp(s - mx)
        den = jnp.sum(pr, axis=-1, keepdims=True)
        pr = pr / den
        out = _mm(pr[:, 0:LANE], tile_v(0)[:, gs])
        for t in range(1, NT):
            out = out + _mm(pr[:, t * LANE:(t + 1) * LANE], tile_v(t)[:, gs])
        for hh, h in enumerate(heads):
            o_ref[:, h * HEAD_DIM:(h + 1) * HEAD_DIM] = out[hh * TQ:(hh + 1) * TQ].astype(o_ref.dtype)


def _dsa_sample(p, row0, n_seq, kidx_ln, base, cache_k, cache_v, cache_kidx, page_table, n_valid):
    n_pages = page_table.shape[1]
    past = n_pages * PAGE_SIZE
    top = min(TOPK_MAX, (past + n_valid) // 4)
    rb0 = row0 // SEQ_PAD
    n_pool = cache_k.shape[0]
    ck = cache_k.reshape(n_pool, PAGE_SIZE, KV_HEADS * HEAD_DIM)
    cv = cache_v.reshape(n_pool, PAGE_SIZE, KV_HEADS * HEAD_DIM)

    def rmap(col):
        return lambda b, pt: (rb0 + b, col)

    def pmap(pg):
        return lambda b, pt: (pt[b, pg], 0, 0)

    in_specs = [pl.BlockSpec((SEQ_PAD, 1024), rmap(C_Q // 1024)),
                pl.BlockSpec((SEQ_PAD, 1024), rmap(C_QI // 1024)),
                pl.BlockSpec((SEQ_PAD, LANE), rmap(C_MISC // LANE)),
                pl.BlockSpec((SEQ_PAD, 256), rmap(C_K // 256)),
                pl.BlockSpec((SEQ_PAD, 256), rmap(C_V // 256)),
                pl.BlockSpec((SEQ_PAD, IDX_DIM), lambda b, pt: (rb0 + b, 0)),
                pl.BlockSpec((ATT_HEADS, 2 * LANE), lambda b, pt: (0, 0))]
    in_specs += [pl.BlockSpec((1, PAGE_SIZE, IDX_DIM), pmap(pg)) for pg in range(n_pages)]
    in_specs += [pl.BlockSpec((1, PAGE_SIZE, KV_HEADS * HEAD_DIM), pmap(pg)) for pg in range(n_pages)]
    in_specs += [pl.BlockSpec((1, PAGE_SIZE, KV_HEADS * HEAD_DIM), pmap(pg)) for pg in range(n_pages)]
    kern = functools.partial(_dsa_sample_kernel, n_pages=n_pages, n_valid=n_valid, top=top)
    nk = (n_pages + 1) * LANE
    return pl.pallas_call(
        kern,
        grid_spec=pltpu.PrefetchScalarGridSpec(
            num_scalar_prefetch=1,
            grid=(n_seq,),
            in_specs=in_specs,
            out_specs=pl.BlockSpec((SEQ_PAD, MIX_W), lambda b, pt: (b, 0)),
            scratch_shapes=[pltpu.VMEM((SEQ_PAD, nk), I32),
                            pltpu.VMEM((GQA * SEQ_PAD, nk), F32)]),
        out_shape=jax.ShapeDtypeStruct((n_seq * SEQ_PAD, MIX_W), BF16),
        compiler_params=_cparams(("arbitrary",)),
        name="dsa_sample",
    )(page_table, p, p, p, p, p, kidx_ln, base,
      *([cache_kidx] * n_pages), *([ck] * n_pages), *([cv] * n_pages))


def _t5_base(t5_table):
    x = jnp.arange(2 * LANE, dtype=jnp.int32)
    rel = jnp.where(x <= LANE, LANE - x, T5_MAX_DIST)
    n = jnp.maximum(rel, 0)
    exact = T5_BUCKETS // 2
    nf = jnp.maximum(n, exact).astype(F32)
    large = exact + (jnp.log(nf / exact) / math.log(T5_MAX_DIST / exact)
                     * (T5_BUCKETS - exact)).astype(jnp.int32)
    bucket = jnp.where(n < exact, n, jnp.minimum(large, T5_BUCKETS - 1))
    return t5_table[bucket].astype(F32).T


def _relayout_w_in(w):
    widths = (M_INNER, CONV_DIM, M_HEADS, ATT_HEADS * HEAD_DIM, KV_HEADS * HEAD_DIM,
              KV_HEADS * HEAD_DIM, IDX_HEADS * IDX_DIM, IDX_DIM, IDX_HEADS, R_PROJ, 3 * D_MODEL)
    offs = [0]
    for wd in widths:
        offs.append(offs[-1] + wd)
    z, xbc, dt, q, k, v, qi, ki, wi, pr, gates = (w[:, offs[i]:offs[i + 1]] for i in range(len(widths)))
    zeros = lambda n: jnp.zeros((w.shape[0], n), w.dtype)
    lora = pr[:, 3 * R_WIDTH:]
    cols = [z, xbc, q, qi, pr[:, :3 * R_WIDTH], gates,
            lora, zeros(LORA_W - lora.shape[1]), k, v,
            ki, dt, wi, zeros(LANE - IDX_DIM - M_HEADS - IDX_HEADS)]
    out = jnp.concatenate(cols, axis=1).astype(BF16)
    assert out.shape[1] == P_W
    return out


def _pad_cols(x, n):
    return jnp.concatenate([x, jnp.zeros(x.shape[:-1] + (n - x.shape[-1],), x.dtype)], axis=-1)


def _shift_layout(x):
    return _pad_cols(x, 3 * R_WIDTH + LORA_W)


def kernel(x_prompt, x_sample, cache_k, cache_v, cache_kidx, page_table, state_ssm, state_conv, state_wkv, state_shift, ffn1_norm, ffn1_gate, ffn1_up, ffn1_down, mix_norm, w_in, conv_w, conv_b, dt_bias, a_log, d_skip, ssm_norm, kidx_ln_g, kidx_ln_b, t5_table, rwkv_mu, rwkv_w0, rwkv_w2, rwkv_a0, rwkv_a2, rwkv_g2, rwkv_kk, rwkv_ka, rwkv_rk, rwkv_gn_g, rwkv_gn_b, w_br_a, w_br_b, w_br_c, w_out, ffn2_norm, ffn2_gate, ffn2_up, ffn2_down, final_norm):
    bp, S, d = x_prompt.shape
    nb, T_dec, _ = x_sample.shape
    depth = w_in.shape[0]
    assert bp == 1 and T_dec <= SEQ_PAD and S % SSD_CHUNK == 0
    n_p = bp * S
    n_rows = n_p + nb * SEQ_PAD
    xs_pad = jnp.concatenate([x_sample, jnp.zeros((nb, SEQ_PAD - T_dec, d), x_sample.dtype)], axis=1)
    h = jnp.concatenate([x_prompt.reshape(n_p, d), xs_pad.reshape(nb * SEQ_PAD, d)], axis=0)

    base = _t5_base(t5_table)
    e16 = jnp.repeat(jnp.eye(M_HEADS, dtype=F32), M_HEAD_DIM, axis=1)
    row = lambda x: x.reshape(1, -1).astype(F32)
    rc = min(RWKV_RC_CHUNK, S)
    outs = {k_: [] for k_ in ("k_p", "v_p", "ki_p", "ssm_p", "conv_p", "wkv_p", "sh_p",
                              "k_s", "v_s", "ki_s", "ssm_s", "conv_s", "wkv_s", "sh_s")}
    for l in range(depth):
        bf = lambda x: x.astype(BF16)
        h = _ffn(h, row(ffn1_norm[l]), bf(ffn1_gate[l]), bf(ffn1_up[l]), bf(ffn1_down[l]))
        p = _inproj(h, row(mix_norm[l]), _relayout_w_in(w_in[l]))
        kidx = _kidx_ln(p, row(kidx_ln_g[l]), row(kidx_ln_b[l]))

        ssd_par = (conv_w[l], row(conv_b[l]), row(dt_bias[l]), row(a_log[l]),
                   row(jnp.repeat(d_skip[l], M_HEAD_DIM)), row(ssm_norm[l]), e16)
        ya_p, ssm_p = _ssd(p, 0, bp, S // SSD_CHUNK, SSD_CHUNK, SSD_CHUNK,
                           jnp.zeros((bp, CONV_W - 1, CONV_DIM), F32),
                           jnp.zeros((bp, M_HEADS, M_HEAD_DIM, M_STATE), F32), *ssd_par)
        ya_s, ssm_s = _ssd(p, n_p, nb, 1, SEQ_PAD, T_dec, state_conv[l], state_ssm[l], *ssd_par)

        g2p = jnp.concatenate([rwkv_g2[l], jnp.zeros((LORA_W - R_DECAY_LORA - R_A_LORA - R_G_LORA,
                                                      R_WIDTH), F32)], axis=0)
        rw_par = (row(_shift_layout(rwkv_mu[l])), row(rwkv_w0[l]), rwkv_w2[l], row(rwkv_a0[l]),
                  rwkv_a2[l], g2p, row(rwkv_kk[l]), row(rwkv_ka[l]), row(rwkv_rk[l]),
                  row(rwkv_gn_g[l]), row(rwkv_gn_b[l]))
        sw = 3 * R_WIDTH + LORA_W
        yc_p, wkv_p = _rwkv(p, 0, bp, S // rc, rc, rc, jnp.zeros((bp, 1, sw), F32),
                            jnp.zeros((bp, R_HEADS, R_HEAD, R_HEAD), F32), *rw_par)
        yc_s, wkv_s = _rwkv(p, n_p, nb, 1, SEQ_PAD, T_dec,
                            _shift_layout(state_shift[l]).reshape(nb, 1, sw), state_wkv[l], *rw_par)

        kp = p[:n_p, C_K:C_K + 256]
        vp = p[:n_p, C_V:C_V + 256]
        kit = bf(kidx[:n_p].T)
        kt = bf(kp.reshape(n_p, KV_HEADS, HEAD_DIM).transpose(1, 2, 0))
        vg = vp.reshape(n_p, KV_HEADS, HEAD_DIM).transpose(1, 0, 2)
        va = bf(jnp.concatenate([vg, jnp.ones((KV_HEADS, n_p, 1), F32),
                                 jnp.zeros((KV_HEADS, n_p, LANE - HEAD_DIM - 1), F32)], axis=-1))
        yb_p = _dsa_prompt(p, kit, kt, va, base, S)
        yb_s = _dsa_sample(p, n_p, nb, kidx, base, cache_k[l], cache_v[l], cache_kidx[l],
                           page_table, T_dec)

        ya = jnp.concatenate([ya_p, ya_s], axis=0)
        yb = jnp.concatenate([yb_p, yb_s], axis=0)
        yc = jnp.concatenate([yc_p, yc_s], axis=0)
        h = _merge(h, p, ya, yb, yc, bf(w_br_a[l]), bf(w_br_b[l]), bf(w_br_c[l]), bf(w_out[l]))
        h = _ffn(h, row(ffn2_norm[l]), bf(ffn2_gate[l]), bf(ffn2_up[l]), bf(ffn2_down[l]))

        ps = p[n_p:].reshape(nb, SEQ_PAD, P_W)[:, :T_dec]
        xbc = lambda t: jnp.concatenate([t[..., C_XS:C_XS + 1024], t[..., C_BC:C_BC + 1024]], axis=-1)
        prj = lambda t: jnp.concatenate([t[..., C_RR:C_RR + 3 * R_WIDTH],
                                         t[..., C_LORA:C_LORA + R_PROJ - 3 * R_WIDTH]], axis=-1)
        outs["k_p"].append(kp.reshape(bp, S, KV_HEADS, HEAD_DIM))
        outs["v_p"].append(vp.reshape(bp, S, KV_HEADS, HEAD_DIM))
        outs["ki_p"].append(kidx[:n_p].reshape(bp, S, IDX_DIM))
        outs["ssm_p"].append(ssm_p)
        outs["conv_p"].append(xbc(p[n_p - (CONV_W - 1):n_p]).reshape(bp, CONV_W - 1, CONV_DIM))
        outs["wkv_p"].append(wkv_p)
        outs["sh_p"].append(prj(p[n_p - 1:n_p]).reshape(bp, R_PROJ))
        outs["k_s"].append(ps[..., C_K:C_K + 256].reshape(nb, T_dec, KV_HEADS, HEAD_DIM))
        outs["v_s"].append(ps[..., C_V:C_V + 256].reshape(nb, T_dec, KV_HEADS, HEAD_DIM))
        outs["ki_s"].append(kidx[n_p:].reshape(nb, SEQ_PAD, IDX_DIM)[:, :T_dec])
        outs["ssm_s"].append(ssm_s)
        conv_full = jnp.concatenate([state_conv[l], xbc(ps)], axis=1)
        outs["conv_s"].append(conv_full[:, T_dec:])
        outs["wkv_s"].append(wkv_s)
        outs["sh_s"].append(prj(ps[:, T_dec - 1]))

    y = _final_norm(h, row(final_norm))
    y_prompt = y[:n_p].reshape(bp, S, d)
    y_sample = y[n_p:].reshape(nb, SEQ_PAD, d)[:, :T_dec]
    stk = lambda name: jnp.stack(outs[name], axis=0)
    return (y_prompt, y_sample, stk("k_p"), stk("v_p"), stk("ki_p"), stk("ssm_p"), stk("conv_p"),
            stk("wkv_p"), stk("sh_p"), stk("k_s"), stk("v_s"), stk("ki_s"), stk("ssm_s"),
            stk("conv_s"), stk("wkv_s"), stk("sh_s"))
```

```python
import functools
import math

import jax
import jax.numpy as jnp
from jax import lax
from jax.experimental import pallas as pl
from jax.experimental.pallas import tpu as pltpu

F32 = jnp.float32
BF16 = jnp.bfloat16
I32 = jnp.int32

D_MODEL = 2048
MIX_W = D_MODEL // 2
M_HEAD_DIM = 64
M_INNER = MIX_W
M_HEADS = M_INNER // M_HEAD_DIM
M_GROUPS = 4
M_STATE = 128
CONV_W = 4
CONV_DIM = M_INNER + 2 * M_GROUPS * M_STATE
SSD_CHUNK = 128
HEAD_DIM = 64
ATT_HEADS = MIX_W // HEAD_DIM
KV_HEADS = 4
GQA = ATT_HEADS // KV_HEADS
IDX_HEADS = 16
IDX_DIM = 64
TOPK_MAX = 256
T5_BUCKETS = 32
T5_MAX_DIST = 128
R_HEAD = 64
R_WIDTH = MIX_W
R_HEADS = R_WIDTH // R_HEAD
R_DECAY_LORA = 64
R_A_LORA = 64
R_G_LORA = 160
R_PROJ = 3 * R_WIDTH + R_DECAY_LORA + R_A_LORA + R_G_LORA
GN_EPS = 64e-5
D_FF = 256 * ((8 * D_MODEL // 3 + 255) // 256)
NORM_EPS = 1e-6
PAGE_SIZE = 128

SEQ_PAD = 8
LANE = 128
MXU_DIM = 256
VMEM_LIMIT = 56 * 2**20
INT_MIN = -2**31
NEG_BIG = -1e30

C_Z, C_XS, C_BC, C_Q, C_QI, C_RR, C_RK, C_RV = (i * 1024 for i in range(8))
C_GATE = 8192
C_LORA = C_GATE + 3 * D_MODEL
LORA_W = 512
C_K = C_LORA + LORA_W
C_V = C_K + 256
C_MISC = C_V + 256
P_W = C_MISC + LANE
MISC_DT = IDX_DIM
MISC_WI = IDX_DIM + M_HEADS
RWKV_RC_CHUNK = 64

HI = lax.Precision.HIGHEST


def _cparams(sem):
    return pltpu.CompilerParams(dimension_semantics=sem, vmem_limit_bytes=VMEM_LIMIT)


def _mm(a, b):
    return jnp.dot(a.astype(BF16), b.astype(BF16), preferred_element_type=F32)


def _mm_nt(a, b):
    return lax.dot_general(a.astype(BF16), b.astype(BF16), (((1,), (1,)), ((), ())),
                           preferred_element_type=F32)


def _mm_tn(a, b):
    return lax.dot_general(a.astype(BF16), b.astype(BF16), (((0,), (0,)), ((), ())),
                           preferred_element_type=F32)


def _mm_hi(a, b):
    return jnp.dot(a, b, precision=HI, preferred_element_type=F32)


def _mm_nt_hi(a, b):
    return lax.dot_general(a, b, (((1,), (1,)), ((), ())), precision=HI, preferred_element_type=F32)


def _mm_tn_hi(a, b):
    return lax.dot_general(a, b, (((0,), (0,)), ((), ())), precision=HI, preferred_element_type=F32)


def _sigmoid(x):
    return jax.nn.sigmoid(x)


def _silu(x):
    return x * jax.nn.sigmoid(x)


def _full(shape):
    nd = len(shape)
    return pl.BlockSpec(shape, lambda *_: (0,) * nd)


def _ffn_kernel(x_ref, g_ref, wg_ref, wu_ref, wd_ref, o_ref, n_ref):
    f = pl.program_id(1)

    @pl.when(f == 0)
    def _():
        x = x_ref[...]
        ms = jnp.mean(x * x, axis=-1, keepdims=True)
        n_ref[...] = (x * lax.rsqrt(ms + NORM_EPS) * g_ref[...]).astype(BF16)
        o_ref[...] = x

    n = n_ref[...]
    a = jnp.dot(n, wg_ref[...], preferred_element_type=F32)
    b = jnp.dot(n, wu_ref[...], preferred_element_type=F32)
    hid = (0.5 * _silu(a) * b).astype(BF16)
    o_ref[...] += jnp.dot(hid, wd_ref[...], preferred_element_type=F32)


def _ffn(x, g, wg, wu, wd, tm=512, tf=512):
    m = x.shape[0]
    tm = min(tm, m)
    return pl.pallas_call(
        _ffn_kernel,
        grid=(m // tm, D_FF // tf),
        in_specs=[pl.BlockSpec((tm, D_MODEL), lambda i, f: (i, 0)),
                  pl.BlockSpec((1, D_MODEL), lambda i, f: (0, 0)),
                  pl.BlockSpec((D_MODEL, tf), lambda i, f: (0, f)),
                  pl.BlockSpec((D_MODEL, tf), lambda i, f: (0, f)),
                  pl.BlockSpec((tf, D_MODEL), lambda i, f: (f, 0))],
        out_specs=pl.BlockSpec((tm, D_MODEL), lambda i, f: (i, 0)),
        out_shape=jax.ShapeDtypeStruct((m, D_MODEL), F32),
        scratch_shapes=[pltpu.VMEM((tm, D_MODEL), BF16)],
        compiler_params=_cparams(("parallel", "arbitrary")),
        name="ffn",
    )(x, g, wg, wu, wd)


def _inproj_kernel(x_ref, g_ref, w_ref, o_ref, n_ref):
    j = pl.program_id(1)

    @pl.when(j == 0)
    def _():
        x = x_ref[...]
        ms = jnp.mean(x * x, axis=-1, keepdims=True)
        n_ref[...] = (x * lax.rsqrt(ms + NORM_EPS) * g_ref[...]).astype(BF16)

    o_ref[...] = jnp.dot(n_ref[...], w_ref[...], preferred_element_type=F32)


def _inproj(x, g, w, tm=512, tn=1408):
    m = x.shape[0]
    tm = min(tm, m)
    return pl.pallas_call(
        _inproj_kernel,
        grid=(m // tm, P_W // tn),
        in_specs=[pl.BlockSpec((tm, D_MODEL), lambda i, j: (i, 0)),
                  pl.BlockSpec((1, D_MODEL), lambda i, j: (0, 0)),
                  pl.BlockSpec((D_MODEL, tn), lambda i, j: (0, j))],
        out_specs=pl.BlockSpec((tm, tn), lambda i, j: (i, j)),
        out_shape=jax.ShapeDtypeStruct((m, P_W), F32),
        scratch_shapes=[pltpu.VMEM((tm, D_MODEL), BF16)],
        compiler_params=_cparams(("parallel", "arbitrary")),
        name="inproj",
    )(x, g, w)


def _merge_kernel(h_ref, ya_ref, yb_ref, yc_ref, ga_ref, gb_ref, gc_ref,
                  wa_ref, wb_ref, wc_ref, wo_ref, o_ref):
    j = pl.program_id(1)

    @pl.when(j == 0)
    def _():
        o_ref[...] = h_ref[...]

    merged = (_sigmoid(ga_ref[...]) * jnp.dot(ya_ref[...], wa_ref[...], preferred_element_type=F32)
              + _sigmoid(gb_ref[...]) * jnp.dot(yb_ref[...], wb_ref[...], preferred_element_type=F32)
              + _sigmoid(gc_ref[...]) * jnp.dot(yc_ref[...], wc_ref[...], preferred_element_type=F32))
    o_ref[...] += jnp.dot(merged.astype(BF16), wo_ref[...], preferred_element_type=F32)


def _merge(h, p, ya, yb, yc, wa, wb, wc, wo, tm=512, tk=512):
    m = h.shape[0]
    tm = min(tm, m)
    gb0 = C_GATE // tk
    nk = D_MODEL // tk
    yspec = pl.BlockSpec((tm, MIX_W), lambda i, j: (i, 0))
    wspec = pl.BlockSpec((MIX_W, tk), lambda i, j: (0, j))
    return pl.pallas_call(
        _merge_kernel,
        grid=(m // tm, nk),
        in_specs=[pl.BlockSpec((tm, D_MODEL), lambda i, j: (i, 0)), yspec, yspec, yspec,
                  pl.BlockSpec((tm, tk), lambda i, j: (i, gb0 + j)),
                  pl.BlockSpec((tm, tk), lambda i, j: (i, gb0 + nk + j)),
                  pl.BlockSpec((tm, tk), lambda i, j: (i, gb0 + 2 * nk + j)),
                  wspec, wspec, wspec,
                  pl.BlockSpec((tk, D_MODEL), lambda i, j: (j, 0))],
        out_specs=pl.BlockSpec((tm, D_MODEL), lambda i, j: (i, 0)),
        out_shape=jax.ShapeDtypeStruct((m, D_MODEL), F32),
        compiler_params=_cparams(("parallel", "arbitrary")),
        name="merge",
    )(h, ya, yb, yc, p, p, p, wa, wb, wc, wo)


def _final_norm_kernel(x_ref, g_ref, o_ref):
    x = x_ref[...]
    ms = jnp.mean(x * x, axis=-1, keepdims=True)
    o_ref[...] = x * lax.rsqrt(ms + NORM_EPS) * g_ref[...]


def _final_norm(x, g, tm=512):
    m = x.shape[0]
    tm = min(tm, m)
    return pl.pallas_call(
        _final_norm_kernel,
        grid=(m // tm,),
        in_specs=[pl.BlockSpec((tm, D_MODEL), lambda i: (i, 0)), _full((1, D_MODEL))],
        out_specs=pl.BlockSpec((tm, D_MODEL), lambda i: (i, 0)),
        out_shape=jax.ShapeDtypeStruct((m, D_MODEL), F32),
        compiler_params=_cparams(("parallel",)),
        name="final_norm",
    )(x, g)


def _kidx_ln_kernel(m_ref, g_ref, b_ref, o_ref):
    x = m_ref[:, 0:IDX_DIM]
    mu = jnp.mean(x, axis=-1, keepdims=True)
    var = jnp.mean(jnp.square(x - mu), axis=-1, keepdims=True)
    o_ref[...] = (x - mu) * lax.rsqrt(var + NORM_EPS) * g_ref[...] + b_ref[...]


def _kidx_ln(p, g, b, tm=512):
    m = p.shape[0]
    tm = min(tm, m)
    return pl.pallas_call(
        _kidx_ln_kernel,
        grid=(m // tm,),
        in_specs=[pl.BlockSpec((tm, LANE), lambda i: (i, C_MISC // LANE)),
                  _full((1, IDX_DIM)), _full((1, IDX_DIM))],
        out_specs=pl.BlockSpec((tm, IDX_DIM), lambda i: (i, 0)),
        out_shape=jax.ShapeDtypeStruct((m, IDX_DIM), F32),
        compiler_params=_cparams(("parallel",)),
        name="kidx_ln",
    )(p, g, b)


def _shifted(x, k, prev, rows):
    y = pltpu.roll(x, k, 0)
    for r in range(k):
        y = jnp.where(rows == r, prev[3 - k + r:4 - k + r, :], y)
    return y


def _ssd_kernel(z_ref, xs_ref, bc_ref, misc_ref, cst_ref, st0_ref, cw_ref, cb_ref, dtb_ref,
                alog_ref, dexp_ref, norm_ref, e16_ref, y_ref, stout_ref,
                prev_ref, st_ref, yd_ref, *, T, n_valid):
    c = pl.program_id(1)
    nc = pl.num_programs(1)

    @pl.when(c == 0)
    def _():
        prev_ref[...] = cst_ref[0, 0]
        st_ref[...] = st0_ref[0, 0]

    rows = lax.broadcasted_iota(I32, (T, 1), 0)
    cw = cw_ref[...]
    prev = prev_ref[...]

    def conv(x, lo, hi):
        pv = prev[:, lo:hi]
        y = cb_ref[:, lo:hi] + x * cw[3:4, lo:hi]
        for k in (1, 2, 3):
            y = y + _shifted(x, k, pv, rows) * cw[3 - k:4 - k, lo:hi]
        return _silu(y)

    xs_raw = xs_ref[...]
    bc_raw = bc_ref[...]
    xs = conv(xs_raw, 0, M_INNER)
    bc = conv(bc_raw, M_INNER, CONV_DIM)
    prev_ref[:, 0:M_INNER] = xs_raw[T - 3:T, :]
    prev_ref[:, M_INNER:CONV_DIM] = bc_raw[T - 3:T, :]

    dt = jax.nn.softplus(misc_ref[:, MISC_DT:MISC_DT + M_HEADS] + dtb_ref[...])
    if n_valid < T:
        dt = jnp.where(rows < n_valid, dt, 0.0)
    ad = dt * (-jnp.exp(alog_ref[...]))
    ri = lax.broadcasted_iota(I32, (T, T), 0)
    ci = lax.broadcasted_iota(I32, (T, T), 1)
    lower = ri >= ci
    tril = lower.astype(F32)
    triu = (ri <= ci).astype(F32)
    eye16 = (lax.broadcasted_iota(I32, (M_HEADS, M_HEADS), 0)
             == lax.broadcasted_iota(I32, (M_HEADS, M_HEADS), 1)).astype(F32)
    cs = _mm_hi(tril, ad)
    cst = _mm_hi(_mm_nt_hi(eye16, ad), triu)
    e16 = e16_ref[...]
    dt_e = _mm_hi(dt, e16)
    ecs_e = _mm_hi(jnp.exp(cs), e16)
    wl_e = _mm_hi(jnp.exp(cs[T - 1:T, :] - cs), e16)
    xd = xs * dt_e
    xdw = (xd * wl_e).astype(BF16)
    xd = xd.astype(BF16)
    bcb = bc.astype(BF16)
    for g in range(M_GROUPS):
        bg = bcb[:, g * M_STATE:(g + 1) * M_STATE]
        cg = bcb[:, (M_GROUPS + g) * M_STATE:(M_GROUPS + g + 1) * M_STATE]
        cbm = _mm_nt(cg, bg)
        for hh in range(M_HEADS // M_GROUPS):
            h = g * (M_HEADS // M_GROUPS) + hh
            hs = slice(h * M_HEAD_DIM, (h + 1) * M_HEAD_DIM)
            diff = cs[:, h:h + 1] - cst[h:h + 1, :]
            lm = jnp.exp(jnp.where(lower, diff, -jnp.inf))
            yd = _mm(cbm * lm, xd[:, hs])
            st = st_ref[h]
            yo = _mm_nt(cg, st)
            yd_ref[:, hs] = yd + yo * ecs_e[:, hs]
            upd = _mm_tn(xdw[:, hs], bg)
            dec = jnp.exp(cst[h:h + 1, T - 1:T])
            st_ref[h] = st * dec + upd

    y = yd_ref[...] + xs * dexp_ref[...]
    y = y * _silu(z_ref[...])
    gw = M_INNER // M_GROUPS
    for g in range(M_GROUPS):
        yg = y[:, g * gw:(g + 1) * gw]
        ms = jnp.mean(yg * yg, axis=-1, keepdims=True)
        y_ref[:, g * gw:(g + 1) * gw] = (yg * lax.rsqrt(ms + NORM_EPS)
                                         * norm_ref[:, g * gw:(g + 1) * gw]).astype(y_ref.dtype)

    @pl.when(c == nc - 1)
    def _():
        stout_ref[0] = st_ref[...]


def _ssd(p, row0, n_seq, n_chunks, T, n_valid, layer, conv_state, ssm0, cw, cb, dtb, alog, dexp, norm,
         e16):
    rb0 = row0 // T

    def rmap(col):
        return lambda s, c: (rb0 + s * n_chunks + c, col)

    kern = functools.partial(_ssd_kernel, T=T, n_valid=n_valid)
    return pl.pallas_call(
        kern,
        grid=(n_seq, n_chunks),
        in_specs=[pl.BlockSpec((T, 1024), rmap(C_Z // 1024)),
                  pl.BlockSpec((T, 1024), rmap(C_XS // 1024)),
                  pl.BlockSpec((T, 1024), rmap(C_BC // 1024)),
                  pl.BlockSpec((T, LANE), rmap(C_MISC // LANE)),
                  pl.BlockSpec((1, 1, CONV_W - 1, CONV_DIM), lambda s, c: (layer, s, 0, 0)),
                  pl.BlockSpec((1, 1, M_HEADS, M_HEAD_DIM, M_STATE), lambda s, c: (layer, s, 0, 0, 0)),
                  _full((CONV_W, CONV_DIM)), _full((1, CONV_DIM)), _full((1, M_HEADS)),
                  _full((1, M_HEADS)), _full((1, M_INNER)), _full((1, M_INNER)),
                  _full((M_HEADS, M_INNER))],
        out_specs=[pl.BlockSpec((T, M_INNER), lambda s, c: (s * n_chunks + c, 0)),
                   pl.BlockSpec((1, M_HEADS, M_HEAD_DIM, M_STATE), lambda s, c: (s, 0, 0, 0))],
        out_shape=[jax.ShapeDtypeStruct((n_seq * n_chunks * T, M_INNER), BF16),
                   jax.ShapeDtypeStruct((n_seq, M_HEADS, M_HEAD_DIM, M_STATE), F32)],
        scratch_shapes=[pltpu.VMEM((CONV_W - 1, CONV_DIM), F32),
                        pltpu.VMEM((M_HEADS, M_HEAD_DIM, M_STATE), F32),
                        pltpu.VMEM((T, M_INNER), F32)],
        compiler_params=_cparams(("parallel", "arbitrary")),
        name="ssd",
    )(p, p, p, p, conv_state, ssm0, cw, cb, dtb, alog, dexp, norm, e16)


def _rwkv_kernel(r_ref, k_ref, v_ref, lo_ref, sh0_ref, s0_ref, mu_ref, w0_ref, w2_ref, a0_ref,
                 a2_ref, g2_ref, kkw_ref, kaw_ref, rk_ref, gng_ref, gnb_ref, y_ref, sout_ref,
                 last_ref, s_ref, yb_ref, *, T, n_valid, HG):
    c = pl.program_id(1)
    nc = pl.num_programs(1)

    @pl.when(c == 0)
    def _():
        last_ref[...] = sh0_ref[0, 0]
        s_ref[...] = s0_ref[0, 0]

    rows = lax.broadcasted_iota(I32, (T, 1), 0)

    def mix(ref, lo, hi):
        x = ref[...]
        prev = jnp.where(rows == 0, last_ref[:, lo:hi], pltpu.roll(x, 1, 0))
        last_ref[:, lo:hi] = x[T - 1:T, :]
        return x + (prev - x) * mu_ref[:, lo:hi]

    r = mix(r_ref, 0, 1024)
    k = mix(k_ref, 1024, 2048)
    v = mix(v_ref, 2048, 3072)
    lo = mix(lo_ref, 3072, 3072 + LORA_W)
    xw = lo[:, 0:R_DECAY_LORA]
    xa = lo[:, R_DECAY_LORA:R_DECAY_LORA + R_A_LORA]
    xg = lo[:, R_DECAY_LORA + R_A_LORA:LORA_W]
    wl = -jax.nn.softplus(-(w0_ref[...] + _mm_hi(jnp.tanh(xw), w2_ref[...]))) - 0.5
    ld = -jnp.exp(wl)
    a = _sigmoid(a0_ref[...] + _mm_hi(xa, a2_ref[...]))
    gate = _mm_hi(_sigmoid(xg), g2_ref[...])
    kk = k * kkw_ref[...]
    km = k * (1.0 + (a - 1.0) * kaw_ref[...])
    if n_valid < T:
        ok = rows < n_valid
        ld = jnp.where(ok, ld, 0.0)
        kk = jnp.where(ok, kk, 0.0)
        km = jnp.where(ok, km, 0.0)
        v = jnp.where(ok, v, 0.0)

    ri = lax.broadcasted_iota(I32, (T, T), 0)
    ci = lax.broadcasted_iota(I32, (T, T), 1)
    cum = _mm_hi((ri >= ci).astype(F32), ld)
    eg = jnp.exp(cum)
    egm = jnp.exp(cum - ld)
    ei = jnp.exp(-cum)
    el = jnp.exp(cum[T - 1:T, :] - cum)
    r_eg = r * eg
    k_ei = km * ei
    k_el = km * el
    rk_sum = r * km * rk_ref[...]
    gng = gng_ref[...]
    gnb = gnb_ref[...]

    R = HG * T
    GW = HG * R_HEAD
    rr = lax.broadcasted_iota(I32, (R, R), 0)
    cc = lax.broadcasted_iota(I32, (R, R), 1)
    same = (rr // T) == (cc // T)
    strict = jnp.logical_and(same, rr > cc)
    incl = jnp.logical_and(same, rr >= cc)
    head_of = (lax.broadcasted_iota(I32, (R, GW), 0) // T) == (lax.broadcasted_iota(I32, (R, GW), 1) // R_HEAD)

    for gi in range(R_HEADS // HG):
        hsl = [slice((gi * HG + hh) * R_HEAD, (gi * HG + hh + 1) * R_HEAD) for hh in range(HG)]

        def stack(x):
            return jnp.concatenate([x[:, s_] for s_ in hsl], axis=0)

        kks = stack(kk)
        nrm = jnp.sqrt(jnp.sum(kks * kks, axis=-1, keepdims=True))
        kks = kks / jnp.maximum(nrm, 1e-12)
        bs = kks * stack(a)
        vs = stack(v)
        q2 = jnp.concatenate([kks * stack(egm), stack(r_eg)], axis=0)
        k2 = jnp.concatenate([stack(k_ei), bs * stack(ei)], axis=0)
        gm = _mm_nt(q2, k2)
        scat = s_ref[gi * HG:(gi + 1) * HG].reshape(GW, R_HEAD)
        hmw = _mm_nt(q2, scat)
        hk = jnp.concatenate([hmw[hh * T:(hh + 1) * T, hh * R_HEAD:(hh + 1) * R_HEAD]
                              for hh in range(HG)], axis=0)
        hr = jnp.concatenate([hmw[R + hh * T:R + (hh + 1) * T, hh * R_HEAD:(hh + 1) * R_HEAD]
                              for hh in range(HG)], axis=0)
        lk = jnp.where(strict, gm[0:R, 0:R], 0.0)
        lb = jnp.where(strict, gm[0:R, R:2 * R], 0.0)
        ak = jnp.where(incl, gm[R:2 * R, 0:R], 0.0)
        ab = jnp.where(incl, gm[R:2 * R, R:2 * R], 0.0)
        x = hk + _mm(lk, vs)
        zz = x - _mm(lb, x)
        lp = lb
        span = 2
        while span < T:
            lp = _mm(lp, lp)
            zz = zz + _mm(lp, zz)
            span *= 2
        u = -zz
        y = hr + _mm(jnp.concatenate([ak, ab], axis=1), jnp.concatenate([vs, u], axis=0))
        vw = jnp.where(head_of, jnp.concatenate([v[:, gi * GW:(gi + 1) * GW]] * HG, axis=0), 0.0)
        uw = jnp.where(head_of, jnp.concatenate([u] * HG, axis=1), 0.0)
        upd = _mm_tn(jnp.concatenate([vw, uw], axis=0),
                     jnp.concatenate([stack(k_el), bs * stack(el)], axis=0))
        g_last = jnp.concatenate([jnp.broadcast_to(eg[T - 1:T, s_], (R_HEAD, R_HEAD)) for s_ in hsl],
                                 axis=0)
        s_ref[gi * HG:(gi + 1) * HG] = (scat * g_last + upd).reshape(HG, R_HEAD, R_HEAD)
        mu = jnp.mean(y, axis=-1, keepdims=True)
        var = jnp.mean(jnp.square(y - mu), axis=-1, keepdims=True)
        gs = jnp.concatenate([jnp.broadcast_to(gng[:, s_], (T, R_HEAD)) for s_ in hsl], axis=0)
        gb = jnp.concatenate([jnp.broadcast_to(gnb[:, s_], (T, R_HEAD)) for s_ in hsl], axis=0)
        yn = (y - mu) * lax.rsqrt(var + GN_EPS) * gs + gb
        yn = yn + jnp.sum(stack(rk_sum), axis=-1, keepdims=True) * vs
        for hh in range(HG):
            yb_ref[:, hsl[hh]] = yn[hh * T:(hh + 1) * T]

    y_ref[...] = (yb_ref[...] * gate).astype(y_ref.dtype)

    @pl.when(c == nc - 1)
    def _():
        sout_ref[0] = s_ref[...]


def _rwkv(p, row0, n_seq, n_chunks, T, n_valid, layer, shift0, s0, mu, w0, w2, a0, a2, g2, kkw, kaw,
          rk, gng, gnb):
    rb0 = row0 // T

    def rmap(col):
        return lambda s, c: (rb0 + s * n_chunks + c, col)

    sw = 3072 + LORA_W
    hg = min(R_HEADS, max(1, MXU_DIM // T))
    kern = functools.partial(_rwkv_kernel, T=T, n_valid=n_valid, HG=hg)
    return pl.pallas_call(
        kern,
        grid=(n_seq, n_chunks),
        in_specs=[pl.BlockSpec((T, 1024), rmap(C_RR // 1024)),
                  pl.BlockSpec((T, 1024), rmap(C_RK // 1024)),
                  pl.BlockSpec((T, 1024), rmap(C_RV // 1024)),
                  pl.BlockSpec((T, LORA_W), rmap(C_LORA // LORA_W)),
                  pl.BlockSpec((1, 1, 1, sw), lambda s, c: (layer, s, 0, 0)),
                  pl.BlockSpec((1, 1, R_HEADS, R_HEAD, R_HEAD), lambda s, c: (layer, s, 0, 0, 0)),
                  _full((1, sw)), _full((1, R_WIDTH)), _full((R_DECAY_LORA, R_WIDTH)),
                  _full((1, R_WIDTH)), _full((R_A_LORA, R_WIDTH)),
                  _full((LORA_W - R_DECAY_LORA - R_A_LORA, R_WIDTH)),
                  _full((1, R_WIDTH)), _full((1, R_WIDTH)), _full((1, R_WIDTH)),
                  _full((1, R_WIDTH)), _full((1, R_WIDTH))],
        out_specs=[pl.BlockSpec((T, R_WIDTH), lambda s, c: (s * n_chunks + c, 0)),
                   pl.BlockSpec((1, R_HEADS, R_HEAD, R_HEAD), lambda s, c: (s, 0, 0, 0))],
        out_shape=[jax.ShapeDtypeStruct((n_seq * n_chunks * T, R_WIDTH), BF16),
                   jax.ShapeDtypeStruct((n_seq, R_HEADS, R_HEAD, R_HEAD), F32)],
        scratch_shapes=[pltpu.VMEM((1, sw), F32),
                        pltpu.VMEM((R_HEADS, R_HEAD, R_HEAD), F32),
                        pltpu.VMEM((T, R_WIDTH), F32)],
        compiler_params=_cparams(("parallel", "arbitrary")),
        name="rwkv",
    )(p, p, p, p, shift0, s0, mu, w0, w2, a0, a2, g2, kkw, kaw, rk, gng, gnb)


def _score_keys(acc, causal):
    bits = pltpu.bitcast(acc, I32)
    key = bits ^ ((bits >> 31) & 0x7FFFFFFF)
    return jnp.where(causal, key, INT_MIN)


def _kth_threshold(count_ge, shape, top):
    tb = jnp.zeros(shape, I32)
    for bit in range(31, -1, -1):
        inc = INT_MIN if bit == 31 else (1 << bit)
        cand_b = tb | jnp.int32(inc)
        cnt = count_ge(cand_b ^ jnp.int32(INT_MIN))
        tb = jnp.where(cnt >= top, cand_b, tb)
    return jnp.maximum(tb ^ jnp.int32(INT_MIN), INT_MIN + 1)


def _toeplitz_bias(base_row, nrows):
    return pltpu.roll(jnp.broadcast_to(base_row, (nrows, 2 * LANE)), 0, 1, stride=1, stride_axis=0)


def _dsa_prompt_kernel(q_ref, qi_ref, misc_ref, kit_ref, kt_ref, va_ref, base_ref, o_ref,
                       key_ref, mx_ref, acc_ref, *, top, TA):
    i = pl.program_id(0)
    QB = LANE
    q0 = i * QB
    n_a = (q0 + QB + TA - 1) // TA
    wi = misc_ref[:, MISC_WI:MISC_WI + IDX_HEADS] * (IDX_HEADS ** -0.5 * IDX_DIM ** -0.5)
    qi = qi_ref[...]
    qis = jnp.concatenate([qi[:, h * IDX_DIM:(h + 1) * IDX_DIM] for h in range(IDX_HEADS)],
                          axis=0).astype(BF16)
    rowq = q0 + lax.broadcasted_iota(I32, (QB, 1), 0)

    def score_tile(kt, carry):
        off = pl.multiple_of(kt * TA, TA)
        s = jnp.maximum(jnp.dot(qis, kit_ref[:, pl.ds(off, TA)], preferred_element_type=F32), 0.0)
        acc = s[0:QB] * wi[:, 0:1]
        for h in range(1, IDX_HEADS):
            acc = acc + s[h * QB:(h + 1) * QB] * wi[:, h:h + 1]
        col = off + lax.broadcasted_iota(I32, (1, TA), 1)
        key_ref[:, pl.ds(off, TA)] = _score_keys(acc, col <= rowq)
        return carry

    lax.fori_loop(0, n_a, score_tile, 0)

    def count_ge(cand):
        def body(kt, cnt):
            off = pl.multiple_of(kt * TA, TA)
            ge = jnp.where(key_ref[:, pl.ds(off, TA)] >= cand, 1.0, 0.0)
            for j in range(TA // LANE):
                cnt = cnt + ge[:, j * LANE:(j + 1) * LANE]
            return cnt
        cnt = lax.fori_loop(0, n_a, body, jnp.zeros((QB, LANE), F32))
        return jnp.sum(cnt, axis=-1, keepdims=True)

    thr = _kth_threshold(count_ge, (QB, 1), float(top))

    q = q_ref[...]
    lim = (i - 1) * QB
    n_far = (jnp.maximum(lim, 0) + TA - 1) // TA
    qgs, nears = [], []
    for g in range(KV_HEADS):
        heads = range(g * GQA, (g + 1) * GQA)
        qgs.append((jnp.concatenate([q[:, h * HEAD_DIM:(h + 1) * HEAD_DIM] for h in heads], axis=0)
                    * HEAD_DIM ** -0.5).astype(BF16))
        nears.append(jnp.concatenate(
            [_toeplitz_bias(base_ref[h:h + 1, :], QB) - base_ref[h:h + 1, 2 * LANE - 1:2 * LANE]
             for h in heads], axis=0))

    def far_logits(g, off, msk):
        s = jnp.dot(qgs[g], kt_ref[g, :, pl.ds(off, TA)], preferred_element_type=F32)
        return (s.reshape(GQA, QB, TA) + msk[None]).reshape(GQA * QB, TA)

    def far_mask(off):
        col = off + lax.broadcasted_iota(I32, (1, TA), 1)
        return jnp.where(key_ref[:, pl.ds(off, TA)] >= thr,
                         jnp.where(col < lim, 0.0, NEG_BIG), NEG_BIG)

    def near_logits(g, off, lo, msk):
        s = (jnp.dot(qgs[g], kt_ref[g, :, pl.ds(off, LANE)], preferred_element_type=F32)
             + nears[g][:, lo:lo + LANE])
        return (s.reshape(GQA, QB, LANE) + msk[None]).reshape(GQA * QB, LANE)

    def near_mask(off):
        return jnp.where(key_ref[:, pl.ds(off, LANE)] >= thr, 0.0, NEG_BIG)

    mx_ref[...] = jnp.full(mx_ref.shape, NEG_BIG, F32)

    def far_max(kt, carry):
        off = pl.multiple_of(kt * TA, TA)
        msk = far_mask(off)
        for g in range(KV_HEADS):
            s = far_logits(g, off, msk)
            m = s[:, 0:LANE]
            for j in range(1, TA // LANE):
                m = jnp.maximum(m, s[:, j * LANE:(j + 1) * LANE])
            mx_ref[g] = jnp.maximum(mx_ref[g], m)
        return carry

    def near_max(kt, lo):
        off = pl.multiple_of(kt * LANE, LANE)
        msk = near_mask(off)
        for g in range(KV_HEADS):
            mx_ref[g] = jnp.maximum(mx_ref[g], near_logits(g, off, lo, msk))

    lax.fori_loop(0, n_far, far_max, 0)

    @pl.when(i >= 1)
    def _():
        near_max(i - 1, 0)

    near_max(i, LANE)
    for g in range(KV_HEADS):
        mx_ref[g] = jnp.broadcast_to(jnp.max(mx_ref[g], axis=-1, keepdims=True), (GQA * QB, LANE))
    acc_ref[...] = jnp.zeros(acc_ref.shape, F32)

    def far_acc(kt, carry):
        off = pl.multiple_of(kt * TA, TA)
        msk = far_mask(off)
        for g in range(KV_HEADS):
            s = far_logits(g, off, msk)
            pr = jnp.exp(s - jnp.concatenate([mx_ref[g]] * (TA // LANE), axis=1))
            acc_ref[g] += jnp.dot(pr.astype(BF16), va_ref[g, pl.ds(off, TA), :],
                                  preferred_element_type=F32)
        return carry

    def near_acc(kt, lo):
        off = pl.multiple_of(kt * LANE, LANE)
        msk = near_mask(off)
        for g in range(KV_HEADS):
            pr = jnp.exp(near_logits(g, off, lo, msk) - mx_ref[g])
            acc_ref[g] += jnp.dot(pr.astype(BF16), va_ref[g, pl.ds(off, LANE), :],
                                  preferred_element_type=F32)

    lax.fori_loop(0, n_far, far_acc, 0)

    @pl.when(i >= 1)
    def _():
        near_acc(i - 1, 0)

    near_acc(i, LANE)
    for g in range(KV_HEADS):
        acc = acc_ref[g]
        out = acc[:, 0:HEAD_DIM] / acc[:, HEAD_DIM:HEAD_DIM + 1]
        for hh in range(GQA):
            h = g * GQA + hh
            o_ref[:, h * HEAD_DIM:(h + 1) * HEAD_DIM] = out[hh * QB:(hh + 1) * QB].astype(o_ref.dtype)


def _dsa_prompt(p, kit, kt, va, base, S):
    top = min(TOPK_MAX, S // 4)
    TA = min(512, S)
    kern = functools.partial(_dsa_prompt_kernel, top=top, TA=TA)
    return pl.pallas_call(
        kern,
        grid=(S // LANE,),
        in_specs=[pl.BlockSpec((LANE, 1024), lambda i: (i, C_Q // 1024)),
                  pl.BlockSpec((LANE, 1024), lambda i: (i, C_QI // 1024)),
                  pl.BlockSpec((LANE, LANE), lambda i: (i, C_MISC // LANE)),
                  _full((IDX_DIM, S)), _full((KV_HEADS, HEAD_DIM, S)), _full((KV_HEADS, S, LANE)),
                  _full((ATT_HEADS, 2 * LANE))],
        out_specs=pl.BlockSpec((LANE, MIX_W), lambda i: (i, 0)),
        out_shape=jax.ShapeDtypeStruct((S, MIX_W), BF16),
        scratch_shapes=[pltpu.VMEM((LANE, S), I32),
                        pltpu.VMEM((KV_HEADS, GQA * LANE, LANE), F32),
                        pltpu.VMEM((KV_HEADS, GQA * LANE, LANE), F32)],
        compiler_params=_cparams(("arbitrary",)),
        name="dsa_prompt",
    )(p, p, p, kit, kt, va, base)


def _dsa_sample_kernel(pt_ref, q_ref, qi_ref, misc_ref, kn_ref, vn_ref, kin_ref, base_ref, *rest,
                       n_pages, n_valid, top):
    ki_pages = rest[0:n_pages]
    k_pages = rest[n_pages:2 * n_pages]
    v_pages = rest[2 * n_pages:3 * n_pages]
    o_ref = rest[3 * n_pages]
    kib_ref, kb_ref, vb_ref = rest[3 * n_pages + 1:]
    TQ = SEQ_PAD
    lo = n_pages * LANE
    NK = lo + LANE
    wi = misc_ref[:, MISC_WI:MISC_WI + IDX_HEADS] * (IDX_HEADS ** -0.5 * IDX_DIM ** -0.5)
    qi = qi_ref[...]
    qis = jnp.concatenate([qi[:, h * IDX_DIM:(h + 1) * IDX_DIM] for h in range(IDX_HEADS)], axis=0)

    def pad_rows(x):
        return jnp.concatenate([x, jnp.zeros((LANE - TQ, x.shape[1]), x.dtype)], axis=0)

    for t in range(n_pages):
        kib_ref[t * LANE:(t + 1) * LANE, :] = ki_pages[t][0, 0].astype(BF16)
        kb_ref[t * LANE:(t + 1) * LANE, :] = k_pages[t][0, 0].astype(BF16)
        vb_ref[t * LANE:(t + 1) * LANE, :] = v_pages[t][0, 0].astype(BF16)
    kib_ref[lo:NK, :] = pad_rows(kin_ref[...]).astype(BF16)
    kb_ref[lo:NK, :] = pad_rows(kn_ref[...]).astype(BF16)
    vb_ref[lo:NK, :] = pad_rows(vn_ref[...]).astype(BF16)

    s = jnp.maximum(_mm_nt(qis, kib_ref[...]), 0.0)
    acc = s[0:TQ] * wi[:, 0:1]
    for h in range(1, IDX_HEADS):
        acc = acc + s[h * TQ:(h + 1) * TQ] * wi[:, h:h + 1]
    trow = lax.broadcasted_iota(I32, (TQ, 1), 0)
    new = lax.broadcasted_iota(I32, (1, NK), 1) - lo
    keys = _score_keys(acc, new <= jnp.minimum(trow, n_valid - 1))

    def count_ge(cand):
        return jnp.sum(jnp.where(keys >= cand, 1.0, 0.0), axis=-1, keepdims=True)

    thr = _kth_threshold(count_ge, (TQ, 1), float(top))
    msk = jnp.where(keys >= thr, 0.0, NEG_BIG)
    msk = jnp.concatenate([msk] * GQA, axis=0)

    q = q_ref[...]
    for g in range(KV_HEADS):
        heads = range(g * GQA, (g + 1) * GQA)
        gs = slice(g * HEAD_DIM, (g + 1) * HEAD_DIM)
        qg = (jnp.concatenate([q[:, h * HEAD_DIM:(h + 1) * HEAD_DIM] for h in heads], axis=0)
              * HEAD_DIM ** -0.5)
        near = jnp.concatenate(
            [_toeplitz_bias(base_ref[h:h + 1, :], TQ) - base_ref[h:h + 1, 2 * LANE - 1:2 * LANE]
             for h in heads], axis=0)
        bias = jnp.concatenate([jnp.zeros((GQA * TQ, lo - LANE), F32), near], axis=1)
        s = _mm_nt(qg, kb_ref[:, gs]) + bias + msk
        mx = jnp.max(s, axis=-1, keepdims=True)
        pr = jnp.exp(s - mx)
        pr = pr / jnp.sum(pr, axis=-1, keepdims=True)
        out = _mm(pr, vb_ref[:, gs])
        for hh, h in enumerate(heads):
            o_ref[:, h * HEAD_DIM:(h + 1) * HEAD_DIM] = out[hh * TQ:(hh + 1) * TQ].astype(o_ref.dtype)


def _dsa_sample(p, row0, n_seq, kidx_ln, base, layer, cache_k, cache_v, cache_kidx, page_table, n_valid):
    n_pages = page_table.shape[1]
    past = n_pages * PAGE_SIZE
    top = min(TOPK_MAX, (past + n_valid) // 4)
    rb0 = row0 // SEQ_PAD
    n_layers, n_pool = cache_k.shape[:2]
    ck = cache_k.reshape(n_layers, n_pool, PAGE_SIZE, KV_HEADS * HEAD_DIM)
    cv = cache_v.reshape(n_layers, n_pool, PAGE_SIZE, KV_HEADS * HEAD_DIM)

    def rmap(col):
        return lambda b, pt: (rb0 + b, col)

    def pmap(pg):
        return lambda b, pt: (layer, pt[b, pg], 0, 0)

    in_specs = [pl.BlockSpec((SEQ_PAD, 1024), rmap(C_Q // 1024)),
                pl.BlockSpec((SEQ_PAD, 1024), rmap(C_QI // 1024)),
                pl.BlockSpec((SEQ_PAD, LANE), rmap(C_MISC // LANE)),
                pl.BlockSpec((SEQ_PAD, 256), rmap(C_K // 256)),
                pl.BlockSpec((SEQ_PAD, 256), rmap(C_V // 256)),
                pl.BlockSpec((SEQ_PAD, IDX_DIM), lambda b, pt: (rb0 + b, 0)),
                pl.BlockSpec((ATT_HEADS, 2 * LANE), lambda b, pt: (0, 0))]
    in_specs += [pl.BlockSpec((1, 1, PAGE_SIZE, IDX_DIM), pmap(pg)) for pg in range(n_pages)]
    in_specs += [pl.BlockSpec((1, 1, PAGE_SIZE, KV_HEADS * HEAD_DIM), pmap(pg)) for pg in range(n_pages)]
    in_specs += [pl.BlockSpec((1, 1, PAGE_SIZE, KV_HEADS * HEAD_DIM), pmap(pg)) for pg in range(n_pages)]
    kern = functools.partial(_dsa_sample_kernel, n_pages=n_pages, n_valid=n_valid, top=top)
    nk = (n_pages + 1) * LANE
    return pl.pallas_call(
        kern,
        grid_spec=pltpu.PrefetchScalarGridSpec(
            num_scalar_prefetch=1,
            grid=(n_seq,),
            in_specs=in_specs,
            out_specs=pl.BlockSpec((SEQ_PAD, MIX_W), lambda b, pt: (b, 0)),
            scratch_shapes=[pltpu.VMEM((nk, IDX_DIM), BF16),
                            pltpu.VMEM((nk, KV_HEADS * HEAD_DIM), BF16),
                            pltpu.VMEM((nk, KV_HEADS * HEAD_DIM), BF16)]),
        out_shape=jax.ShapeDtypeStruct((n_seq * SEQ_PAD, MIX_W), BF16),
        compiler_params=_cparams(("arbitrary",)),
        name="dsa_sample",
    )(page_table, p, p, p, p, p, kidx_ln, base,
      *([cache_kidx] * n_pages), *([ck] * n_pages), *([cv] * n_pages))


def _t5_base(t5_table):
    x = jnp.arange(2 * LANE, dtype=jnp.int32)
    rel = jnp.where(x <= LANE, LANE - x, T5_MAX_DIST)
    n = jnp.maximum(rel, 0)
    exact = T5_BUCKETS // 2
    nf = jnp.maximum(n, exact).astype(F32)
    large = exact + (jnp.log(nf / exact) / math.log(T5_MAX_DIST / exact)
                     * (T5_BUCKETS - exact)).astype(jnp.int32)
    bucket = jnp.where(n < exact, n, jnp.minimum(large, T5_BUCKETS - 1))
    return t5_table[bucket].astype(F32).T


def _relayout_w_in(w):
    widths = (M_INNER, CONV_DIM, M_HEADS, ATT_HEADS * HEAD_DIM, KV_HEADS * HEAD_DIM,
              KV_HEADS * HEAD_DIM, IDX_HEADS * IDX_DIM, IDX_DIM, IDX_HEADS, R_PROJ, 3 * D_MODEL)
    offs = [0]
    for wd in widths:
        offs.append(offs[-1] + wd)
    z, xbc, dt, q, k, v, qi, ki, wi, pr, gates = (w[:, offs[i]:offs[i + 1]] for i in range(len(widths)))
    zeros = lambda n: jnp.zeros((w.shape[0], n), w.dtype)
    lora = pr[:, 3 * R_WIDTH:]
    cols = [z, xbc, q, qi, pr[:, :3 * R_WIDTH], gates,
            lora, zeros(LORA_W - lora.shape[1]), k, v,
            ki, dt, wi, zeros(LANE - IDX_DIM - M_HEADS - IDX_HEADS)]
    out = jnp.concatenate(cols, axis=1).astype(BF16)
    assert out.shape[1] == P_W
    return out


def _pad_cols(x, n):
    return jnp.concatenate([x, jnp.zeros(x.shape[:-1] + (n - x.shape[-1],), x.dtype)], axis=-1)


def _shift_layout(x):
    return _pad_cols(x, 3 * R_WIDTH + LORA_W)


def kernel(x_prompt, x_sample, cache_k, cache_v, cache_kidx, page_table, state_ssm, state_conv, state_wkv, state_shift, ffn1_norm, ffn1_gate, ffn1_up, ffn1_down, mix_norm, w_in, conv_w, conv_b, dt_bias, a_log, d_skip, ssm_norm, kidx_ln_g, kidx_ln_b, t5_table, rwkv_mu, rwkv_w0, rwkv_w2, rwkv_a0, rwkv_a2, rwkv_g2, rwkv_kk, rwkv_ka, rwkv_rk, rwkv_gn_g, rwkv_gn_b, w_br_a, w_br_b, w_br_c, w_out, ffn2_norm, ffn2_gate, ffn2_up, ffn2_down, final_norm):
    bp, S, d = x_prompt.shape
    nb, T_dec, _ = x_sample.shape
    depth = w_in.shape[0]
    assert bp == 1 and T_dec <= SEQ_PAD and S % SSD_CHUNK == 0
    n_p = bp * S
    n_rows = n_p + nb * SEQ_PAD
    xs_pad = jnp.concatenate([x_sample, jnp.zeros((nb, SEQ_PAD - T_dec, d), x_sample.dtype)], axis=1)
    h = jnp.concatenate([x_prompt.reshape(n_p, d), xs_pad.reshape(nb * SEQ_PAD, d)], axis=0)

    base = _t5_base(t5_table)
    e16 = jnp.repeat(jnp.eye(M_HEADS, dtype=F32), M_HEAD_DIM, axis=1)
    row = lambda x: x.reshape(1, -1).astype(F32)
    rc = min(RWKV_RC_CHUNK, S)
    shift_all = _shift_layout(state_shift).reshape(depth, nb, 1, 3 * R_WIDTH + LORA_W)
    outs = {k_: [] for k_ in ("k_p", "v_p", "ki_p", "ssm_p", "conv_p", "wkv_p", "sh_p",
                              "k_s", "v_s", "ki_s", "ssm_s", "conv_s", "wkv_s", "sh_s")}
    for l in range(depth):
        bf = lambda x: x.astype(BF16)
        h = _ffn(h, row(ffn1_norm[l]), bf(ffn1_gate[l]), bf(ffn1_up[l]), bf(ffn1_down[l]))
        p = _inproj(h, row(mix_norm[l]), _relayout_w_in(w_in[l]))
        kidx = _kidx_ln(p, row(kidx_ln_g[l]), row(kidx_ln_b[l]))

        ssd_par = (conv_w[l], row(conv_b[l]), row(dt_bias[l]), row(a_log[l]),
                   row(jnp.repeat(d_skip[l], M_HEAD_DIM)), row(ssm_norm[l]), e16)
        ya_p, ssm_p = _ssd(p, 0, bp, S // SSD_CHUNK, SSD_CHUNK, SSD_CHUNK, 0,
                           jnp.zeros((1, bp, CONV_W - 1, CONV_DIM), F32),
                           jnp.zeros((1, bp, M_HEADS, M_HEAD_DIM, M_STATE), F32), *ssd_par)
        ya_s, ssm_s = _ssd(p, n_p, nb, 1, SEQ_PAD, T_dec, l, state_conv, state_ssm, *ssd_par)

        g2p = jnp.concatenate([rwkv_g2[l], jnp.zeros((LORA_W - R_DECAY_LORA - R_A_LORA - R_G_LORA,
                                                      R_WIDTH), F32)], axis=0)
        rw_par = (row(_shift_layout(rwkv_mu[l])), row(rwkv_w0[l]), rwkv_w2[l], row(rwkv_a0[l]),
                  rwkv_a2[l], g2p, row(rwkv_kk[l]), row(rwkv_ka[l]), row(rwkv_rk[l]),
                  row(rwkv_gn_g[l]), row(rwkv_gn_b[l]))
        sw = 3 * R_WIDTH + LORA_W
        yc_p, wkv_p = _rwkv(p, 0, bp, S // rc, rc, rc, 0, jnp.zeros((1, bp, 1, sw), F32),
                            jnp.zeros((1, bp, R_HEADS, R_HEAD, R_HEAD), F32), *rw_par)
        yc_s, wkv_s = _rwkv(p, n_p, nb, 1, SEQ_PAD, T_dec, l, shift_all, state_wkv, *rw_par)

        kp = p[:n_p, C_K:C_K + 256]
        vp = p[:n_p, C_V:C_V + 256]
        kit = bf(kidx[:n_p].T)
        kt = bf(kp.reshape(n_p, KV_HEADS, HEAD_DIM).transpose(1, 2, 0))
        vg = vp.reshape(n_p, KV_HEADS, HEAD_DIM).transpose(1, 0, 2)
        va = bf(jnp.concatenate([vg, jnp.ones((KV_HEADS, n_p, 1), F32),
                                 jnp.zeros((KV_HEADS, n_p, LANE - HEAD_DIM - 1), F32)], axis=-1))
        yb_p = _dsa_prompt(p, kit, kt, va, base, S)
        yb_s = _dsa_sample(p, n_p, nb, kidx, base, l, cache_k, cache_v, cache_kidx, page_table, T_dec)

        ya = jnp.concatenate([ya_p, ya_s], axis=0)
        yb = jnp.concatenate([yb_p, yb_s], axis=0)
        yc = jnp.concatenate([yc_p, yc_s], axis=0)
        h = _merge(h, p, ya, yb, yc, bf(w_br_a[l]), bf(w_br_b[l]), bf(w_br_c[l]), bf(w_out[l]))
        h = _ffn(h, row(ffn2_norm[l]), bf(ffn2_gate[l]), bf(ffn2_up[l]), bf(ffn2_down[l]))

        ps = p[n_p:].reshape(nb, SEQ_PAD, P_W)[:, :T_dec]
        xbc = lambda t: jnp.concatenate([t[..., C_XS:C_XS + 1024], t[..., C_BC:C_BC + 1024]], axis=-1)
        prj = lambda t: jnp.concatenate([t[..., C_RR:C_RR + 3 * R_WIDTH],
                                         t[..., C_LORA:C_LORA + R_PROJ - 3 * R_WIDTH]], axis=-1)
        outs["k_p"].append(kp.reshape(bp, S, KV_HEADS, HEAD_DIM))
        outs["v_p"].append(vp.reshape(bp, S, KV_HEADS, HEAD_DIM))
        outs["ki_p"].append(kidx[:n_p].reshape(bp, S, IDX_DIM))
        outs["ssm_p"].append(ssm_p)
        outs["conv_p"].append(xbc(p[n_p - (CONV_W - 1):n_p]).reshape(bp, CONV_W - 1, CONV_DIM))
        outs["wkv_p"].append(wkv_p)
        outs["sh_p"].append(prj(p[n_p - 1:n_p]).reshape(bp, R_PROJ))
        outs["k_s"].append(ps[..., C_K:C_K + 256].reshape(nb, T_dec, KV_HEADS, HEAD_DIM))
        outs["v_s"].append(ps[..., C_V:C_V + 256].reshape(nb, T_dec, KV_HEADS, HEAD_DIM))
        outs["ki_s"].append(kidx[n_p:].reshape(nb, SEQ_PAD, IDX_DIM)[:, :T_dec])
        outs["ssm_s"].append(ssm_s)
        conv_full = jnp.concatenate([state_conv[l], xbc(ps)], axis=1)
        outs["conv_s"].append(conv_full[:, T_dec:])
        outs["wkv_s"].append(wkv_s)
        outs["sh_s"].append(prj(ps[:, T_dec - 1]))

    y = _final_norm(h, row(final_norm))
    y_prompt = y[:n_p].reshape(bp, S, d)
    y_sample = y[n_p:].reshape(nb, SEQ_PAD, d)[:, :T_dec]
    stk = lambda name: jnp.stack(outs[name], axis=0)
    return (y_prompt, y_sample, stk("k_p"), stk("v_p"), stk("ki_p"), stk("ssm_p"), stk("conv_p"),
            stk("wkv_p"), stk("sh_p"), stk("k_s"), stk("v_s"), stk("ki_s"), stk("ssm_s"),
            stk("conv_s"), stk("wkv_s"), stk("sh_s"))
```

```python
import functools
import math

import jax
import jax.numpy as jnp
from jax import lax
from jax.experimental import pallas as pl
from jax.experimental.pallas import tpu as pltpu

F32 = jnp.float32
BF16 = jnp.bfloat16
I32 = jnp.int32

D_MODEL = 2048
MIX_W = D_MODEL // 2
M_HEAD_DIM = 64
M_INNER = MIX_W
M_HEADS = M_INNER // M_HEAD_DIM
M_GROUPS = 4
M_STATE = 128
CONV_W = 4
CONV_DIM = M_INNER + 2 * M_GROUPS * M_STATE
SSD_CHUNK = 128
HEAD_DIM = 64
ATT_HEADS = MIX_W // HEAD_DIM
KV_HEADS = 4
GQA = ATT_HEADS // KV_HEADS
IDX_HEADS = 16
IDX_DIM = 64
TOPK_MAX = 256
T5_BUCKETS = 32
T5_MAX_DIST = 128
R_HEAD = 64
R_WIDTH = MIX_W
R_HEADS = R_WIDTH // R_HEAD
R_DECAY_LORA = 64
R_A_LORA = 64
R_G_LORA = 160
R_PROJ = 3 * R_WIDTH + R_DECAY_LORA + R_A_LORA + R_G_LORA
GN_EPS = 64e-5
D_FF = 256 * ((8 * D_MODEL // 3 + 255) // 256)
NORM_EPS = 1e-6
PAGE_SIZE = 128

SEQ_PAD = 8
LANE = 128
MXU_DIM = 256
VMEM_LIMIT = 56 * 2**20
INT_MIN = -2**31
NEG_BIG = -1e30

C_Z, C_XS, C_BC, C_Q, C_QI, C_RR, C_RK, C_RV = (i * 1024 for i in range(8))
C_GATE = 8192
C_LORA = C_GATE + 3 * D_MODEL
LORA_W = 512
C_K = C_LORA + LORA_W
C_V = C_K + 256
C_MISC = C_V + 256
P_W = C_MISC + LANE
MISC_DT = IDX_DIM
MISC_WI = IDX_DIM + M_HEADS
RWKV_RC_CHUNK = 64

HI = lax.Precision.HIGHEST


def _cparams(sem):
    return pltpu.CompilerParams(dimension_semantics=sem, vmem_limit_bytes=VMEM_LIMIT)


def _mm(a, b):
    return jnp.dot(a.astype(BF16), b.astype(BF16), preferred_element_type=F32)


def _mm_nt(a, b):
    return lax.dot_general(a.astype(BF16), b.astype(BF16), (((1,), (1,)), ((), ())),
                           preferred_element_type=F32)


def _mm_tn(a, b):
    return lax.dot_general(a.astype(BF16), b.astype(BF16), (((0,), (0,)), ((), ())),
                           preferred_element_type=F32)


def _mm_hi(a, b):
    return jnp.dot(a, b, precision=HI, preferred_element_type=F32)


def _mm_nt_hi(a, b):
    return lax.dot_general(a, b, (((1,), (1,)), ((), ())), precision=HI, preferred_element_type=F32)


def _mm_tn_hi(a, b):
    return lax.dot_general(a, b, (((0,), (0,)), ((), ())), precision=HI, preferred_element_type=F32)


def _sigmoid(x):
    return jax.nn.sigmoid(x)


def _silu(x):
    return x * jax.nn.sigmoid(x)


def _full(shape):
    nd = len(shape)
    return pl.BlockSpec(shape, lambda *_: (0,) * nd)


def _ffn_kernel(x_ref, g_ref, wg_ref, wu_ref, wd_ref, o_ref, n_ref):
    f = pl.program_id(1)

    @pl.when(f == 0)
    def _():
        x = x_ref[...]
        ms = jnp.mean(x * x, axis=-1, keepdims=True)
        n_ref[...] = (x * lax.rsqrt(ms + NORM_EPS) * g_ref[...]).astype(BF16)
        o_ref[...] = x

    n = n_ref[...]
    a = jnp.dot(n, wg_ref[...], preferred_element_type=F32)
    b = jnp.dot(n, wu_ref[...], preferred_element_type=F32)
    hid = (0.5 * _silu(a) * b).astype(BF16)
    o_ref[...] += jnp.dot(hid, wd_ref[...], preferred_element_type=F32)


def _ffn(x, g, wg, wu, wd, tm=512, tf=512):
    m = x.shape[0]
    tm = min(tm, m)
    return pl.pallas_call(
        _ffn_kernel,
        grid=(m // tm, D_FF // tf),
        in_specs=[pl.BlockSpec((tm, D_MODEL), lambda i, f: (i, 0)),
                  pl.BlockSpec((1, D_MODEL), lambda i, f: (0, 0)),
                  pl.BlockSpec((D_MODEL, tf), lambda i, f: (0, f)),
                  pl.BlockSpec((D_MODEL, tf), lambda i, f: (0, f)),
                  pl.BlockSpec((tf, D_MODEL), lambda i, f: (f, 0))],
        out_specs=pl.BlockSpec((tm, D_MODEL), lambda i, f: (i, 0)),
        out_shape=jax.ShapeDtypeStruct((m, D_MODEL), F32),
        scratch_shapes=[pltpu.VMEM((tm, D_MODEL), BF16)],
        compiler_params=_cparams(("parallel", "arbitrary")),
        name="ffn",
    )(x, g, wg, wu, wd)


def _inproj_kernel(x_ref, g_ref, w_ref, o_ref, n_ref):
    j = pl.program_id(1)

    @pl.when(j == 0)
    def _():
        x = x_ref[...]
        ms = jnp.mean(x * x, axis=-1, keepdims=True)
        n_ref[...] = (x * lax.rsqrt(ms + NORM_EPS) * g_ref[...]).astype(BF16)

    o_ref[...] = jnp.dot(n_ref[...], w_ref[...], preferred_element_type=F32)


def _inproj(x, g, w, tm=512, tn=1408):
    m = x.shape[0]
    tm = min(tm, m)
    return pl.pallas_call(
        _inproj_kernel,
        grid=(m // tm, P_W // tn),
        in_specs=[pl.BlockSpec((tm, D_MODEL), lambda i, j: (i, 0)),
                  pl.BlockSpec((1, D_MODEL), lambda i, j: (0, 0)),
                  pl.BlockSpec((D_MODEL, tn), lambda i, j: (0, j))],
        out_specs=pl.BlockSpec((tm, tn), lambda i, j: (i, j)),
        out_shape=jax.ShapeDtypeStruct((m, P_W), F32),
        scratch_shapes=[pltpu.VMEM((tm, D_MODEL), BF16)],
        compiler_params=_cparams(("parallel", "arbitrary")),
        name="inproj",
    )(x, g, w)


def _merge_kernel(h_ref, ya_ref, yb_ref, yc_ref, ga_ref, gb_ref, gc_ref,
                  wa_ref, wb_ref, wc_ref, wo_ref, o_ref):
    j = pl.program_id(1)

    @pl.when(j == 0)
    def _():
        o_ref[...] = h_ref[...]

    merged = (_sigmoid(ga_ref[...]) * jnp.dot(ya_ref[...], wa_ref[...], preferred_element_type=F32)
              + _sigmoid(gb_ref[...]) * jnp.dot(yb_ref[...], wb_ref[...], preferred_element_type=F32)
              + _sigmoid(gc_ref[...]) * jnp.dot(yc_ref[...], wc_ref[...], preferred_element_type=F32))
    o_ref[...] += jnp.dot(merged.astype(BF16), wo_ref[...], preferred_element_type=F32)


def _merge(h, p, ya, yb, yc, wa, wb, wc, wo, tm=512, tk=512):
    m = h.shape[0]
    tm = min(tm, m)
    gb0 = C_GATE // tk
    nk = D_MODEL // tk
    yspec = pl.BlockSpec((tm, MIX_W), lambda i, j: (i, 0))
    wspec = pl.BlockSpec((MIX_W, tk), lambda i, j: (0, j))
    return pl.pallas_call(
        _merge_kernel,
        grid=(m // tm, nk),
        in_specs=[pl.BlockSpec((tm, D_MODEL), lambda i, j: (i, 0)), yspec, yspec, yspec,
                  pl.BlockSpec((tm, tk), lambda i, j: (i, gb0 + j)),
                  pl.BlockSpec((tm, tk), lambda i, j: (i, gb0 + nk + j)),
                  pl.BlockSpec((tm, tk), lambda i, j: (i, gb0 + 2 * nk + j)),
                  wspec, wspec, wspec,
                  pl.BlockSpec((tk, D_MODEL), lambda i, j: (j, 0))],
        out_specs=pl.BlockSpec((tm, D_MODEL), lambda i, j: (i, 0)),
        out_shape=jax.ShapeDtypeStruct((m, D_MODEL), F32),
        compiler_params=_cparams(("parallel", "arbitrary")),
        name="merge",
    )(h, ya, yb, yc, p, p, p, wa, wb, wc, wo)


def _final_norm_kernel(x_ref, g_ref, o_ref):
    x = x_ref[...]
    ms = jnp.mean(x * x, axis=-1, keepdims=True)
    o_ref[...] = x * lax.rsqrt(ms + NORM_EPS) * g_ref[...]


def _final_norm(x, g, tm=512):
    m = x.shape[0]
    tm = min(tm, m)
    return pl.pallas_call(
        _final_norm_kernel,
        grid=(m // tm,),
        in_specs=[pl.BlockSpec((tm, D_MODEL), lambda i: (i, 0)), _full((1, D_MODEL))],
        out_specs=pl.BlockSpec((tm, D_MODEL), lambda i: (i, 0)),
        out_shape=jax.ShapeDtypeStruct((m, D_MODEL), F32),
        compiler_params=_cparams(("parallel",)),
        name="final_norm",
    )(x, g)


def _kidx_ln_kernel(m_ref, g_ref, b_ref, o_ref):
    x = m_ref[:, 0:IDX_DIM]
    mu = jnp.mean(x, axis=-1, keepdims=True)
    var = jnp.mean(jnp.square(x - mu), axis=-1, keepdims=True)
    o_ref[...] = (x - mu) * lax.rsqrt(var + NORM_EPS) * g_ref[...] + b_ref[...]


def _kidx_ln(p, g, b, tm=512):
    m = p.shape[0]
    tm = min(tm, m)
    return pl.pallas_call(
        _kidx_ln_kernel,
        grid=(m // tm,),
        in_specs=[pl.BlockSpec((tm, LANE), lambda i: (i, C_MISC // LANE)),
                  _full((1, IDX_DIM)), _full((1, IDX_DIM))],
        out_specs=pl.BlockSpec((tm, IDX_DIM), lambda i: (i, 0)),
        out_shape=jax.ShapeDtypeStruct((m, IDX_DIM), F32),
        compiler_params=_cparams(("parallel",)),
        name="kidx_ln",
    )(p, g, b)


def _shifted(x, k, prev, rows):
    y = pltpu.roll(x, k, 0)
    for r in range(k):
        y = jnp.where(rows == r, prev[3 - k + r:4 - k + r, :], y)
    return y


def _ssd_kernel(*refs, T, n_valid, n_alias):
    (z_ref, xs_ref, bc_ref, misc_ref, cst_ref, st0_ref, cw_ref, cb_ref, dtb_ref,
     alog_ref, dexp_ref, norm_ref, e16_ref) = refs[:13]
    y_ref, stout_ref, prev_ref, st_ref, yd_ref = refs[13 + n_alias:]
    c = pl.program_id(1)
    nc = pl.num_programs(1)

    @pl.when(c == 0)
    def _():
        prev_ref[...] = cst_ref[0, 0]
        st_ref[...] = st0_ref[0, 0]

    rows = lax.broadcasted_iota(I32, (T, 1), 0)
    cw = cw_ref[...]
    prev = prev_ref[...]

    def conv(x, lo, hi):
        pv = prev[:, lo:hi]
        y = cb_ref[:, lo:hi] + x * cw[3:4, lo:hi]
        for k in (1, 2, 3):
            y = y + _shifted(x, k, pv, rows) * cw[3 - k:4 - k, lo:hi]
        return _silu(y)

    xs_raw = xs_ref[...]
    bc_raw = bc_ref[...]
    xs = conv(xs_raw, 0, M_INNER)
    bc = conv(bc_raw, M_INNER, CONV_DIM)
    prev_ref[:, 0:M_INNER] = xs_raw[T - 3:T, :]
    prev_ref[:, M_INNER:CONV_DIM] = bc_raw[T - 3:T, :]

    dt = jax.nn.softplus(misc_ref[:, MISC_DT:MISC_DT + M_HEADS] + dtb_ref[...])
    if n_valid < T:
        dt = jnp.where(rows < n_valid, dt, 0.0)
    ad = dt * (-jnp.exp(alog_ref[...]))
    ri = lax.broadcasted_iota(I32, (T, T), 0)
    ci = lax.broadcasted_iota(I32, (T, T), 1)
    lower = ri >= ci
    tril = lower.astype(F32)
    triu = (ri <= ci).astype(F32)
    eye16 = (lax.broadcasted_iota(I32, (M_HEADS, M_HEADS), 0)
             == lax.broadcasted_iota(I32, (M_HEADS, M_HEADS), 1)).astype(F32)
    cs = _mm_hi(tril, ad)
    cst = _mm_hi(_mm_nt_hi(eye16, ad), triu)
    e16 = e16_ref[...]
    dt_e = _mm_hi(dt, e16)
    ecs_e = _mm_hi(jnp.exp(cs), e16)
    wl_e = _mm_hi(jnp.exp(cs[T - 1:T, :] - cs), e16)
    xd = xs * dt_e
    xdw = (xd * wl_e).astype(BF16)
    xd = xd.astype(BF16)
    bcb = bc.astype(BF16)
    for g in range(M_GROUPS):
        bg = bcb[:, g * M_STATE:(g + 1) * M_STATE]
        cg = bcb[:, (M_GROUPS + g) * M_STATE:(M_GROUPS + g + 1) * M_STATE]
        cbm = _mm_nt(cg, bg)
        for hh in range(M_HEADS // M_GROUPS):
            h = g * (M_HEADS // M_GROUPS) + hh
            hs = slice(h * M_HEAD_DIM, (h + 1) * M_HEAD_DIM)
            diff = cs[:, h:h + 1] - cst[h:h + 1, :]
            lm = jnp.exp(jnp.where(lower, diff, -jnp.inf))
            yd = _mm(cbm * lm, xd[:, hs])
            st = st_ref[h]
            yo = _mm_nt(cg, st)
            yd_ref[:, hs] = yd + yo * ecs_e[:, hs]
            upd = _mm_tn(xdw[:, hs], bg)
            dec = jnp.exp(cst[h:h + 1, T - 1:T])
            st_ref[h] = st * dec + upd

    y = yd_ref[...] + xs * dexp_ref[...]
    y = y * _silu(z_ref[...])
    gw = M_INNER // M_GROUPS
    for g in range(M_GROUPS):
        yg = y[:, g * gw:(g + 1) * gw]
        ms = jnp.mean(yg * yg, axis=-1, keepdims=True)
        y_ref[:, g * gw:(g + 1) * gw] = (yg * lax.rsqrt(ms + NORM_EPS)
                                         * norm_ref[:, g * gw:(g + 1) * gw]).astype(y_ref.dtype)

    @pl.when(c == nc - 1)
    def _():
        stout_ref[0, 0] = st_ref[...]


def _alias_inputs(n_in, bufs):
    specs, args, aliases = [], [], {}
    for k, b in enumerate(bufs):
        if b is not None:
            aliases[n_in + len(args)] = k
            specs.append(pl.BlockSpec(memory_space=pl.ANY))
            args.append(b)
    return specs, args, aliases


def _ssd(p, row0, n_seq, n_chunks, T, n_valid, layer, conv_state, ssm0, cw, cb, dtb, alog, dexp, norm,
         e16, *, n_rows, depth, out_layer, y_buf=None, st_buf=None):
    rb0 = row0 // T

    def rmap(col):
        return lambda s, c: (rb0 + s * n_chunks + c, col)

    args = [p, p, p, p, conv_state, ssm0, cw, cb, dtb, alog, dexp, norm, e16]
    a_specs, a_args, aliases = _alias_inputs(len(args), (y_buf, st_buf))
    kern = functools.partial(_ssd_kernel, T=T, n_valid=n_valid, n_alias=len(a_args))
    return pl.pallas_call(
        kern,
        grid=(n_seq, n_chunks),
        in_specs=[pl.BlockSpec((T, 1024), rmap(C_Z // 1024)),
                  pl.BlockSpec((T, 1024), rmap(C_XS // 1024)),
                  pl.BlockSpec((T, 1024), rmap(C_BC // 1024)),
                  pl.BlockSpec((T, LANE), rmap(C_MISC // LANE)),
                  pl.BlockSpec((1, 1, CONV_W - 1, CONV_DIM), lambda s, c: (layer, s, 0, 0)),
                  pl.BlockSpec((1, 1, M_HEADS, M_HEAD_DIM, M_STATE), lambda s, c: (layer, s, 0, 0, 0)),
                  _full((CONV_W, CONV_DIM)), _full((1, CONV_DIM)), _full((1, M_HEADS)),
                  _full((1, M_HEADS)), _full((1, M_INNER)), _full((1, M_INNER)),
                  _full((M_HEADS, M_INNER))] + a_specs,
        out_specs=[pl.BlockSpec((T, M_INNER), rmap(0)),
                   pl.BlockSpec((1, 1, M_HEADS, M_HEAD_DIM, M_STATE), lambda s, c: (out_layer, s, 0, 0, 0))],
        out_shape=[jax.ShapeDtypeStruct((n_rows, M_INNER), BF16),
                   jax.ShapeDtypeStruct((depth, n_seq, M_HEADS, M_HEAD_DIM, M_STATE), F32)],
        scratch_shapes=[pltpu.VMEM((CONV_W - 1, CONV_DIM), F32),
                        pltpu.VMEM((M_HEADS, M_HEAD_DIM, M_STATE), F32),
                        pltpu.VMEM((T, M_INNER), F32)],
        input_output_aliases=aliases,
        compiler_params=_cparams(("parallel", "arbitrary")),
        name="ssd",
    )(*args, *a_args)


def _rwkv_kernel(*refs, T, n_valid, HG, n_alias):
    (r_ref, k_ref, v_ref, lo_ref, sh0_ref, s0_ref, mu_ref, w0_ref, w2_ref, a0_ref,
     a2_ref, g2_ref, kkw_ref, kaw_ref, rk_ref, gng_ref, gnb_ref) = refs[:17]
    y_ref, sout_ref, last_ref, s_ref, yb_ref = refs[17 + n_alias:]
    c = pl.program_id(1)
    nc = pl.num_programs(1)

    @pl.when(c == 0)
    def _():
        last_ref[...] = sh0_ref[0, 0]
        s_ref[...] = s0_ref[0, 0]

    rows = lax.broadcasted_iota(I32, (T, 1), 0)

    def mix(ref, lo, hi):
        x = ref[...]
        prev = jnp.where(rows == 0, last_ref[:, lo:hi], pltpu.roll(x, 1, 0))
        last_ref[:, lo:hi] = x[T - 1:T, :]
        return x + (prev - x) * mu_ref[:, lo:hi]

    r = mix(r_ref, 0, 1024)
    k = mix(k_ref, 1024, 2048)
    v = mix(v_ref, 2048, 3072)
    lo = mix(lo_ref, 3072, 3072 + LORA_W)
    xw = lo[:, 0:R_DECAY_LORA]
    xa = lo[:, R_DECAY_LORA:R_DECAY_LORA + R_A_LORA]
    xg = lo[:, R_DECAY_LORA + R_A_LORA:LORA_W]
    wl = -jax.nn.softplus(-(w0_ref[...] + _mm_hi(jnp.tanh(xw), w2_ref[...]))) - 0.5
    ld = -jnp.exp(wl)
    a = _sigmoid(a0_ref[...] + _mm_hi(xa, a2_ref[...]))
    gate = _mm_hi(_sigmoid(xg), g2_ref[...])
    kk = k * kkw_ref[...]
    km = k * (1.0 + (a - 1.0) * kaw_ref[...])
    if n_valid < T:
        ok = rows < n_valid
        ld = jnp.where(ok, ld, 0.0)
        kk = jnp.where(ok, kk, 0.0)
        km = jnp.where(ok, km, 0.0)
        v = jnp.where(ok, v, 0.0)

    ri = lax.broadcasted_iota(I32, (T, T), 0)
    ci = lax.broadcasted_iota(I32, (T, T), 1)
    cum = _mm_hi((ri >= ci).astype(F32), ld)
    eg = jnp.exp(cum)
    egm = jnp.exp(cum - ld)
    ei = jnp.exp(-cum)
    el = jnp.exp(cum[T - 1:T, :] - cum)
    r_eg = r * eg
    k_ei = km * ei
    k_el = km * el
    rk_sum = r * km * rk_ref[...]
    gng = gng_ref[...]
    gnb = gnb_ref[...]

    R = HG * T
    GW = HG * R_HEAD
    rr = lax.broadcasted_iota(I32, (R, R), 0)
    cc = lax.broadcasted_iota(I32, (R, R), 1)
    same = (rr // T) == (cc // T)
    strict = jnp.logical_and(same, rr > cc)
    incl = jnp.logical_and(same, rr >= cc)
    head_of = (lax.broadcasted_iota(I32, (R, GW), 0) // T) == (lax.broadcasted_iota(I32, (R, GW), 1) // R_HEAD)

    for gi in range(R_HEADS // HG):
        hsl = [slice((gi * HG + hh) * R_HEAD, (gi * HG + hh + 1) * R_HEAD) for hh in range(HG)]

        def stack(x):
            return jnp.concatenate([x[:, s_] for s_ in hsl], axis=0)

        kks = stack(kk)
        nrm = jnp.sqrt(jnp.sum(kks * kks, axis=-1, keepdims=True))
        kks = kks / jnp.maximum(nrm, 1e-12)
        bs = kks * stack(a)
        vs = stack(v)
        q2 = jnp.concatenate([kks * stack(egm), stack(r_eg)], axis=0)
        k2 = jnp.concatenate([stack(k_ei), bs * stack(ei)], axis=0)
        gm = _mm_nt(q2, k2)
        scat = s_ref[gi * HG:(gi + 1) * HG].reshape(GW, R_HEAD)
        hmw = _mm_nt(q2, scat)
        hk = jnp.concatenate([hmw[hh * T:(hh + 1) * T, hh * R_HEAD:(hh + 1) * R_HEAD]
                              for hh in range(HG)], axis=0)
        hr = jnp.concatenate([hmw[R + hh * T:R + (hh + 1) * T, hh * R_HEAD:(hh + 1) * R_HEAD]
                              for hh in range(HG)], axis=0)
        lk = jnp.where(strict, gm[0:R, 0:R], 0.0)
        lb = jnp.where(strict, gm[0:R, R:2 * R], 0.0)
        ak = jnp.where(incl, gm[R:2 * R, 0:R], 0.0)
        ab = jnp.where(incl, gm[R:2 * R, R:2 * R], 0.0)
        x = hk + _mm(lk, vs)
        zz = x - _mm(lb, x)
        lp = lb
        span = 2
        while span < T:
            lp = _mm(lp, lp)
            zz = zz + _mm(lp, zz)
            span *= 2
        u = -zz
        y = hr + _mm(jnp.concatenate([ak, ab], axis=1), jnp.concatenate([vs, u], axis=0))
        vw = jnp.where(head_of, jnp.concatenate([v[:, gi * GW:(gi + 1) * GW]] * HG, axis=0), 0.0)
        uw = jnp.where(head_of, jnp.concatenate([u] * HG, axis=1), 0.0)
        upd = _mm_tn(jnp.concatenate([vw, uw], axis=0),
                     jnp.concatenate([stack(k_el), bs * stack(el)], axis=0))
        g_last = jnp.concatenate([jnp.broadcast_to(eg[T - 1:T, s_], (R_HEAD, R_HEAD)) for s_ in hsl],
                                 axis=0)
        s_ref[gi * HG:(gi + 1) * HG] = (scat * g_last + upd).reshape(HG, R_HEAD, R_HEAD)
        mu = jnp.mean(y, axis=-1, keepdims=True)
        var = jnp.mean(jnp.square(y - mu), axis=-1, keepdims=True)
        gs = jnp.concatenate([jnp.broadcast_to(gng[:, s_], (T, R_HEAD)) for s_ in hsl], axis=0)
        gb = jnp.concatenate([jnp.broadcast_to(gnb[:, s_], (T, R_HEAD)) for s_ in hsl], axis=0)
        yn = (y - mu) * lax.rsqrt(var + GN_EPS) * gs + gb
        yn = yn + jnp.sum(stack(rk_sum), axis=-1, keepdims=True) * vs
        for hh in range(HG):
            yb_ref[:, hsl[hh]] = yn[hh * T:(hh + 1) * T]

    y_ref[...] = (yb_ref[...] * gate).astype(y_ref.dtype)

    @pl.when(c == nc - 1)
    def _():
        sout_ref[0, 0] = s_ref[...]


def _rwkv(p, row0, n_seq, n_chunks, T, n_valid, layer, shift0, s0, mu, w0, w2, a0, a2, g2, kkw, kaw,
          rk, gng, gnb, *, n_rows, depth, out_layer, y_buf=None, st_buf=None):
    rb0 = row0 // T

    def rmap(col):
        return lambda s, c: (rb0 + s * n_chunks + c, col)

    sw = 3072 + LORA_W
    hg = min(R_HEADS, max(1, MXU_DIM // T))
    args = [p, p, p, p, shift0, s0, mu, w0, w2, a0, a2, g2, kkw, kaw, rk, gng, gnb]
    a_specs, a_args, aliases = _alias_inputs(len(args), (y_buf, st_buf))
    kern = functools.partial(_rwkv_kernel, T=T, n_valid=n_valid, HG=hg, n_alias=len(a_args))
    return pl.pallas_call(
        kern,
        grid=(n_seq, n_chunks),
        in_specs=[pl.BlockSpec((T, 1024), rmap(C_RR // 1024)),
                  pl.BlockSpec((T, 1024), rmap(C_RK // 1024)),
                  pl.BlockSpec((T, 1024), rmap(C_RV // 1024)),
                  pl.BlockSpec((T, LORA_W), rmap(C_LORA // LORA_W)),
                  pl.BlockSpec((1, 1, 1, sw), lambda s, c: (layer, s, 0, 0)),
                  pl.BlockSpec((1, 1, R_HEADS, R_HEAD, R_HEAD), lambda s, c: (layer, s, 0, 0, 0)),
                  _full((1, sw)), _full((1, R_WIDTH)), _full((R_DECAY_LORA, R_WIDTH)),
                  _full((1, R_WIDTH)), _full((R_A_LORA, R_WIDTH)),
                  _full((LORA_W - R_DECAY_LORA - R_A_LORA, R_WIDTH)),
                  _full((1, R_WIDTH)), _full((1, R_WIDTH)), _full((1, R_WIDTH)),
                  _full((1, R_WIDTH)), _full((1, R_WIDTH))] + a_specs,
        out_specs=[pl.BlockSpec((T, R_WIDTH), rmap(0)),
                   pl.BlockSpec((1, 1, R_HEADS, R_HEAD, R_HEAD), lambda s, c: (out_layer, s, 0, 0, 0))],
        out_shape=[jax.ShapeDtypeStruct((n_rows, R_WIDTH), BF16),
                   jax.ShapeDtypeStruct((depth, n_seq, R_HEADS, R_HEAD, R_HEAD), F32)],
        scratch_shapes=[pltpu.VMEM((1, sw), F32),
                        pltpu.VMEM((R_HEADS, R_HEAD, R_HEAD), F32),
                        pltpu.VMEM((T, R_WIDTH), F32)],
        input_output_aliases=aliases,
        compiler_params=_cparams(("parallel", "arbitrary")),
        name="rwkv",
    )(*args, *a_args)


def _score_keys(acc, causal):
    bits = pltpu.bitcast(acc, I32)
    key = bits ^ ((bits >> 31) & 0x7FFFFFFF)
    return jnp.where(causal, key, INT_MIN)


def _kth_threshold(count_ge, shape, top, bits_per_step=1):
    tb = jnp.zeros(shape, I32)
    for shift in range(32 - bits_per_step, -1, -bits_per_step):
        digit = jnp.zeros(shape, I32)
        for d in range(1, 1 << bits_per_step):
            inc = d << shift
            inc = inc - (1 << 32) if inc >= (1 << 31) else inc
            cnt = count_ge((tb | jnp.int32(inc)) ^ jnp.int32(INT_MIN))
            digit = digit + jnp.where(cnt >= top, 1, 0)
        tb = tb | (digit << shift)
    return jnp.maximum(tb ^ jnp.int32(INT_MIN), INT_MIN + 1)


def _toeplitz_bias(base_row, nrows):
    return pltpu.roll(jnp.broadcast_to(base_row, (nrows, 2 * LANE)), 0, 1, stride=1, stride_axis=0)


def _dsa_prompt_kernel(q_ref, qi_ref, misc_ref, kit_ref, kt_ref, va_ref, base_ref, o_ref,
                       key_ref, mx_ref, acc_ref, *, top, TA):
    i = pl.program_id(0)
    QB = LANE
    q0 = i * QB
    n_a = (q0 + QB + TA - 1) // TA
    wi = misc_ref[:, MISC_WI:MISC_WI + IDX_HEADS] * (IDX_HEADS ** -0.5 * IDX_DIM ** -0.5)
    qi = qi_ref[...]
    qis = jnp.concatenate([qi[:, h * IDX_DIM:(h + 1) * IDX_DIM] for h in range(IDX_HEADS)],
                          axis=0).astype(BF16)
    rowq = q0 + lax.broadcasted_iota(I32, (QB, 1), 0)

    def score_tile(kt, carry):
        off = pl.multiple_of(kt * TA, TA)
        s = jnp.maximum(jnp.dot(qis, kit_ref[:, pl.ds(off, TA)], preferred_element_type=F32), 0.0)
        acc = s[0:QB] * wi[:, 0:1]
        for h in range(1, IDX_HEADS):
            acc = acc + s[h * QB:(h + 1) * QB] * wi[:, h:h + 1]
        col = off + lax.broadcasted_iota(I32, (1, TA), 1)
        key_ref[:, pl.ds(off, TA)] = _score_keys(acc, col <= rowq)
        return carry

    lax.fori_loop(0, n_a, score_tile, 0)

    def count_ge(cand):
        def body(kt, cnt):
            off = pl.multiple_of(kt * TA, TA)
            ge = jnp.where(key_ref[:, pl.ds(off, TA)] >= cand, 1.0, 0.0)
            for j in range(TA // LANE):
                cnt = cnt + ge[:, j * LANE:(j + 1) * LANE]
            return cnt
        cnt = lax.fori_loop(0, n_a, body, jnp.zeros((QB, LANE), F32))
        return jnp.sum(cnt, axis=-1, keepdims=True)

    thr = _kth_threshold(count_ge, (QB, 1), float(top))

    q = q_ref[...]
    lim = (i - 1) * QB
    n_far = (jnp.maximum(lim, 0) + TA - 1) // TA
    qgs, nears = [], []
    for g in range(KV_HEADS):
        heads = range(g * GQA, (g + 1) * GQA)
        qgs.append((jnp.concatenate([q[:, h * HEAD_DIM:(h + 1) * HEAD_DIM] for h in heads], axis=0)
                    * HEAD_DIM ** -0.5).astype(BF16))
        nears.append(jnp.concatenate(
            [_toeplitz_bias(base_ref[h:h + 1, :], QB) - base_ref[h:h + 1, 2 * LANE - 1:2 * LANE]
             for h in heads], axis=0))

    def far_logits(g, off, msk):
        s = jnp.dot(qgs[g], kt_ref[g, :, pl.ds(off, TA)], preferred_element_type=F32)
        return (s.reshape(GQA, QB, TA) + msk[None]).reshape(GQA * QB, TA)

    def far_mask(off):
        col = off + lax.broadcasted_iota(I32, (1, TA), 1)
        return jnp.where(key_ref[:, pl.ds(off, TA)] >= thr,
                         jnp.where(col < lim, 0.0, NEG_BIG), NEG_BIG)

    def near_logits(g, off, lo, msk):
        s = (jnp.dot(qgs[g], kt_ref[g, :, pl.ds(off, LANE)], preferred_element_type=F32)
             + nears[g][:, lo:lo + LANE])
        return (s.reshape(GQA, QB, LANE) + msk[None]).reshape(GQA * QB, LANE)

    def near_mask(off):
        return jnp.where(key_ref[:, pl.ds(off, LANE)] >= thr, 0.0, NEG_BIG)

    mx_ref[...] = jnp.full(mx_ref.shape, NEG_BIG, F32)

    def far_max(kt, carry):
        off = pl.multiple_of(kt * TA, TA)
        msk = far_mask(off)
        for g in range(KV_HEADS):
            s = far_logits(g, off, msk)
            m = s[:, 0:LANE]
            for j in range(1, TA // LANE):
                m = jnp.maximum(m, s[:, j * LANE:(j + 1) * LANE])
            mx_ref[g] = jnp.maximum(mx_ref[g], m)
        return carry

    def near_max(kt, lo):
        off = pl.multiple_of(kt * LANE, LANE)
        msk = near_mask(off)
        for g in range(KV_HEADS):
            mx_ref[g] = jnp.maximum(mx_ref[g], near_logits(g, off, lo, msk))

    lax.fori_loop(0, n_far, far_max, 0)

    @pl.when(i >= 1)
    def _():
        near_max(i - 1, 0)

    near_max(i, LANE)
    for g in range(KV_HEADS):
        mx_ref[g] = jnp.broadcast_to(jnp.max(mx_ref[g], axis=-1, keepdims=True), (GQA * QB, LANE))
    acc_ref[...] = jnp.zeros(acc_ref.shape, F32)

    def far_acc(kt, carry):
        off = pl.multiple_of(kt * TA, TA)
        msk = far_mask(off)
        for g in range(KV_HEADS):
            s = far_logits(g, off, msk)
            pr = jnp.exp(s - jnp.concatenate([mx_ref[g]] * (TA // LANE), axis=1))
            acc_ref[g] += jnp.dot(pr.astype(BF16), va_ref[g, pl.ds(off, TA), :],
                                  preferred_element_type=F32)
        return carry

    def near_acc(kt, lo):
        off = pl.multiple_of(kt * LANE, LANE)
        msk = near_mask(off)
        for g in range(KV_HEADS):
            pr = jnp.exp(near_logits(g, off, lo, msk) - mx_ref[g])
            acc_ref[g] += jnp.dot(pr.astype(BF16), va_ref[g, pl.ds(off, LANE), :],
                                  preferred_element_type=F32)

    lax.fori_loop(0, n_far, far_acc, 0)

    @pl.when(i >= 1)
    def _():
        near_acc(i - 1, 0)

    near_acc(i, LANE)
    for g in range(KV_HEADS):
        acc = acc_ref[g]
        out = acc[:, 0:HEAD_DIM] / acc[:, HEAD_DIM:HEAD_DIM + 1]
        for hh in range(GQA):
            h = g * GQA + hh
            o_ref[:, h * HEAD_DIM:(h + 1) * HEAD_DIM] = out[hh * QB:(hh + 1) * QB].astype(o_ref.dtype)


def _dsa_prompt(p, kit, kt, va, base, S, n_rows):
    top = min(TOPK_MAX, S // 4)
    TA = min(512, S)
    kern = functools.partial(_dsa_prompt_kernel, top=top, TA=TA)
    return pl.pallas_call(
        kern,
        grid=(S // LANE,),
        in_specs=[pl.BlockSpec((LANE, 1024), lambda i: (i, C_Q // 1024)),
                  pl.BlockSpec((LANE, 1024), lambda i: (i, C_QI // 1024)),
                  pl.BlockSpec((LANE, LANE), lambda i: (i, C_MISC // LANE)),
                  _full((IDX_DIM, S)), _full((KV_HEADS, HEAD_DIM, S)), _full((KV_HEADS, S, LANE)),
                  _full((ATT_HEADS, 2 * LANE))],
        out_specs=pl.BlockSpec((LANE, MIX_W), lambda i: (i, 0)),
        out_shape=jax.ShapeDtypeStruct((n_rows, MIX_W), BF16),
        scratch_shapes=[pltpu.VMEM((LANE, S), I32),
                        pltpu.VMEM((KV_HEADS, GQA * LANE, LANE), F32),
                        pltpu.VMEM((KV_HEADS, GQA * LANE, LANE), F32)],
        compiler_params=_cparams(("arbitrary",)),
        name="dsa_prompt",
    )(p, p, p, kit, kt, va, base)


def _dsa_sample_kernel(pt_ref, q_ref, qi_ref, misc_ref, kn_ref, vn_ref, kin_ref, base_ref, *rest,
                       n_pages, n_valid, top):
    ki_pages = rest[0:n_pages]
    k_pages = rest[n_pages:2 * n_pages]
    v_pages = rest[2 * n_pages:3 * n_pages]
    o_ref = rest[3 * n_pages + 1]
    kib_ref, kb_ref, vb_ref = rest[3 * n_pages + 2:]
    TQ = SEQ_PAD
    lo = n_pages * LANE
    NK = lo + LANE
    wi = misc_ref[:, MISC_WI:MISC_WI + IDX_HEADS] * (IDX_HEADS ** -0.5 * IDX_DIM ** -0.5)
    qi = qi_ref[...]
    qis = jnp.concatenate([qi[:, h * IDX_DIM:(h + 1) * IDX_DIM] for h in range(IDX_HEADS)], axis=0)

    eye = (lax.broadcasted_iota(I32, (HEAD_DIM, HEAD_DIM), 0)
           == lax.broadcasted_iota(I32, (HEAD_DIM, HEAD_DIM), 1)).astype(BF16)

    def new_slot(x):
        xt = _mm_nt(eye, x)
        return jnp.concatenate([xt, jnp.zeros((HEAD_DIM, LANE - TQ), F32)], axis=1).astype(BF16)

    for t in range(n_pages):
        kib_ref[:, t * LANE:(t + 1) * LANE] = ki_pages[t][0, 0].astype(BF16)
        for g in range(KV_HEADS):
            kb_ref[g, :, t * LANE:(t + 1) * LANE] = k_pages[t][0, 0, g].astype(BF16)
            vb_ref[g, :, t * LANE:(t + 1) * LANE] = v_pages[t][0, 0, g].astype(BF16)
    kib_ref[:, lo:NK] = new_slot(kin_ref[...])
    kn = kn_ref[...]
    vn = vn_ref[...]
    for g in range(KV_HEADS):
        kb_ref[g, :, lo:NK] = new_slot(kn[:, g * HEAD_DIM:(g + 1) * HEAD_DIM])
        vb_ref[g, :, lo:NK] = new_slot(vn[:, g * HEAD_DIM:(g + 1) * HEAD_DIM])

    s = jnp.maximum(_mm(qis, kib_ref[...]), 0.0)
    acc = s[0:TQ] * wi[:, 0:1]
    for h in range(1, IDX_HEADS):
        acc = acc + s[h * TQ:(h + 1) * TQ] * wi[:, h:h + 1]
    trow = lax.broadcasted_iota(I32, (TQ, 1), 0)
    new = lax.broadcasted_iota(I32, (1, NK), 1) - lo
    keys = _score_keys(acc, new <= jnp.minimum(trow, n_valid - 1))

    def count_ge(cand):
        return jnp.sum(jnp.where(keys >= cand, 1.0, 0.0), axis=-1, keepdims=True)

    thr = _kth_threshold(count_ge, (TQ, 1), float(top), bits_per_step=4)
    msk = jnp.where(keys >= thr, 0.0, NEG_BIG)
    msk = jnp.concatenate([msk] * GQA, axis=0)

    q = q_ref[...]
    for g in range(KV_HEADS):
        heads = range(g * GQA, (g + 1) * GQA)
        gs = slice(g * HEAD_DIM, (g + 1) * HEAD_DIM)
        qg = (jnp.concatenate([q[:, h * HEAD_DIM:(h + 1) * HEAD_DIM] for h in heads], axis=0)
              * HEAD_DIM ** -0.5)
        near = jnp.concatenate(
            [_toeplitz_bias(base_ref[h:h + 1, :], TQ) - base_ref[h:h + 1, 2 * LANE - 1:2 * LANE]
             for h in heads], axis=0)
        bias = jnp.concatenate([jnp.zeros((GQA * TQ, lo - LANE), F32), near], axis=1)
        s = _mm(qg, kb_ref[g]) + bias + msk
        mx = jnp.max(s, axis=-1, keepdims=True)
        pr = jnp.exp(s - mx)
        pr = pr / jnp.sum(pr, axis=-1, keepdims=True)
        out = _mm_nt(pr, vb_ref[g])
        for hh, h in enumerate(heads):
            o_ref[:, h * HEAD_DIM:(h + 1) * HEAD_DIM] = out[hh * TQ:(hh + 1) * TQ].astype(o_ref.dtype)


def _dsa_sample(p, row0, n_seq, kidx_ln, base, layer, cache_k, cache_v, cache_kidx, page_table, n_valid,
                y_buf):
    n_pages = page_table.shape[1]
    past = n_pages * PAGE_SIZE
    top = min(TOPK_MAX, (past + n_valid) // 4)
    rb0 = row0 // SEQ_PAD
    ck = jnp.transpose(cache_k, (0, 1, 3, 4, 2))
    cv = jnp.transpose(cache_v, (0, 1, 3, 4, 2))
    cki = jnp.transpose(cache_kidx, (0, 1, 3, 2))

    def rmap(col):
        return lambda b, pt: (rb0 + b, col)

    def pmap(pg):
        return lambda b, pt: (layer, pt[b, pg], 0, 0)

    def pmap5(pg):
        return lambda b, pt: (layer, pt[b, pg], 0, 0, 0)

    in_specs = [pl.BlockSpec((SEQ_PAD, 1024), rmap(C_Q // 1024)),
                pl.BlockSpec((SEQ_PAD, 1024), rmap(C_QI // 1024)),
                pl.BlockSpec((SEQ_PAD, LANE), rmap(C_MISC // LANE)),
                pl.BlockSpec((SEQ_PAD, 256), rmap(C_K // 256)),
                pl.BlockSpec((SEQ_PAD, 256), rmap(C_V // 256)),
                pl.BlockSpec((SEQ_PAD, IDX_DIM), lambda b, pt: (rb0 + b, 0)),
                pl.BlockSpec((ATT_HEADS, 2 * LANE), lambda b, pt: (0, 0))]
    in_specs += [pl.BlockSpec((1, 1, IDX_DIM, PAGE_SIZE), pmap(pg)) for pg in range(n_pages)]
    in_specs += [pl.BlockSpec((1, 1, KV_HEADS, HEAD_DIM, PAGE_SIZE), pmap5(pg)) for pg in range(n_pages)]
    in_specs += [pl.BlockSpec((1, 1, KV_HEADS, HEAD_DIM, PAGE_SIZE), pmap5(pg)) for pg in range(n_pages)]
    kern = functools.partial(_dsa_sample_kernel, n_pages=n_pages, n_valid=n_valid, top=top)
    nk = (n_pages + 1) * LANE
    return pl.pallas_call(
        kern,
        grid_spec=pltpu.PrefetchScalarGridSpec(
            num_scalar_prefetch=1,
            grid=(n_seq,),
            in_specs=in_specs + [pl.BlockSpec(memory_space=pl.ANY)],
            out_specs=pl.BlockSpec((SEQ_PAD, MIX_W), lambda b, pt: (rb0 + b, 0)),
            scratch_shapes=[pltpu.VMEM((IDX_DIM, nk), BF16),
                            pltpu.VMEM((KV_HEADS, HEAD_DIM, nk), BF16),
                            pltpu.VMEM((KV_HEADS, HEAD_DIM, nk), BF16)]),
        out_shape=jax.ShapeDtypeStruct(y_buf.shape, BF16),
        input_output_aliases={1 + len(in_specs): 0},
        compiler_params=_cparams(("arbitrary",)),
        name="dsa_sample",
    )(page_table, p, p, p, p, p, kidx_ln, base,
      *([cki] * n_pages), *([ck] * n_pages), *([cv] * n_pages), y_buf)


def _t5_base(t5_table):
    x = jnp.arange(2 * LANE, dtype=jnp.int32)
    rel = jnp.where(x <= LANE, LANE - x, T5_MAX_DIST)
    n = jnp.maximum(rel, 0)
    exact = T5_BUCKETS // 2
    nf = jnp.maximum(n, exact).astype(F32)
    large = exact + (jnp.log(nf / exact) / math.log(T5_MAX_DIST / exact)
                     * (T5_BUCKETS - exact)).astype(jnp.int32)
    bucket = jnp.where(n < exact, n, jnp.minimum(large, T5_BUCKETS - 1))
    return t5_table[bucket].astype(F32).T


def _relayout_w_in(w):
    widths = (M_INNER, CONV_DIM, M_HEADS, ATT_HEADS * HEAD_DIM, KV_HEADS * HEAD_DIM,
              KV_HEADS * HEAD_DIM, IDX_HEADS * IDX_DIM, IDX_DIM, IDX_HEADS, R_PROJ, 3 * D_MODEL)
    offs = [0]
    for wd in widths:
        offs.append(offs[-1] + wd)
    z, xbc, dt, q, k, v, qi, ki, wi, pr, gates = (w[:, offs[i]:offs[i + 1]] for i in range(len(widths)))
    zeros = lambda n: jnp.zeros((w.shape[0], n), w.dtype)
    lora = pr[:, 3 * R_WIDTH:]
    cols = [z, xbc, q, qi, pr[:, :3 * R_WIDTH], gates,
            lora, zeros(LORA_W - lora.shape[1]), k, v,
            ki, dt, wi, zeros(LANE - IDX_DIM - M_HEADS - IDX_HEADS)]
    out = jnp.concatenate(cols, axis=1).astype(BF16)
    assert out.shape[1] == P_W
    return out


def _pad_cols(x, n):
    return jnp.concatenate([x, jnp.zeros(x.shape[:-1] + (n - x.shape[-1],), x.dtype)], axis=-1)


def _shift_layout(x):
    return _pad_cols(x, 3 * R_WIDTH + LORA_W)


def kernel(x_prompt, x_sample, cache_k, cache_v, cache_kidx, page_table, state_ssm, state_conv, state_wkv, state_shift, ffn1_norm, ffn1_gate, ffn1_up, ffn1_down, mix_norm, w_in, conv_w, conv_b, dt_bias, a_log, d_skip, ssm_norm, kidx_ln_g, kidx_ln_b, t5_table, rwkv_mu, rwkv_w0, rwkv_w2, rwkv_a0, rwkv_a2, rwkv_g2, rwkv_kk, rwkv_ka, rwkv_rk, rwkv_gn_g, rwkv_gn_b, w_br_a, w_br_b, w_br_c, w_out, ffn2_norm, ffn2_gate, ffn2_up, ffn2_down, final_norm):
    bp, S, d = x_prompt.shape
    nb, T_dec, _ = x_sample.shape
    depth = w_in.shape[0]
    assert bp == 1 and T_dec <= SEQ_PAD and S % SSD_CHUNK == 0
    n_p = bp * S
    n_rows = n_p + nb * SEQ_PAD
    xs_pad = jnp.concatenate([x_sample, jnp.zeros((nb, SEQ_PAD - T_dec, d), x_sample.dtype)], axis=1)
    h = jnp.concatenate([x_prompt.reshape(n_p, d), xs_pad.reshape(nb * SEQ_PAD, d)], axis=0)

    base = _t5_base(t5_table)
    e16 = jnp.repeat(jnp.eye(M_HEADS, dtype=F32), M_HEAD_DIM, axis=1)
    row = lambda x: x.reshape(1, -1).astype(F32)
    rc = min(RWKV_RC_CHUNK, S)
    shift_all = _shift_layout(state_shift).reshape(depth, nb, 1, 3 * R_WIDTH + LORA_W)
    outs = {k_: [] for k_ in ("k_p", "v_p", "ki_p", "conv_p", "sh_p", "k_s", "v_s", "ki_s", "conv_s", "sh_s")}
    ssm_p = ssm_s = wkv_p = wkv_s = None
    for l in range(depth):
        bf = lambda x: x.astype(BF16)
        h = _ffn(h, row(ffn1_norm[l]), bf(ffn1_gate[l]), bf(ffn1_up[l]), bf(ffn1_down[l]))
        p = _inproj(h, row(mix_norm[l]), _relayout_w_in(w_in[l]))
        kidx = _kidx_ln(p, row(kidx_ln_g[l]), row(kidx_ln_b[l]))

        ssd_par = (conv_w[l], row(conv_b[l]), row(dt_bias[l]), row(a_log[l]),
                   row(jnp.repeat(d_skip[l], M_HEAD_DIM)), row(ssm_norm[l]), e16)
        stack = dict(n_rows=n_rows, depth=depth, out_layer=l)
        ya, ssm_p = _ssd(p, 0, bp, S // SSD_CHUNK, SSD_CHUNK, SSD_CHUNK, 0,
                         jnp.zeros((1, bp, CONV_W - 1, CONV_DIM), F32),
                         jnp.zeros((1, bp, M_HEADS, M_HEAD_DIM, M_STATE), F32), *ssd_par,
                         st_buf=ssm_p, **stack)
        ya, ssm_s = _ssd(p, n_p, nb, 1, SEQ_PAD, T_dec, l, state_conv, state_ssm, *ssd_par,
                         y_buf=ya, st_buf=ssm_s, **stack)

        g2p = jnp.concatenate([rwkv_g2[l], jnp.zeros((LORA_W - R_DECAY_LORA - R_A_LORA - R_G_LORA,
                                                      R_WIDTH), F32)], axis=0)
        rw_par = (row(_shift_layout(rwkv_mu[l])), row(rwkv_w0[l]), rwkv_w2[l], row(rwkv_a0[l]),
                  rwkv_a2[l], g2p, row(rwkv_kk[l]), row(rwkv_ka[l]), row(rwkv_rk[l]),
                  row(rwkv_gn_g[l]), row(rwkv_gn_b[l]))
        sw = 3 * R_WIDTH + LORA_W
        yc, wkv_p = _rwkv(p, 0, bp, S // rc, rc, rc, 0, jnp.zeros((1, bp, 1, sw), F32),
                          jnp.zeros((1, bp, R_HEADS, R_HEAD, R_HEAD), F32), *rw_par,
                          st_buf=wkv_p, **stack)
        yc, wkv_s = _rwkv(p, n_p, nb, 1, SEQ_PAD, T_dec, l, shift_all, state_wkv, *rw_par,
                          y_buf=yc, st_buf=wkv_s, **stack)

        kp = p[:n_p, C_K:C_K + 256]
        vp = p[:n_p, C_V:C_V + 256]
        kit = bf(kidx[:n_p].T)
        kt = bf(kp.reshape(n_p, KV_HEADS, HEAD_DIM).transpose(1, 2, 0))
        vg = vp.reshape(n_p, KV_HEADS, HEAD_DIM).transpose(1, 0, 2)
        va = bf(jnp.concatenate([vg, jnp.ones((KV_HEADS, n_p, 1), F32),
                                 jnp.zeros((KV_HEADS, n_p, LANE - HEAD_DIM - 1), F32)], axis=-1))
        yb = _dsa_prompt(p, kit, kt, va, base, S, n_rows)
        yb = _dsa_sample(p, n_p, nb, kidx, base, l, cache_k, cache_v, cache_kidx, page_table, T_dec, yb)
        h = _merge(h, p, ya, yb, yc, bf(w_br_a[l]), bf(w_br_b[l]), bf(w_br_c[l]), bf(w_out[l]))
        h = _ffn(h, row(ffn2_norm[l]), bf(ffn2_gate[l]), bf(ffn2_up[l]), bf(ffn2_down[l]))

        ps = p[n_p:].reshape(nb, SEQ_PAD, P_W)[:, :T_dec]
        xbc = lambda t: jnp.concatenate([t[..., C_XS:C_XS + 1024], t[..., C_BC:C_BC + 1024]], axis=-1)
        prj = lambda t: jnp.concatenate([t[..., C_RR:C_RR + 3 * R_WIDTH],
                                         t[..., C_LORA:C_LORA + R_PROJ - 3 * R_WIDTH]], axis=-1)
        outs["k_p"].append(kp.reshape(bp, S, KV_HEADS, HEAD_DIM))
        outs["v_p"].append(vp.reshape(bp, S, KV_HEADS, HEAD_DIM))
        outs["ki_p"].append(kidx[:n_p].reshape(bp, S, IDX_DIM))
        outs["conv_p"].append(xbc(p[n_p - (CONV_W - 1):n_p]).reshape(bp, CONV_W - 1, CONV_DIM))
        outs["sh_p"].append(prj(p[n_p - 1:n_p]).reshape(bp, R_PROJ))
        outs["k_s"].append(ps[..., C_K:C_K + 256].reshape(nb, T_dec, KV_HEADS, HEAD_DIM))
        outs["v_s"].append(ps[..., C_V:C_V + 256].reshape(nb, T_dec, KV_HEADS, HEAD_DIM))
        outs["ki_s"].append(kidx[n_p:].reshape(nb, SEQ_PAD, IDX_DIM)[:, :T_dec])
        conv_full = jnp.concatenate([state_conv[l], xbc(ps)], axis=1)
        outs["conv_s"].append(conv_full[:, T_dec:])
        outs["sh_s"].append(prj(ps[:, T_dec - 1]))

    y = _final_norm(h, row(final_norm))
    y_prompt = y[:n_p].reshape(bp, S, d)
    y_sample = y[n_p:].reshape(nb, SEQ_PAD, d)[:, :T_dec]
    stk = lambda name: jnp.stack(outs[name], axis=0)
    return (y_prompt, y_sample, stk("k_p"), stk("v_p"), stk("ki_p"), ssm_p, stk("conv_p"),
            wkv_p, stk("sh_p"), stk("k_s"), stk("v_s"), stk("ki_s"), ssm_s,
            stk("conv_s"), wkv_s, stk("sh_s"))
```

```python
import functools
import math

import jax
import jax.numpy as jnp
from jax import lax
from jax.experimental import pallas as pl
from jax.experimental.pallas import tpu as pltpu

F32 = jnp.float32
BF16 = jnp.bfloat16
I32 = jnp.int32

D_MODEL = 2048
MIX_W = D_MODEL // 2
M_HEAD_DIM = 64
M_INNER = MIX_W
M_HEADS = M_INNER // M_HEAD_DIM
M_GROUPS = 4
M_STATE = 128
CONV_W = 4
CONV_DIM = M_INNER + 2 * M_GROUPS * M_STATE
SSD_CHUNK = 128
HEAD_DIM = 64
ATT_HEADS = MIX_W // HEAD_DIM
KV_HEADS = 4
GQA = ATT_HEADS // KV_HEADS
IDX_HEADS = 16
IDX_DIM = 64
TOPK_MAX = 256
T5_BUCKETS = 32
T5_MAX_DIST = 128
R_HEAD = 64
R_WIDTH = MIX_W
R_HEADS = R_WIDTH // R_HEAD
R_DECAY_LORA = 64
R_A_LORA = 64
R_G_LORA = 160
R_PROJ = 3 * R_WIDTH + R_DECAY_LORA + R_A_LORA + R_G_LORA
GN_EPS = 64e-5
D_FF = 256 * ((8 * D_MODEL // 3 + 255) // 256)
NORM_EPS = 1e-6
PAGE_SIZE = 128

SEQ_PAD = 8
LANE = 128
SUBLANES = 8
MXU_DIM = 256
VMEM_LIMIT = 56 * 2**20
INT_MIN = -2**31
NEG_BIG = -1e30

C_Z, C_XS, C_BC, C_Q, C_QI, C_RR, C_RK, C_RV = (i * 1024 for i in range(8))
C_GATE = 8192
C_LORA = C_GATE + 3 * D_MODEL
LORA_W = 512
C_K = C_LORA + LORA_W
C_V = C_K + 256
C_MISC = C_V + 256
P_W = C_MISC + LANE
MISC_DT = IDX_DIM
MISC_WI = IDX_DIM + M_HEADS
RWKV_RC_CHUNK = 64

HI = lax.Precision.HIGHEST


def _cparams(sem):
    return pltpu.CompilerParams(dimension_semantics=sem, vmem_limit_bytes=VMEM_LIMIT)


def _mm(a, b):
    return jnp.dot(a.astype(BF16), b.astype(BF16), preferred_element_type=F32)


def _mm_nt(a, b):
    return lax.dot_general(a.astype(BF16), b.astype(BF16), (((1,), (1,)), ((), ())),
                           preferred_element_type=F32)


def _mm_tn(a, b):
    return lax.dot_general(a.astype(BF16), b.astype(BF16), (((0,), (0,)), ((), ())),
                           preferred_element_type=F32)


def _mm_hi(a, b):
    return jnp.dot(a, b, precision=HI, preferred_element_type=F32)


def _mm_nt_hi(a, b):
    return lax.dot_general(a, b, (((1,), (1,)), ((), ())), precision=HI, preferred_element_type=F32)


def _mm_tn_hi(a, b):
    return lax.dot_general(a, b, (((0,), (0,)), ((), ())), precision=HI, preferred_element_type=F32)


def _sigmoid(x):
    return jax.nn.sigmoid(x)


def _silu(x):
    return x * jax.nn.sigmoid(x)


def _full(shape):
    nd = len(shape)
    return pl.BlockSpec(shape, lambda *_: (0,) * nd)


def _ffn_kernel(x_ref, g_ref, wg_ref, wu_ref, wd_ref, o_ref, n_ref):
    f = pl.program_id(1)

    @pl.when(f == 0)
    def _():
        x = x_ref[...]
        ms = jnp.mean(x * x, axis=-1, keepdims=True)
        n_ref[...] = (x * lax.rsqrt(ms + NORM_EPS) * g_ref[...]).astype(BF16)
        o_ref[...] = x

    n = n_ref[...]
    a = jnp.dot(n, wg_ref[...], preferred_element_type=F32)
    b = jnp.dot(n, wu_ref[...], preferred_element_type=F32)
    hid = (0.5 * _silu(a) * b).astype(BF16)
    o_ref[...] += jnp.dot(hid, wd_ref[...], preferred_element_type=F32)


def _ffn(x, g, wg, wu, wd, tm=512, tf=512):
    m = x.shape[0]
    tm = min(tm, m)
    return pl.pallas_call(
        _ffn_kernel,
        grid=(m // tm, D_FF // tf),
        in_specs=[pl.BlockSpec((tm, D_MODEL), lambda i, f: (i, 0)),
                  pl.BlockSpec((1, D_MODEL), lambda i, f: (0, 0)),
                  pl.BlockSpec((D_MODEL, tf), lambda i, f: (0, f)),
                  pl.BlockSpec((D_MODEL, tf), lambda i, f: (0, f)),
                  pl.BlockSpec((tf, D_MODEL), lambda i, f: (f, 0))],
        out_specs=pl.BlockSpec((tm, D_MODEL), lambda i, f: (i, 0)),
        out_shape=jax.ShapeDtypeStruct((m, D_MODEL), F32),
        scratch_shapes=[pltpu.VMEM((tm, D_MODEL), BF16)],
        compiler_params=_cparams(("parallel", "arbitrary")),
        name="ffn",
    )(x, g, wg, wu, wd)


def _inproj_kernel(x_ref, g_ref, w_ref, o_ref, n_ref):
    j = pl.program_id(1)

    @pl.when(j == 0)
    def _():
        x = x_ref[...]
        ms = jnp.mean(x * x, axis=-1, keepdims=True)
        n_ref[...] = (x * lax.rsqrt(ms + NORM_EPS) * g_ref[...]).astype(BF16)

    o_ref[...] = jnp.dot(n_ref[...], w_ref[...], preferred_element_type=F32)


def _inproj(x, g, w, tm=512, tn=1408):
    m = x.shape[0]
    tm = min(tm, m)
    return pl.pallas_call(
        _inproj_kernel,
        grid=(m // tm, P_W // tn),
        in_specs=[pl.BlockSpec((tm, D_MODEL), lambda i, j: (i, 0)),
                  pl.BlockSpec((1, D_MODEL), lambda i, j: (0, 0)),
                  pl.BlockSpec((D_MODEL, tn), lambda i, j: (0, j))],
        out_specs=pl.BlockSpec((tm, tn), lambda i, j: (i, j)),
        out_shape=jax.ShapeDtypeStruct((m, P_W), F32),
        scratch_shapes=[pltpu.VMEM((tm, D_MODEL), BF16)],
        compiler_params=_cparams(("parallel", "arbitrary")),
        name="inproj",
    )(x, g, w)


def _merge_kernel(h_ref, ya_ref, yb_ref, yc_ref, ga_ref, gb_ref, gc_ref,
                  wa_ref, wb_ref, wc_ref, wo_ref, o_ref):
    j = pl.program_id(1)

    @pl.when(j == 0)
    def _():
        o_ref[...] = h_ref[...]

    merged = (_sigmoid(ga_ref[...]) * jnp.dot(ya_ref[...], wa_ref[...], preferred_element_type=F32)
              + _sigmoid(gb_ref[...]) * jnp.dot(yb_ref[...], wb_ref[...], preferred_element_type=F32)
              + _sigmoid(gc_ref[...]) * jnp.dot(yc_ref[...], wc_ref[...], preferred_element_type=F32))
    o_ref[...] += jnp.dot(merged.astype(BF16), wo_ref[...], preferred_element_type=F32)


def _merge(h, p, ya, yb, yc, wa, wb, wc, wo, tm=512, tk=512):
    m = h.shape[0]
    tm = min(tm, m)
    gb0 = C_GATE // tk
    nk = D_MODEL // tk
    yspec = pl.BlockSpec((tm, MIX_W), lambda i, j: (i, 0))
    wspec = pl.BlockSpec((MIX_W, tk), lambda i, j: (0, j))
    return pl.pallas_call(
        _merge_kernel,
        grid=(m // tm, nk),
        in_specs=[pl.BlockSpec((tm, D_MODEL), lambda i, j: (i, 0)), yspec, yspec, yspec,
                  pl.BlockSpec((tm, tk), lambda i, j: (i, gb0 + j)),
                  pl.BlockSpec((tm, tk), lambda i, j: (i, gb0 + nk + j)),
                  pl.BlockSpec((tm, tk), lambda i, j: (i, gb0 + 2 * nk + j)),
                  wspec, wspec, wspec,
                  pl.BlockSpec((tk, D_MODEL), lambda i, j: (j, 0))],
        out_specs=pl.BlockSpec((tm, D_MODEL), lambda i, j: (i, 0)),
        out_shape=jax.ShapeDtypeStruct((m, D_MODEL), F32),
        compiler_params=_cparams(("parallel", "arbitrary")),
        name="merge",
    )(h, ya, yb, yc, p, p, p, wa, wb, wc, wo)


def _final_norm_kernel(x_ref, g_ref, o_ref):
    x = x_ref[...]
    ms = jnp.mean(x * x, axis=-1, keepdims=True)
    o_ref[...] = x * lax.rsqrt(ms + NORM_EPS) * g_ref[...]


def _final_norm(x, g, tm=512):
    m = x.shape[0]
    tm = min(tm, m)
    return pl.pallas_call(
        _final_norm_kernel,
        grid=(m // tm,),
        in_specs=[pl.BlockSpec((tm, D_MODEL), lambda i: (i, 0)), _full((1, D_MODEL))],
        out_specs=pl.BlockSpec((tm, D_MODEL), lambda i: (i, 0)),
        out_shape=jax.ShapeDtypeStruct((m, D_MODEL), F32),
        compiler_params=_cparams(("parallel",)),
        name="final_norm",
    )(x, g)


def _kidx_ln_kernel(m_ref, g_ref, b_ref, o_ref):
    x = m_ref[:, 0:IDX_DIM]
    mu = jnp.mean(x, axis=-1, keepdims=True)
    var = jnp.mean(jnp.square(x - mu), axis=-1, keepdims=True)
    o_ref[...] = (x - mu) * lax.rsqrt(var + NORM_EPS) * g_ref[...] + b_ref[...]


def _kidx_ln(p, g, b, tm=512):
    m = p.shape[0]
    tm = min(tm, m)
    return pl.pallas_call(
        _kidx_ln_kernel,
        grid=(m // tm,),
        in_specs=[pl.BlockSpec((tm, LANE), lambda i: (i, C_MISC // LANE)),
                  _full((1, IDX_DIM)), _full((1, IDX_DIM))],
        out_specs=pl.BlockSpec((tm, IDX_DIM), lambda i: (i, 0)),
        out_shape=jax.ShapeDtypeStruct((m, IDX_DIM), F32),
        compiler_params=_cparams(("parallel",)),
        name="kidx_ln",
    )(p, g, b)


def _shifted(x, k, prev, rows):
    y = pltpu.roll(x, k, 0)
    for r in range(k):
        y = jnp.where(rows == r, prev[3 - k + r:4 - k + r, :], y)
    return y


def _ssd_kernel(*refs, T, n_valid, n_alias):
    (z_ref, xs_ref, bc_ref, misc_ref, cst_ref, st0_ref, cw_ref, cb_ref, dtb_ref,
     alog_ref, dexp_ref, norm_ref, e16_ref) = refs[:13]
    y_ref, stout_ref, prev_ref, st_ref, yd_ref = refs[13 + n_alias:]
    c = pl.program_id(1)
    nc = pl.num_programs(1)

    @pl.when(c == 0)
    def _():
        prev_ref[...] = cst_ref[0, 0]
        st_ref[...] = st0_ref[0, 0]

    rows = lax.broadcasted_iota(I32, (T, 1), 0)
    cw = cw_ref[...]
    prev = prev_ref[...]

    def conv(x, lo, hi):
        pv = prev[:, lo:hi]
        y = cb_ref[:, lo:hi] + x * cw[3:4, lo:hi]
        for k in (1, 2, 3):
            y = y + _shifted(x, k, pv, rows) * cw[3 - k:4 - k, lo:hi]
        return _silu(y)

    xs_raw = xs_ref[...]
    bc_raw = bc_ref[...]
    xs = conv(xs_raw, 0, M_INNER)
    bc = conv(bc_raw, M_INNER, CONV_DIM)
    prev_ref[:, 0:M_INNER] = xs_raw[T - 3:T, :]
    prev_ref[:, M_INNER:CONV_DIM] = bc_raw[T - 3:T, :]

    dt = jax.nn.softplus(misc_ref[:, MISC_DT:MISC_DT + M_HEADS] + dtb_ref[...])
    if n_valid < T:
        dt = jnp.where(rows < n_valid, dt, 0.0)
    ad = dt * (-jnp.exp(alog_ref[...]))
    ri = lax.broadcasted_iota(I32, (T, T), 0)
    ci = lax.broadcasted_iota(I32, (T, T), 1)
    lower = ri >= ci
    tril = lower.astype(F32)
    triu = (ri <= ci).astype(F32)
    eye16 = (lax.broadcasted_iota(I32, (M_HEADS, M_HEADS), 0)
             == lax.broadcasted_iota(I32, (M_HEADS, M_HEADS), 1)).astype(F32)
    cs = _mm_hi(tril, ad)
    cst = _mm_hi(_mm_nt_hi(eye16, ad), triu)
    e16 = e16_ref[...]
    dt_e = _mm_hi(dt, e16)
    ecs_e = _mm_hi(jnp.exp(cs), e16)
    wl_e = _mm_hi(jnp.exp(cs[T - 1:T, :] - cs), e16)
    xd = xs * dt_e
    xdw = (xd * wl_e).astype(BF16)
    xd = xd.astype(BF16)
    bcb = bc.astype(BF16)
    for g in range(M_GROUPS):
        bg = bcb[:, g * M_STATE:(g + 1) * M_STATE]
        cg = bcb[:, (M_GROUPS + g) * M_STATE:(M_GROUPS + g + 1) * M_STATE]
        cbm = _mm_nt(cg, bg)
        for hh in range(M_HEADS // M_GROUPS):
            h = g * (M_HEADS // M_GROUPS) + hh
            hs = slice(h * M_HEAD_DIM, (h + 1) * M_HEAD_DIM)
            diff = cs[:, h:h + 1] - cst[h:h + 1, :]
            lm = jnp.exp(jnp.where(lower, diff, -jnp.inf))
            yd = _mm(cbm * lm, xd[:, hs])
            st = st_ref[h]
            yo = _mm_nt(cg, st)
            yd_ref[:, hs] = yd + yo * ecs_e[:, hs]
            upd = _mm_tn(xdw[:, hs], bg)
            dec = jnp.exp(cst[h:h + 1, T - 1:T])
            st_ref[h] = st * dec + upd

    y = yd_ref[...] + xs * dexp_ref[...]
    y = y * _silu(z_ref[...])
    gw = M_INNER // M_GROUPS
    for g in range(M_GROUPS):
        yg = y[:, g * gw:(g + 1) * gw]
        ms = jnp.mean(yg * yg, axis=-1, keepdims=True)
        y_ref[:, g * gw:(g + 1) * gw] = (yg * lax.rsqrt(ms + NORM_EPS)
                                         * norm_ref[:, g * gw:(g + 1) * gw]).astype(y_ref.dtype)

    @pl.when(c == nc - 1)
    def _():
        stout_ref[0, 0] = st_ref[...]


def _alias_inputs(n_in, bufs):
    specs, args, aliases = [], [], {}
    for k, b in enumerate(bufs):
        if b is not None:
            aliases[n_in + len(args)] = k
            specs.append(pl.BlockSpec(memory_space=pl.ANY))
            args.append(b)
    return specs, args, aliases


def _ssd(p, row0, n_seq, n_chunks, T, n_valid, layer, conv_state, ssm0, cw, cb, dtb, alog, dexp, norm,
         e16, *, n_rows, depth, out_layer, y_buf=None, st_buf=None):
    rb0 = row0 // T

    def rmap(col):
        return lambda s, c: (rb0 + s * n_chunks + c, col)

    args = [p, p, p, p, conv_state, ssm0, cw, cb, dtb, alog, dexp, norm, e16]
    a_specs, a_args, aliases = _alias_inputs(len(args), (y_buf, st_buf))
    kern = functools.partial(_ssd_kernel, T=T, n_valid=n_valid, n_alias=len(a_args))
    return pl.pallas_call(
        kern,
        grid=(n_seq, n_chunks),
        in_specs=[pl.BlockSpec((T, 1024), rmap(C_Z // 1024)),
                  pl.BlockSpec((T, 1024), rmap(C_XS // 1024)),
                  pl.BlockSpec((T, 1024), rmap(C_BC // 1024)),
                  pl.BlockSpec((T, LANE), rmap(C_MISC // LANE)),
                  pl.BlockSpec((1, 1, CONV_W - 1, CONV_DIM), lambda s, c: (layer, s, 0, 0)),
                  pl.BlockSpec((1, 1, M_HEADS, M_HEAD_DIM, M_STATE), lambda s, c: (layer, s, 0, 0, 0)),
                  _full((CONV_W, CONV_DIM)), _full((1, CONV_DIM)), _full((1, M_HEADS)),
                  _full((1, M_HEADS)), _full((1, M_INNER)), _full((1, M_INNER)),
                  _full((M_HEADS, M_INNER))] + a_specs,
        out_specs=[pl.BlockSpec((T, M_INNER), rmap(0)),
                   pl.BlockSpec((1, 1, M_HEADS, M_HEAD_DIM, M_STATE), lambda s, c: (out_layer, s, 0, 0, 0))],
        out_shape=[jax.ShapeDtypeStruct((n_rows, M_INNER), BF16),
                   jax.ShapeDtypeStruct((depth, n_seq, M_HEADS, M_HEAD_DIM, M_STATE), F32)],
        scratch_shapes=[pltpu.VMEM((CONV_W - 1, CONV_DIM), F32),
                        pltpu.VMEM((M_HEADS, M_HEAD_DIM, M_STATE), F32),
                        pltpu.VMEM((T, M_INNER), F32)],
        input_output_aliases=aliases,
        compiler_params=_cparams(("parallel", "arbitrary")),
        name="ssd",
    )(*args, *a_args)


def _rwkv_kernel(*refs, T, n_valid, HG, n_alias):
    (r_ref, k_ref, v_ref, lo_ref, sh0_ref, s0_ref, mu_ref, w0_ref, w2_ref, a0_ref,
     a2_ref, g2_ref, kkw_ref, kaw_ref, rk_ref, gng_ref, gnb_ref) = refs[:17]
    y_ref, sout_ref, last_ref, s_ref, yb_ref, xs_ref = refs[17 + n_alias:]
    c = pl.program_id(1)
    nc = pl.num_programs(1)

    @pl.when(c == 0)
    def _():
        last_ref[...] = sh0_ref[0, 0]
        s_ref[...] = s0_ref[0, 0]

    rows = lax.broadcasted_iota(I32, (T, 1), 0)

    def mix(ref, lo, hi):
        x = ref[...]
        prev = jnp.where(rows == 0, last_ref[:, lo:hi], pltpu.roll(x, 1, 0))
        last_ref[:, lo:hi] = x[T - 1:T, :]
        return x + (prev - x) * mu_ref[:, lo:hi]

    r = mix(r_ref, 0, 1024)
    k = mix(k_ref, 1024, 2048)
    v = mix(v_ref, 2048, 3072)
    lo = mix(lo_ref, 3072, 3072 + LORA_W)
    xw = lo[:, 0:R_DECAY_LORA]
    xa = lo[:, R_DECAY_LORA:R_DECAY_LORA + R_A_LORA]
    xg = lo[:, R_DECAY_LORA + R_A_LORA:LORA_W]
    wl = -jax.nn.softplus(-(w0_ref[...] + _mm_hi(jnp.tanh(xw), w2_ref[...]))) - 0.5
    ld = -jnp.exp(wl)
    a = _sigmoid(a0_ref[...] + _mm_hi(xa, a2_ref[...]))
    gate = _mm_hi(_sigmoid(xg), g2_ref[...])
    kk = k * kkw_ref[...]
    km = k * (1.0 + (a - 1.0) * kaw_ref[...])
    if n_valid < T:
        ok = rows < n_valid
        ld = jnp.where(ok, ld, 0.0)
        kk = jnp.where(ok, kk, 0.0)
        km = jnp.where(ok, km, 0.0)
        v = jnp.where(ok, v, 0.0)

    ri = lax.broadcasted_iota(I32, (T, T), 0)
    ci = lax.broadcasted_iota(I32, (T, T), 1)
    cum = _mm_hi((ri >= ci).astype(F32), ld)
    eg = jnp.exp(cum)
    egm = jnp.exp(cum - ld)
    ei = jnp.exp(-cum)
    el = jnp.exp(cum[T - 1:T, :] - cum)
    r_eg = r * eg
    k_ei = km * ei
    k_el = km * el
    rk_sum = r * km * rk_ref[...]
    gng = gng_ref[...]
    gnb = gnb_ref[...]

    R = HG * T
    GW = HG * R_HEAD
    rr = lax.broadcasted_iota(I32, (R, R), 0)
    cc = lax.broadcasted_iota(I32, (R, R), 1)
    same = (rr // T) == (cc // T)
    strict = jnp.logical_and(same, rr > cc)
    incl = jnp.logical_and(same, rr >= cc)
    head_of = (lax.broadcasted_iota(I32, (R, GW), 0) // T) == (lax.broadcasted_iota(I32, (R, GW), 1) // R_HEAD)
    blk_r = (rr % T) // SUBLANES
    blk_c = (cc % T) // SUBLANES

    for gi in range(R_HEADS // HG):
        hsl = [slice((gi * HG + hh) * R_HEAD, (gi * HG + hh + 1) * R_HEAD) for hh in range(HG)]

        def stack(x):
            return jnp.concatenate([x[:, s_] for s_ in hsl], axis=0)

        kks = stack(kk)
        nrm = jnp.sqrt(jnp.sum(kks * kks, axis=-1, keepdims=True))
        kks = kks / jnp.maximum(nrm, 1e-12)
        bs = kks * stack(a)
        vs = stack(v)
        q2 = jnp.concatenate([kks * stack(egm), stack(r_eg)], axis=0)
        k2 = jnp.concatenate([stack(k_ei), bs * stack(ei)], axis=0)
        gm = _mm_nt(q2, k2)
        scat = s_ref[gi * HG:(gi + 1) * HG].reshape(GW, R_HEAD)
        hmw = _mm_nt(q2, scat)
        hk = jnp.concatenate([hmw[hh * T:(hh + 1) * T, hh * R_HEAD:(hh + 1) * R_HEAD]
                              for hh in range(HG)], axis=0)
        hr = jnp.concatenate([hmw[R + hh * T:R + (hh + 1) * T, hh * R_HEAD:(hh + 1) * R_HEAD]
                              for hh in range(HG)], axis=0)
        lk = jnp.where(strict, gm[0:R, 0:R], 0.0)
        ak = jnp.where(incl, gm[R:2 * R, 0:R], 0.0)
        ab = jnp.where(incl, gm[R:2 * R, R:2 * R], 0.0)
        x = hk + _mm(lk, vs)
        lb = jnp.where(strict, gm[0:R, R:2 * R], 0.0)
        xg_ref = xs_ref.at[gi]
        xg_ref[...] = x
        for jb in range(T // SUBLANES):
            for hh in range(HG):
                r0 = hh * T + jb * SUBLANES
                xb = xg_ref[r0:r0 + SUBLANES, :]
                for s in range(SUBLANES - 1):
                    xb = xb - lb[r0:r0 + SUBLANES, r0 + s:r0 + s + 1] * xb[s:s + 1, :]
                xg_ref[r0:r0 + SUBLANES, :] = xb
            if jb + 1 < T // SUBLANES:
                later = jnp.where(jnp.logical_and(blk_c == jb, blk_r > jb), lb, 0.0)
                xg_ref[...] = xg_ref[...] - _mm(later, xg_ref[...])
        u = -xg_ref[...]
        y = hr + _mm(jnp.concatenate([ak, ab], axis=1), jnp.concatenate([vs, u], axis=0))
        vw = jnp.where(head_of, jnp.concatenate([v[:, gi * GW:(gi + 1) * GW]] * HG, axis=0), 0.0)
        uw = jnp.where(head_of, jnp.concatenate([u] * HG, axis=1), 0.0)
        upd = _mm_tn(jnp.concatenate([vw, uw], axis=0),
                     jnp.concatenate([stack(k_el), bs * stack(el)], axis=0))
        g_last = jnp.concatenate([jnp.broadcast_to(eg[T - 1:T, s_], (R_HEAD, R_HEAD)) for s_ in hsl],
                                 axis=0)
        s_ref[gi * HG:(gi + 1) * HG] = (scat * g_last + upd).reshape(HG, R_HEAD, R_HEAD)
        mu = jnp.mean(y, axis=-1, keepdims=True)
        var = jnp.mean(jnp.square(y - mu), axis=-1, keepdims=True)
        gs = jnp.concatenate([jnp.broadcast_to(gng[:, s_], (T, R_HEAD)) for s_ in hsl], axis=0)
        gb = jnp.concatenate([jnp.broadcast_to(gnb[:, s_], (T, R_HEAD)) for s_ in hsl], axis=0)
        yn = (y - mu) * lax.rsqrt(var + GN_EPS) * gs + gb
        yn = yn + jnp.sum(stack(rk_sum), axis=-1, keepdims=True) * vs
        for hh in range(HG):
            yb_ref[:, hsl[hh]] = yn[hh * T:(hh + 1) * T]

    y_ref[...] = (yb_ref[...] * gate).astype(y_ref.dtype)

    @pl.when(c == nc - 1)
    def _():
        sout_ref[0, 0] = s_ref[...]


def _rwkv(p, row0, n_seq, n_chunks, T, n_valid, layer, shift0, s0, mu, w0, w2, a0, a2, g2, kkw, kaw,
          rk, gng, gnb, *, n_rows, depth, out_layer, y_buf=None, st_buf=None):
    rb0 = row0 // T

    def rmap(col):
        return lambda s, c: (rb0 + s * n_chunks + c, col)

    sw = 3072 + LORA_W
    hg = min(R_HEADS, max(1, MXU_DIM // T))
    args = [p, p, p, p, shift0, s0, mu, w0, w2, a0, a2, g2, kkw, kaw, rk, gng, gnb]
    a_specs, a_args, aliases = _alias_inputs(len(args), (y_buf, st_buf))
    kern = functools.partial(_rwkv_kernel, T=T, n_valid=n_valid, HG=hg, n_alias=len(a_args))
    return pl.pallas_call(
        kern,
        grid=(n_seq, n_chunks),
        in_specs=[pl.BlockSpec((T, 1024), rmap(C_RR // 1024)),
                  pl.BlockSpec((T, 1024), rmap(C_RK // 1024)),
                  pl.BlockSpec((T, 1024), rmap(C_RV // 1024)),
                  pl.BlockSpec((T, LORA_W), rmap(C_LORA // LORA_W)),
                  pl.BlockSpec((1, 1, 1, sw), lambda s, c: (layer, s, 0, 0)),
                  pl.BlockSpec((1, 1, R_HEADS, R_HEAD, R_HEAD), lambda s, c: (layer, s, 0, 0, 0)),
                  _full((1, sw)), _full((1, R_WIDTH)), _full((R_DECAY_LORA, R_WIDTH)),
                  _full((1, R_WIDTH)), _full((R_A_LORA, R_WIDTH)),
                  _full((LORA_W - R_DECAY_LORA - R_A_LORA, R_WIDTH)),
                  _full((1, R_WIDTH)), _full((1, R_WIDTH)), _full((1, R_WIDTH)),
                  _full((1, R_WIDTH)), _full((1, R_WIDTH))] + a_specs,
        out_specs=[pl.BlockSpec((T, R_WIDTH), rmap(0)),
                   pl.BlockSpec((1, 1, R_HEADS, R_HEAD, R_HEAD), lambda s, c: (out_layer, s, 0, 0, 0))],
        out_shape=[jax.ShapeDtypeStruct((n_rows, R_WIDTH), BF16),
                   jax.ShapeDtypeStruct((depth, n_seq, R_HEADS, R_HEAD, R_HEAD), F32)],
        scratch_shapes=[pltpu.VMEM((1, sw), F32),
                        pltpu.VMEM((R_HEADS, R_HEAD, R_HEAD), F32),
                        pltpu.VMEM((T, R_WIDTH), F32),
                        pltpu.VMEM((R_HEADS // hg, hg * T, R_HEAD), F32)],
        input_output_aliases=aliases,
        compiler_params=_cparams(("parallel", "arbitrary")),
        name="rwkv",
    )(*args, *a_args)


def _score_keys(acc, causal):
    bits = pltpu.bitcast(acc, I32)
    key = bits ^ ((bits >> 31) & 0x7FFFFFFF)
    return jnp.where(causal, key, INT_MIN)


def _kth_threshold(count_ge, shape, top, bits_per_step=1):
    tb = jnp.zeros(shape, I32)
    for shift in range(32 - bits_per_step, -1, -bits_per_step):
        digit = jnp.zeros(shape, I32)
        for d in range(1, 1 << bits_per_step):
            inc = d << shift
            inc = inc - (1 << 32) if inc >= (1 << 31) else inc
            cnt = count_ge((tb | jnp.int32(inc)) ^ jnp.int32(INT_MIN))
            digit = digit + jnp.where(cnt >= top, 1, 0)
        tb = tb | (digit << shift)
    return jnp.maximum(tb ^ jnp.int32(INT_MIN), INT_MIN + 1)


def _toeplitz_bias(base_row, nrows):
    return pltpu.roll(jnp.broadcast_to(base_row, (nrows, 2 * LANE)), 0, 1, stride=1, stride_axis=0)


def _dsa_prompt_kernel(q_ref, qi_ref, misc_ref, kit_ref, kt_ref, va_ref, base_ref, o_ref,
                       key_ref, mx_ref, acc_ref, *, top, TA):
    i = pl.program_id(0)
    QB = LANE
    q0 = i * QB
    n_a = (q0 + QB + TA - 1) // TA
    wi = misc_ref[:, MISC_WI:MISC_WI + IDX_HEADS] * (IDX_HEADS ** -0.5 * IDX_DIM ** -0.5)
    qi = qi_ref[...]
    qis = jnp.concatenate([qi[:, h * IDX_DIM:(h + 1) * IDX_DIM] for h in range(IDX_HEADS)],
                          axis=0).astype(BF16)
    rowq = q0 + lax.broadcasted_iota(I32, (QB, 1), 0)

    def score_tile(kt, carry):
        off = pl.multiple_of(kt * TA, TA)
        s = jnp.maximum(jnp.dot(qis, kit_ref[:, pl.ds(off, TA)], preferred_element_type=F32), 0.0)
        acc = s[0:QB] * wi[:, 0:1]
        for h in range(1, IDX_HEADS):
            acc = acc + s[h * QB:(h + 1) * QB] * wi[:, h:h + 1]
        col = off + lax.broadcasted_iota(I32, (1, TA), 1)
        key_ref[:, pl.ds(off, TA)] = _score_keys(acc, col <= rowq)
        return carry

    lax.fori_loop(0, n_a, score_tile, 0)

    def count_ge(cand):
        def body(kt, cnt):
            off = pl.multiple_of(kt * TA, TA)
            ge = jnp.where(key_ref[:, pl.ds(off, TA)] >= cand, 1.0, 0.0)
            for j in range(TA // LANE):
                cnt = cnt + ge[:, j * LANE:(j + 1) * LANE]
            return cnt
        cnt = lax.fori_loop(0, n_a, body, jnp.zeros((QB, LANE), F32))
        return jnp.sum(cnt, axis=-1, keepdims=True)

    thr = _kth_threshold(count_ge, (QB, 1), float(top))

    n_ge = count_ge(thr)

    @pl.when(jnp.max(n_ge) > top)
    def _():
        def count_tied_before(pos):
            def body(kt, cnt):
                off = pl.multiple_of(kt * TA, TA)
                col = off + lax.broadcasted_iota(I32, (1, TA), 1)
                hit = jnp.where(key_ref[:, pl.ds(off, TA)] == thr, jnp.where(col < pos, 1.0, 0.0), 0.0)
                for j in range(TA // LANE):
                    cnt = cnt + hit[:, j * LANE:(j + 1) * LANE]
                return cnt
            cnt = lax.fori_loop(0, n_a, body, jnp.zeros((QB, LANE), F32))
            return jnp.sum(cnt, axis=-1, keepdims=True)

        need = top - count_ge(thr + 1)
        last = jnp.zeros((QB, 1), I32)
        for bit in range(max(1, (kit_ref.shape[1] - 1).bit_length()) - 1, -1, -1):
            cand = last | jnp.int32(1 << bit)
            last = jnp.where(count_tied_before(cand) < need, cand, last)

        def drop(kt, carry):
            off = pl.multiple_of(kt * TA, TA)
            col = off + lax.broadcasted_iota(I32, (1, TA), 1)
            k = key_ref[:, pl.ds(off, TA)]
            key_ref[:, pl.ds(off, TA)] = jnp.where(k == thr, jnp.where(col > last, INT_MIN, k), k)
            return carry

        lax.fori_loop(0, n_a, drop, 0)

    q = q_ref[...]
    lim = (i - 1) * QB
    n_far = (jnp.maximum(lim, 0) + TA - 1) // TA
    qgs, nears = [], []
    for g in range(KV_HEADS):
        heads = range(g * GQA, (g + 1) * GQA)
        qgs.append((jnp.concatenate([q[:, h * HEAD_DIM:(h + 1) * HEAD_DIM] for h in heads], axis=0)
                    * HEAD_DIM ** -0.5).astype(BF16))
        nears.append(jnp.concatenate(
            [_toeplitz_bias(base_ref[h:h + 1, :], QB) - base_ref[h:h + 1, 2 * LANE - 1:2 * LANE]
             for h in heads], axis=0))

    def far_logits(g, off, msk):
        s = jnp.dot(qgs[g], kt_ref[g, :, pl.ds(off, TA)], preferred_element_type=F32)
        return (s.reshape(GQA, QB, TA) + msk[None]).reshape(GQA * QB, TA)

    def far_mask(off):
        col = off + lax.broadcasted_iota(I32, (1, TA), 1)
        return jnp.where(key_ref[:, pl.ds(off, TA)] >= thr,
                         jnp.where(col < lim, 0.0, NEG_BIG), NEG_BIG)

    def near_logits(g, off, lo, msk):
        s = (jnp.dot(qgs[g], kt_ref[g, :, pl.ds(off, LANE)], preferred_element_type=F32)
             + nears[g][:, lo:lo + LANE])
        return (s.reshape(GQA, QB, LANE) + msk[None]).reshape(GQA * QB, LANE)

    def near_mask(off):
        return jnp.where(key_ref[:, pl.ds(off, LANE)] >= thr, 0.0, NEG_BIG)

    mx_ref[...] = jnp.full(mx_ref.shape, NEG_BIG, F32)
    acc_ref[...] = jnp.zeros(acc_ref.shape, F32)

    def accumulate(g, s, v_tile):
        width = s.shape[1]
        m = s[:, 0:LANE]
        for j in range(1, width // LANE):
            m = jnp.maximum(m, s[:, j * LANE:(j + 1) * LANE])
        m_old = mx_ref[g]
        m_new = jnp.maximum(m_old, jnp.broadcast_to(jnp.max(m, axis=-1, keepdims=True), m.shape))
        pr = jnp.exp(s - jnp.concatenate([m_new] * (width // LANE), axis=1))
        acc_ref[g] = (acc_ref[g] * jnp.exp(m_old - m_new)
                      + jnp.dot(pr.astype(BF16), v_tile, preferred_element_type=F32))
        mx_ref[g] = m_new

    def far_acc(kt, carry):
        off = pl.multiple_of(kt * TA, TA)
        msk = far_mask(off)
        for g in range(KV_HEADS):
            accumulate(g, far_logits(g, off, msk), va_ref[g, pl.ds(off, TA), :])
        return carry

    def near_acc(kt, lo):
        off = pl.multiple_of(kt * LANE, LANE)
        msk = near_mask(off)
        for g in range(KV_HEADS):
            accumulate(g, near_logits(g, off, lo, msk), va_ref[g, pl.ds(off, LANE), :])

    lax.fori_loop(0, n_far, far_acc, 0)

    @pl.when(i >= 1)
    def _():
        near_acc(i - 1, 0)

    near_acc(i, LANE)
    for g in range(KV_HEADS):
        acc = acc_ref[g]
        out = acc[:, 0:HEAD_DIM] / acc[:, HEAD_DIM:HEAD_DIM + 1]
        for hh in range(GQA):
            h = g * GQA + hh
            o_ref[:, h * HEAD_DIM:(h + 1) * HEAD_DIM] = out[hh * QB:(hh + 1) * QB].astype(o_ref.dtype)


def _dsa_prompt(p, kit, kt, va, base, S, n_rows):
    top = min(TOPK_MAX, S // 4)
    TA = min(512, S)
    kern = functools.partial(_dsa_prompt_kernel, top=top, TA=TA)
    return pl.pallas_call(
        kern,
        grid=(S // LANE,),
        in_specs=[pl.BlockSpec((LANE, 1024), lambda i: (i, C_Q // 1024)),
                  pl.BlockSpec((LANE, 1024), lambda i: (i, C_QI // 1024)),
                  pl.BlockSpec((LANE, LANE), lambda i: (i, C_MISC // LANE)),
                  _full((IDX_DIM, S)), _full((KV_HEADS, HEAD_DIM, S)), _full((KV_HEADS, S, LANE)),
                  _full((ATT_HEADS, 2 * LANE))],
        out_specs=pl.BlockSpec((LANE, MIX_W), lambda i: (i, 0)),
        out_shape=jax.ShapeDtypeStruct((n_rows, MIX_W), BF16),
        scratch_shapes=[pltpu.VMEM((LANE, S), I32),
                        pltpu.VMEM((KV_HEADS, GQA * LANE, LANE), F32),
                        pltpu.VMEM((KV_HEADS, GQA * LANE, LANE), F32)],
        compiler_params=_cparams(("arbitrary",)),
        name="dsa_prompt",
    )(p, p, p, kit, kt, va, base)


def _dsa_sample_kernel(pt_ref, q_ref, qi_ref, misc_ref, kn_ref, vn_ref, kin_ref, base_ref, *rest,
                       n_pages, n_valid, top):
    ki_pages = rest[0:n_pages]
    k_pages = rest[n_pages:2 * n_pages]
    v_pages = rest[2 * n_pages:3 * n_pages]
    o_ref = rest[3 * n_pages + 1]
    kib_ref, kb_ref, vb_ref, msk_ref = rest[3 * n_pages + 2:]
    TQ = SEQ_PAD
    lo = n_pages * LANE
    NK = lo + LANE
    wi = misc_ref[:, MISC_WI:MISC_WI + IDX_HEADS] * (IDX_HEADS ** -0.5 * IDX_DIM ** -0.5)
    qi = qi_ref[...]
    qis = jnp.concatenate([qi[:, h * IDX_DIM:(h + 1) * IDX_DIM] for h in range(IDX_HEADS)], axis=0)

    eye = (lax.broadcasted_iota(I32, (HEAD_DIM, HEAD_DIM), 0)
           == lax.broadcasted_iota(I32, (HEAD_DIM, HEAD_DIM), 1)).astype(BF16)

    def new_slot(x):
        xt = _mm_nt(eye, x)
        return jnp.concatenate([xt, jnp.zeros((HEAD_DIM, LANE - TQ), F32)], axis=1).astype(BF16)

    for t in range(n_pages):
        kib_ref[:, t * LANE:(t + 1) * LANE] = ki_pages[t][0, 0].astype(BF16)
        for g in range(KV_HEADS):
            kb_ref[g, :, t * LANE:(t + 1) * LANE] = k_pages[t][0, 0, g].astype(BF16)
            vb_ref[g, :, t * LANE:(t + 1) * LANE] = v_pages[t][0, 0, g].astype(BF16)
    kib_ref[:, lo:NK] = new_slot(kin_ref[...])
    kn = kn_ref[...]
    vn = vn_ref[...]
    for g in range(KV_HEADS):
        kb_ref[g, :, lo:NK] = new_slot(kn[:, g * HEAD_DIM:(g + 1) * HEAD_DIM])
        vb_ref[g, :, lo:NK] = new_slot(vn[:, g * HEAD_DIM:(g + 1) * HEAD_DIM])

    s = jnp.maximum(_mm(qis, kib_ref[...]), 0.0)
    acc = s[0:TQ] * wi[:, 0:1]
    for h in range(1, IDX_HEADS):
        acc = acc + s[h * TQ:(h + 1) * TQ] * wi[:, h:h + 1]
    trow = lax.broadcasted_iota(I32, (TQ, 1), 0)
    new = lax.broadcasted_iota(I32, (1, NK), 1) - lo
    keys = _score_keys(acc, new <= jnp.minimum(trow, n_valid - 1))

    def count_ge(cand):
        return jnp.sum(jnp.where(keys >= cand, 1.0, 0.0), axis=-1, keepdims=True)

    thr = _kth_threshold(count_ge, (TQ, 1), float(top), bits_per_step=4)
    msk_ref[...] = jnp.where(keys >= thr, 0.0, NEG_BIG)

    @pl.when(jnp.max(count_ge(thr)) > top)
    def _():
        pos = new + lo
        tied = keys == thr

        def count_tied_before(p):
            return jnp.sum(jnp.where(tied, jnp.where(pos < p, 1.0, 0.0), 0.0), axis=-1, keepdims=True)

        need = top - count_ge(thr + 1)
        last = jnp.zeros((TQ, 1), I32)
        for bit in range(max(1, (NK - 1).bit_length()) - 1, -1, -1):
            cand = last | jnp.int32(1 << bit)
            last = jnp.where(count_tied_before(cand) < need, cand, last)
        msk_ref[...] = jnp.where(tied, jnp.where(pos > last, NEG_BIG, 0.0), msk_ref[...])

    msk = jnp.concatenate([msk_ref[...]] * GQA, axis=0)

    q = q_ref[...]
    for g in range(KV_HEADS):
        heads = range(g * GQA, (g + 1) * GQA)
        gs = slice(g * HEAD_DIM, (g + 1) * HEAD_DIM)
        qg = (jnp.concatenate([q[:, h * HEAD_DIM:(h + 1) * HEAD_DIM] for h in heads], axis=0)
              * HEAD_DIM ** -0.5)
        near = jnp.concatenate(
            [_toeplitz_bias(base_ref[h:h + 1, :], TQ) - base_ref[h:h + 1, 2 * LANE - 1:2 * LANE]
             for h in heads], axis=0)
        bias = jnp.concatenate([jnp.zeros((GQA * TQ, lo - LANE), F32), near], axis=1)
        s = _mm(qg, kb_ref[g]) + bias + msk
        mx = jnp.max(s, axis=-1, keepdims=True)
        pr = jnp.exp(s - mx)
        pr = pr / jnp.sum(pr, axis=-1, keepdims=True)
        out = _mm_nt(pr, vb_ref[g])
        for hh, h in enumerate(heads):
            o_ref[:, h * HEAD_DIM:(h + 1) * HEAD_DIM] = out[hh * TQ:(hh + 1) * TQ].astype(o_ref.dtype)


def _dsa_sample(p, row0, n_seq, kidx_ln, base, layer, cache_k, cache_v, cache_kidx, page_table, n_valid,
                y_buf):
    n_pages = page_table.shape[1]
    past = n_pages * PAGE_SIZE
    top = min(TOPK_MAX, (past + n_valid) // 4)
    rb0 = row0 // SEQ_PAD
    ck = jnp.transpose(cache_k, (0, 1, 3, 4, 2))
    cv = jnp.transpose(cache_v, (0, 1, 3, 4, 2))
    cki = jnp.transpose(cache_kidx, (0, 1, 3, 2))

    def rmap(col):
        return lambda b, pt: (rb0 + b, col)

    def pmap(pg):
        return lambda b, pt: (layer, pt[b, pg], 0, 0)

    def pmap5(pg):
        return lambda b, pt: (layer, pt[b, pg], 0, 0, 0)

    in_specs = [pl.BlockSpec((SEQ_PAD, 1024), rmap(C_Q // 1024)),
                pl.BlockSpec((SEQ_PAD, 1024), rmap(C_QI // 1024)),
                pl.BlockSpec((SEQ_PAD, LANE), rmap(C_MISC // LANE)),
                pl.BlockSpec((SEQ_PAD, 256), rmap(C_K // 256)),
                pl.BlockSpec((SEQ_PAD, 256), rmap(C_V // 256)),
                pl.BlockSpec((SEQ_PAD, IDX_DIM), lambda b, pt: (rb0 + b, 0)),
                pl.BlockSpec((ATT_HEADS, 2 * LANE), lambda b, pt: (0, 0))]
    in_specs += [pl.BlockSpec((1, 1, IDX_DIM, PAGE_SIZE), pmap(pg)) for pg in range(n_pages)]
    in_specs += [pl.BlockSpec((1, 1, KV_HEADS, HEAD_DIM, PAGE_SIZE), pmap5(pg)) for pg in range(n_pages)]
    in_specs += [pl.BlockSpec((1, 1, KV_HEADS, HEAD_DIM, PAGE_SIZE), pmap5(pg)) for pg in range(n_pages)]
    kern = functools.partial(_dsa_sample_kernel, n_pages=n_pages, n_valid=n_valid, top=top)
    nk = (n_pages + 1) * LANE
    return pl.pallas_call(
        kern,
        grid_spec=pltpu.PrefetchScalarGridSpec(
            num_scalar_prefetch=1,
            grid=(n_seq,),
            in_specs=in_specs + [pl.BlockSpec(memory_space=pl.ANY)],
            out_specs=pl.BlockSpec((SEQ_PAD, MIX_W), lambda b, pt: (rb0 + b, 0)),
            scratch_shapes=[pltpu.VMEM((IDX_DIM, nk), BF16),
                            pltpu.VMEM((KV_HEADS, HEAD_DIM, nk), BF16),
                            pltpu.VMEM((KV_HEADS, HEAD_DIM, nk), BF16),
                            pltpu.VMEM((SEQ_PAD, nk), F32)]),
        out_shape=jax.ShapeDtypeStruct(y_buf.shape, BF16),
        input_output_aliases={1 + len(in_specs): 0},
        compiler_params=_cparams(("arbitrary",)),
        name="dsa_sample",
    )(page_table, p, p, p, p, p, kidx_ln, base,
      *([cki] * n_pages), *([ck] * n_pages), *([cv] * n_pages), y_buf)


def _t5_base(t5_table):
    x = jnp.arange(2 * LANE, dtype=jnp.int32)
    rel = jnp.where(x <= LANE, LANE - x, T5_MAX_DIST)
    n = jnp.maximum(rel, 0)
    exact = T5_BUCKETS // 2
    nf = jnp.maximum(n, exact).astype(F32)
    large = exact + (jnp.log(nf / exact) / math.log(T5_MAX_DIST / exact)
                     * (T5_BUCKETS - exact)).astype(jnp.int32)
    bucket = jnp.where(n < exact, n, jnp.minimum(large, T5_BUCKETS - 1))
    return t5_table[bucket].astype(F32).T


def _relayout_w_in(w):
    widths = (M_INNER, CONV_DIM, M_HEADS, ATT_HEADS * HEAD_DIM, KV_HEADS * HEAD_DIM,
              KV_HEADS * HEAD_DIM, IDX_HEADS * IDX_DIM, IDX_DIM, IDX_HEADS, R_PROJ, 3 * D_MODEL)
    offs = [0]
    for wd in widths:
        offs.append(offs[-1] + wd)
    z, xbc, dt, q, k, v, qi, ki, wi, pr, gates = (w[:, offs[i]:offs[i + 1]] for i in range(len(widths)))
    zeros = lambda n: jnp.zeros((w.shape[0], n), w.dtype)
    lora = pr[:, 3 * R_WIDTH:]
    cols = [z, xbc, q, qi, pr[:, :3 * R_WIDTH], gates,
            lora, zeros(LORA_W - lora.shape[1]), k, v,
            ki, dt, wi, zeros(LANE - IDX_DIM - M_HEADS - IDX_HEADS)]
    out = jnp.concatenate(cols, axis=1).astype(BF16)
    assert out.shape[1] == P_W
    return out


def _pad_cols(x, n):
    return jnp.concatenate([x, jnp.zeros(x.shape[:-1] + (n - x.shape[-1],), x.dtype)], axis=-1)


def _shift_layout(x):
    return _pad_cols(x, 3 * R_WIDTH + LORA_W)


def kernel(x_prompt, x_sample, cache_k, cache_v, cache_kidx, page_table, state_ssm, state_conv, state_wkv, state_shift, ffn1_norm, ffn1_gate, ffn1_up, ffn1_down, mix_norm, w_in, conv_w, conv_b, dt_bias, a_log, d_skip, ssm_norm, kidx_ln_g, kidx_ln_b, t5_table, rwkv_mu, rwkv_w0, rwkv_w2, rwkv_a0, rwkv_a2, rwkv_g2, rwkv_kk, rwkv_ka, rwkv_rk, rwkv_gn_g, rwkv_gn_b, w_br_a, w_br_b, w_br_c, w_out, ffn2_norm, ffn2_gate, ffn2_up, ffn2_down, final_norm):
    bp, S, d = x_prompt.shape
    nb, T_dec, _ = x_sample.shape
    depth = w_in.shape[0]
    assert bp == 1 and T_dec <= SEQ_PAD and S % SSD_CHUNK == 0
    n_p = bp * S
    n_rows = n_p + nb * SEQ_PAD
    xs_pad = jnp.concatenate([x_sample, jnp.zeros((nb, SEQ_PAD - T_dec, d), x_sample.dtype)], axis=1)
    h = jnp.concatenate([x_prompt.reshape(n_p, d), xs_pad.reshape(nb * SEQ_PAD, d)], axis=0)

    base = _t5_base(t5_table)
    e16 = jnp.repeat(jnp.eye(M_HEADS, dtype=F32), M_HEAD_DIM, axis=1)
    row = lambda x: x.reshape(1, -1).astype(F32)
    rc = min(RWKV_RC_CHUNK, S)
    shift_all = _shift_layout(state_shift).reshape(depth, nb, 1, 3 * R_WIDTH + LORA_W)
    outs = {k_: [] for k_ in ("k_p", "v_p", "ki_p", "conv_p", "sh_p", "k_s", "v_s", "ki_s", "conv_s", "sh_s")}
    ssm_p = ssm_s = wkv_p = wkv_s = None
    for l in range(depth):
        bf = lambda x: x.astype(BF16)
        h = _ffn(h, row(ffn1_norm[l]), bf(ffn1_gate[l]), bf(ffn1_up[l]), bf(ffn1_down[l]))
        p = _inproj(h, row(mix_norm[l]), _relayout_w_in(w_in[l]))
        kidx = _kidx_ln(p, row(kidx_ln_g[l]), row(kidx_ln_b[l]))

        ssd_par = (conv_w[l], row(conv_b[l]), row(dt_bias[l]), row(a_log[l]),
                   row(jnp.repeat(d_skip[l], M_HEAD_DIM)), row(ssm_norm[l]), e16)
        stack = dict(n_rows=n_rows, depth=depth, out_layer=l)
        ya, ssm_p = _ssd(p, 0, bp, S // SSD_CHUNK, SSD_CHUNK, SSD_CHUNK, 0,
                         jnp.zeros((1, bp, CONV_W - 1, CONV_DIM), F32),
                         jnp.zeros((1, bp, M_HEADS, M_HEAD_DIM, M_STATE), F32), *ssd_par,
                         st_buf=ssm_p, **stack)
        ya, ssm_s = _ssd(p, n_p, nb, 1, SEQ_PAD, T_dec, l, state_conv, state_ssm, *ssd_par,
                         y_buf=ya, st_buf=ssm_s, **stack)

        g2p = jnp.concatenate([rwkv_g2[l], jnp.zeros((LORA_W - R_DECAY_LORA - R_A_LORA - R_G_LORA,
                                                      R_WIDTH), F32)], axis=0)
        rw_par = (row(_shift_layout(rwkv_mu[l])), row(rwkv_w0[l]), rwkv_w2[l], row(rwkv_a0[l]),
                  rwkv_a2[l], g2p, row(rwkv_kk[l]), row(rwkv_ka[l]), row(rwkv_rk[l]),
                  row(rwkv_gn_g[l]), row(rwkv_gn_b[l]))
        sw = 3 * R_WIDTH + LORA_W
        yc, wkv_p = _rwkv(p, 0, bp, S // rc, rc, rc, 0, jnp.zeros((1, bp, 1, sw), F32),
                          jnp.zeros((1, bp, R_HEADS, R_HEAD, R_HEAD), F32), *rw_par,
                          st_buf=wkv_p, **stack)
        yc, wkv_s = _rwkv(p, n_p, nb, 1, SEQ_PAD, T_dec, l, shift_all, state_wkv, *rw_par,
                          y_buf=yc, st_buf=wkv_s, **stack)

        kp = p[:n_p, C_K:C_K + 256]
        vp = p[:n_p, C_V:C_V + 256]
        kit = bf(kidx[:n_p].T)
        kt = bf(kp.reshape(n_p, KV_HEADS, HEAD_DIM).transpose(1, 2, 0))
        vg = vp.reshape(n_p, KV_HEADS, HEAD_DIM).transpose(1, 0, 2)
        va = bf(jnp.concatenate([vg, jnp.ones((KV_HEADS, n_p, 1), F32),
                                 jnp.zeros((KV_HEADS, n_p, LANE - HEAD_DIM - 1), F32)], axis=-1))
        yb = _dsa_prompt(p, kit, kt, va, base, S, n_rows)
        yb = _dsa_sample(p, n_p, nb, kidx, base, l, cache_k, cache_v, cache_kidx, page_table, T_dec, yb)
        h = _merge(h, p, ya, yb, yc, bf(w_br_a[l]), bf(w_br_b[l]), bf(w_br_c[l]), bf(w_out[l]))
        h = _ffn(h, row(ffn2_norm[l]), bf(ffn2_gate[l]), bf(ffn2_up[l]), bf(ffn2_down[l]))

        ps = p[n_p:].reshape(nb, SEQ_PAD, P_W)[:, :T_dec]
        xbc = lambda t: jnp.concatenate([t[..., C_XS:C_XS + 1024], t[..., C_BC:C_BC + 1024]], axis=-1)
        prj = lambda t: jnp.concatenate([t[..., C_RR:C_RR + 3 * R_WIDTH],
                                         t[..., C_LORA:C_LORA + R_PROJ - 3 * R_WIDTH]], axis=-1)
        outs["k_p"].append(kp.reshape(bp, S, KV_HEADS, HEAD_DIM))
        outs["v_p"].append(vp.reshape(bp, S, KV_HEADS, HEAD_DIM))
        outs["ki_p"].append(kidx[:n_p].reshape(bp, S, IDX_DIM))
        outs["conv_p"].append(xbc(p[n_p - (CONV_W - 1):n_p]).reshape(bp, CONV_W - 1, CONV_DIM))
        outs["sh_p"].append(prj(p[n_p - 1:n_p]).reshape(bp, R_PROJ))
        outs["k_s"].append(ps[..., C_K:C_K + 256].reshape(nb, T_dec, KV_HEADS, HEAD_DIM))
        outs["v_s"].append(ps[..., C_V:C_V + 256].reshape(nb, T_dec, KV_HEADS, HEAD_DIM))
        outs["ki_s"].append(kidx[n_p:].reshape(nb, SEQ_PAD, IDX_DIM)[:, :T_dec])
        conv_full = jnp.concatenate([state_conv[l], xbc(ps)], axis=1)
        outs["conv_s"].append(conv_full[:, T_dec:])
        outs["sh_s"].append(prj(ps[:, T_dec - 1]))

    y = _final_norm(h, row(final_norm))
    y_prompt = y[:n_p].reshape(bp, S, d)
    y_sample = y[n_p:].reshape(nb, SEQ_PAD, d)[:, :T_dec]
    stk = lambda name: jnp.stack(outs[name], axis=0)
    return (y_prompt, y_sample, stk("k_p"), stk("v_p"), stk("ki_p"), ssm_p, stk("conv_p"),
            wkv_p, stk("sh_p"), stk("k_s"), stk("v_s"), stk("ki_s"), ssm_s,
            stk("conv_s"), wkv_s, stk("sh_s"))
```

```python
import functools
import math

import jax
import jax.numpy as jnp
from jax import lax
from jax.experimental import pallas as pl
from jax.experimental.pallas import tpu as pltpu

F32 = jnp.float32
BF16 = jnp.bfloat16
I32 = jnp.int32

D_MODEL = 2048
MIX_W = D_MODEL // 2
M_HEAD_DIM = 64
M_INNER = MIX_W
M_HEADS = M_INNER // M_HEAD_DIM
M_GROUPS = 4
M_STATE = 128
CONV_W = 4
CONV_DIM = M_INNER + 2 * M_GROUPS * M_STATE
SSD_CHUNK = 128
HEAD_DIM = 64
ATT_HEADS = MIX_W // HEAD_DIM
KV_HEADS = 4
GQA = ATT_HEADS // KV_HEADS
IDX_HEADS = 16
IDX_DIM = 64
TOPK_MAX = 256
T5_BUCKETS = 32
T5_MAX_DIST = 128
R_HEAD = 64
R_WIDTH = MIX_W
R_HEADS = R_WIDTH // R_HEAD
R_DECAY_LORA = 64
R_A_LORA = 64
R_G_LORA = 160
R_PROJ = 3 * R_WIDTH + R_DECAY_LORA + R_A_LORA + R_G_LORA
GN_EPS = 64e-5
D_FF = 256 * ((8 * D_MODEL // 3 + 255) // 256)
NORM_EPS = 1e-6
PAGE_SIZE = 128

SEQ_PAD = 8
LANE = 128
SUBLANES = 8
MXU_DIM = 256
VMEM_LIMIT = 56 * 2**20
INT_MIN = -2**31
NEG_BIG = -1e30

C_Z, C_XS, C_BC, C_Q, C_QI, C_RR, C_RK, C_RV = (i * 1024 for i in range(8))
C_GATE = 8192
C_LORA = C_GATE + 3 * D_MODEL
LORA_W = 512
C_K = C_LORA + LORA_W
C_V = C_K + 256
C_MISC = C_V + 256
P_W = C_MISC + LANE
MISC_DT = IDX_DIM
MISC_WI = IDX_DIM + M_HEADS
RWKV_RC_CHUNK = 64

HI = lax.Precision.HIGHEST


def _cparams(sem):
    return pltpu.CompilerParams(dimension_semantics=sem, vmem_limit_bytes=VMEM_LIMIT)


def _mm(a, b):
    return jnp.dot(a.astype(BF16), b.astype(BF16), preferred_element_type=F32)


def _mm_nt(a, b):
    return lax.dot_general(a.astype(BF16), b.astype(BF16), (((1,), (1,)), ((), ())),
                           preferred_element_type=F32)


def _mm_tn(a, b):
    return lax.dot_general(a.astype(BF16), b.astype(BF16), (((0,), (0,)), ((), ())),
                           preferred_element_type=F32)


def _mm_hi(a, b):
    return jnp.dot(a, b, precision=HI, preferred_element_type=F32)


def _mm_nt_hi(a, b):
    return lax.dot_general(a, b, (((1,), (1,)), ((), ())), precision=HI, preferred_element_type=F32)


def _mm_tn_hi(a, b):
    return lax.dot_general(a, b, (((0,), (0,)), ((), ())), precision=HI, preferred_element_type=F32)


def _sigmoid(x):
    return jax.nn.sigmoid(x)


def _silu(x):
    return x * jax.nn.sigmoid(x)


def _full(shape):
    nd = len(shape)
    return pl.BlockSpec(shape, lambda *_: (0,) * nd)


def _ffn_kernel(x_ref, g_ref, wg_ref, wu_ref, wd_ref, o_ref, n_ref):
    f = pl.program_id(1)

    @pl.when(f == 0)
    def _():
        x = x_ref[...]
        ms = jnp.mean(x * x, axis=-1, keepdims=True)
        n_ref[...] = (x * lax.rsqrt(ms + NORM_EPS) * g_ref[...]).astype(BF16)
        o_ref[...] = x

    n = n_ref[...]
    a = jnp.dot(n, wg_ref[...], preferred_element_type=F32)
    b = jnp.dot(n, wu_ref[...], preferred_element_type=F32)
    hid = (0.5 * _silu(a) * b).astype(BF16)
    o_ref[...] += jnp.dot(hid, wd_ref[...], preferred_element_type=F32)


def _ffn(x, g, wg, wu, wd, tm=512, tf=512):
    m = x.shape[0]
    tm = min(tm, m)
    return pl.pallas_call(
        _ffn_kernel,
        grid=(m // tm, D_FF // tf),
        in_specs=[pl.BlockSpec((tm, D_MODEL), lambda i, f: (i, 0)),
                  pl.BlockSpec((1, D_MODEL), lambda i, f: (0, 0)),
                  pl.BlockSpec((D_MODEL, tf), lambda i, f: (0, f)),
                  pl.BlockSpec((D_MODEL, tf), lambda i, f: (0, f)),
                  pl.BlockSpec((tf, D_MODEL), lambda i, f: (f, 0))],
        out_specs=pl.BlockSpec((tm, D_MODEL), lambda i, f: (i, 0)),
        out_shape=jax.ShapeDtypeStruct((m, D_MODEL), F32),
        scratch_shapes=[pltpu.VMEM((tm, D_MODEL), BF16)],
        compiler_params=_cparams(("parallel", "arbitrary")),
        name="ffn",
    )(x, g, wg, wu, wd)


def _inproj_kernel(x_ref, g_ref, w_ref, o_ref, n_ref):
    j = pl.program_id(1)

    @pl.when(j == 0)
    def _():
        x = x_ref[...]
        ms = jnp.mean(x * x, axis=-1, keepdims=True)
        n_ref[...] = (x * lax.rsqrt(ms + NORM_EPS) * g_ref[...]).astype(BF16)

    o_ref[...] = jnp.dot(n_ref[...], w_ref[...], preferred_element_type=F32)


def _inproj(x, g, w, tm=512, tn=1408):
    m = x.shape[0]
    tm = min(tm, m)
    return pl.pallas_call(
        _inproj_kernel,
        grid=(m // tm, P_W // tn),
        in_specs=[pl.BlockSpec((tm, D_MODEL), lambda i, j: (i, 0)),
                  pl.BlockSpec((1, D_MODEL), lambda i, j: (0, 0)),
                  pl.BlockSpec((D_MODEL, tn), lambda i, j: (0, j))],
        out_specs=pl.BlockSpec((tm, tn), lambda i, j: (i, j)),
        out_shape=jax.ShapeDtypeStruct((m, P_W), F32),
        scratch_shapes=[pltpu.VMEM((tm, D_MODEL), BF16)],
        compiler_params=_cparams(("parallel", "arbitrary")),
        name="inproj",
    )(x, g, w)


def _merge_kernel(h_ref, ya_ref, yb_ref, yc_ref, ga_ref, gb_ref, gc_ref,
                  wa_ref, wb_ref, wc_ref, wo_ref, o_ref):
    j = pl.program_id(1)

    @pl.when(j == 0)
    def _():
        o_ref[...] = h_ref[...]

    merged = (_sigmoid(ga_ref[...]) * jnp.dot(ya_ref[...], wa_ref[...], preferred_element_type=F32)
              + _sigmoid(gb_ref[...]) * jnp.dot(yb_ref[...], wb_ref[...], preferred_element_type=F32)
              + _sigmoid(gc_ref[...]) * jnp.dot(yc_ref[...], wc_ref[...], preferred_element_type=F32))
    o_ref[...] += jnp.dot(merged.astype(BF16), wo_ref[...], preferred_element_type=F32)


def _merge(h, p, ya, yb, yc, wa, wb, wc, wo, tm=512, tk=512):
    m = h.shape[0]
    tm = min(tm, m)
    gb0 = C_GATE // tk
    nk = D_MODEL // tk
    yspec = pl.BlockSpec((tm, MIX_W), lambda i, j: (i, 0))
    wspec = pl.BlockSpec((MIX_W, tk), lambda i, j: (0, j))
    return pl.pallas_call(
        _merge_kernel,
        grid=(m // tm, nk),
        in_specs=[pl.BlockSpec((tm, D_MODEL), lambda i, j: (i, 0)), yspec, yspec, yspec,
                  pl.BlockSpec((tm, tk), lambda i, j: (i, gb0 + j)),
                  pl.BlockSpec((tm, tk), lambda i, j: (i, gb0 + nk + j)),
                  pl.BlockSpec((tm, tk), lambda i, j: (i, gb0 + 2 * nk + j)),
                  wspec, wspec, wspec,
                  pl.BlockSpec((tk, D_MODEL), lambda i, j: (j, 0))],
        out_specs=pl.BlockSpec((tm, D_MODEL), lambda i, j: (i, 0)),
        out_shape=jax.ShapeDtypeStruct((m, D_MODEL), F32),
        compiler_params=_cparams(("parallel", "arbitrary")),
        name="merge",
    )(h, ya, yb, yc, p, p, p, wa, wb, wc, wo)


def _final_norm_kernel(x_ref, g_ref, o_ref):
    x = x_ref[...]
    ms = jnp.mean(x * x, axis=-1, keepdims=True)
    o_ref[...] = x * lax.rsqrt(ms + NORM_EPS) * g_ref[...]


def _final_norm(x, g, tm=512):
    m = x.shape[0]
    tm = min(tm, m)
    return pl.pallas_call(
        _final_norm_kernel,
        grid=(m // tm,),
        in_specs=[pl.BlockSpec((tm, D_MODEL), lambda i: (i, 0)), _full((1, D_MODEL))],
        out_specs=pl.BlockSpec((tm, D_MODEL), lambda i: (i, 0)),
        out_shape=jax.ShapeDtypeStruct((m, D_MODEL), F32),
        compiler_params=_cparams(("parallel",)),
        name="final_norm",
    )(x, g)


def _kidx_ln_kernel(m_ref, g_ref, b_ref, o_ref):
    x = m_ref[:, 0:IDX_DIM]
    mu = jnp.mean(x, axis=-1, keepdims=True)
    var = jnp.mean(jnp.square(x - mu), axis=-1, keepdims=True)
    o_ref[...] = (x - mu) * lax.rsqrt(var + NORM_EPS) * g_ref[...] + b_ref[...]


def _kidx_ln(p, g, b, tm=512):
    m = p.shape[0]
    tm = min(tm, m)
    return pl.pallas_call(
        _kidx_ln_kernel,
        grid=(m // tm,),
        in_specs=[pl.BlockSpec((tm, LANE), lambda i: (i, C_MISC // LANE)),
                  _full((1, IDX_DIM)), _full((1, IDX_DIM))],
        out_specs=pl.BlockSpec((tm, IDX_DIM), lambda i: (i, 0)),
        out_shape=jax.ShapeDtypeStruct((m, IDX_DIM), F32),
        compiler_params=_cparams(("parallel",)),
        name="kidx_ln",
    )(p, g, b)


def _shifted(x, k, prev, rows):
    y = pltpu.roll(x, k, 0)
    for r in range(k):
        y = jnp.where(rows == r, prev[3 - k + r:4 - k + r, :], y)
    return y


def _ssd_kernel(*refs, T, n_valid, n_alias):
    (z_ref, xs_ref, bc_ref, misc_ref, cst_ref, st0_ref, cw_ref, cb_ref, dtb_ref,
     alog_ref, dexp_ref, norm_ref, e16_ref) = refs[:13]
    y_ref, stout_ref, prev_ref, st_ref, yd_ref = refs[13 + n_alias:]
    c = pl.program_id(1)
    nc = pl.num_programs(1)

    @pl.when(c == 0)
    def _():
        prev_ref[...] = cst_ref[0, 0]
        st_ref[...] = st0_ref[0, 0]

    rows = lax.broadcasted_iota(I32, (T, 1), 0)
    cw = cw_ref[...]
    prev = prev_ref[...]

    def conv(x, lo, hi):
        pv = prev[:, lo:hi]
        y = cb_ref[:, lo:hi] + x * cw[3:4, lo:hi]
        for k in (1, 2, 3):
            y = y + _shifted(x, k, pv, rows) * cw[3 - k:4 - k, lo:hi]
        return _silu(y)

    xs_raw = xs_ref[...]
    bc_raw = bc_ref[...]
    xs = conv(xs_raw, 0, M_INNER)
    bc = conv(bc_raw, M_INNER, CONV_DIM)
    prev_ref[:, 0:M_INNER] = xs_raw[T - 3:T, :]
    prev_ref[:, M_INNER:CONV_DIM] = bc_raw[T - 3:T, :]

    dt = jax.nn.softplus(misc_ref[:, MISC_DT:MISC_DT + M_HEADS] + dtb_ref[...])
    if n_valid < T:
        dt = jnp.where(rows < n_valid, dt, 0.0)
    ad = dt * (-jnp.exp(alog_ref[...]))
    ri = lax.broadcasted_iota(I32, (T, T), 0)
    ci = lax.broadcasted_iota(I32, (T, T), 1)
    lower = ri >= ci
    tril = lower.astype(F32)
    triu = (ri <= ci).astype(F32)
    eye16 = (lax.broadcasted_iota(I32, (M_HEADS, M_HEADS), 0)
             == lax.broadcasted_iota(I32, (M_HEADS, M_HEADS), 1)).astype(F32)
    cs = _mm_hi(tril, ad)
    cst = _mm_hi(_mm_nt_hi(eye16, ad), triu)
    e16 = e16_ref[...]
    dt_e = _mm_hi(dt, e16)
    ecs_e = _mm_hi(jnp.exp(cs), e16)
    wl_e = _mm_hi(jnp.exp(cs[T - 1:T, :] - cs), e16)
    xd = xs * dt_e
    xdw = (xd * wl_e).astype(BF16)
    xd = xd.astype(BF16)
    bcb = bc.astype(BF16)
    for g in range(M_GROUPS):
        bg = bcb[:, g * M_STATE:(g + 1) * M_STATE]
        cg = bcb[:, (M_GROUPS + g) * M_STATE:(M_GROUPS + g + 1) * M_STATE]
        cbm = _mm_nt(cg, bg)
        for hh in range(M_HEADS // M_GROUPS):
            h = g * (M_HEADS // M_GROUPS) + hh
            hs = slice(h * M_HEAD_DIM, (h + 1) * M_HEAD_DIM)
            diff = cs[:, h:h + 1] - cst[h:h + 1, :]
            lm = jnp.exp(jnp.where(lower, diff, -jnp.inf))
            yd = _mm(cbm * lm, xd[:, hs])
            st = st_ref[h]
            yo = _mm_nt(cg, st)
            yd_ref[:, hs] = yd + yo * ecs_e[:, hs]
            upd = _mm_tn(xdw[:, hs], bg)
            dec = jnp.exp(cst[h:h + 1, T - 1:T])
            st_ref[h] = st * dec + upd

    y = yd_ref[...] + xs * dexp_ref[...]
    y = y * _silu(z_ref[...])
    gw = M_INNER // M_GROUPS
    for g in range(M_GROUPS):
        yg = y[:, g * gw:(g + 1) * gw]
        ms = jnp.mean(yg * yg, axis=-1, keepdims=True)
        y_ref[:, g * gw:(g + 1) * gw] = (yg * lax.rsqrt(ms + NORM_EPS)
                                         * norm_ref[:, g * gw:(g + 1) * gw]).astype(y_ref.dtype)

    @pl.when(c == nc - 1)
    def _():
        stout_ref[0, 0] = st_ref[...]


def _alias_inputs(n_in, bufs):
    specs, args, aliases = [], [], {}
    for k, b in enumerate(bufs):
        if b is not None:
            aliases[n_in + len(args)] = k
            specs.append(pl.BlockSpec(memory_space=pl.ANY))
            args.append(b)
    return specs, args, aliases


def _ssd(p, row0, n_seq, n_chunks, T, n_valid, layer, conv_state, ssm0, cw, cb, dtb, alog, dexp, norm,
         e16, *, n_rows, depth, out_layer, y_buf=None, st_buf=None):
    rb0 = row0 // T

    def rmap(col):
        return lambda s, c: (rb0 + s * n_chunks + c, col)

    args = [p, p, p, p, conv_state, ssm0, cw, cb, dtb, alog, dexp, norm, e16]
    a_specs, a_args, aliases = _alias_inputs(len(args), (y_buf, st_buf))
    kern = functools.partial(_ssd_kernel, T=T, n_valid=n_valid, n_alias=len(a_args))
    return pl.pallas_call(
        kern,
        grid=(n_seq, n_chunks),
        in_specs=[pl.BlockSpec((T, 1024), rmap(C_Z // 1024)),
                  pl.BlockSpec((T, 1024), rmap(C_XS // 1024)),
                  pl.BlockSpec((T, 1024), rmap(C_BC // 1024)),
                  pl.BlockSpec((T, LANE), rmap(C_MISC // LANE)),
                  pl.BlockSpec((1, 1, CONV_W - 1, CONV_DIM), lambda s, c: (layer, s, 0, 0)),
                  pl.BlockSpec((1, 1, M_HEADS, M_HEAD_DIM, M_STATE), lambda s, c: (layer, s, 0, 0, 0)),
                  _full((CONV_W, CONV_DIM)), _full((1, CONV_DIM)), _full((1, M_HEADS)),
                  _full((1, M_HEADS)), _full((1, M_INNER)), _full((1, M_INNER)),
                  _full((M_HEADS, M_INNER))] + a_specs,
        out_specs=[pl.BlockSpec((T, M_INNER), rmap(0)),
                   pl.BlockSpec((1, 1, M_HEADS, M_HEAD_DIM, M_STATE), lambda s, c: (out_layer, s, 0, 0, 0))],
        out_shape=[jax.ShapeDtypeStruct((n_rows, M_INNER), BF16),
                   jax.ShapeDtypeStruct((depth, n_seq, M_HEADS, M_HEAD_DIM, M_STATE), F32)],
        scratch_shapes=[pltpu.VMEM((CONV_W - 1, CONV_DIM), F32),
                        pltpu.VMEM((M_HEADS, M_HEAD_DIM, M_STATE), F32),
                        pltpu.VMEM((T, M_INNER), F32)],
        input_output_aliases=aliases,
        compiler_params=_cparams(("parallel", "arbitrary")),
        name="ssd",
    )(*args, *a_args)


def _rwkv_kernel(*refs, T, n_valid, HG, n_alias):
    (r_ref, k_ref, v_ref, lo_ref, sh0_ref, s0_ref, mu_ref, w0_ref, w2_ref, a0_ref,
     a2_ref, g2_ref, kkw_ref, kaw_ref, rk_ref, gng_ref, gnb_ref) = refs[:17]
    (y_ref, sout_ref, last_ref, s_ref, yb_ref, xs_ref, hr_ref,
     gm_ref) = refs[17 + n_alias:]
    c = pl.program_id(1)
    nc = pl.num_programs(1)

    @pl.when(c == 0)
    def _():
        last_ref[...] = sh0_ref[0, 0]
        s_ref[...] = s0_ref[0, 0]

    rows = lax.broadcasted_iota(I32, (T, 1), 0)

    def mix(ref, lo, hi):
        x = ref[...]
        prev = jnp.where(rows == 0, last_ref[:, lo:hi], pltpu.roll(x, 1, 0))
        last_ref[:, lo:hi] = x[T - 1:T, :]
        return x + (prev - x) * mu_ref[:, lo:hi]

    r = mix(r_ref, 0, 1024)
    k = mix(k_ref, 1024, 2048)
    v = mix(v_ref, 2048, 3072)
    lo = mix(lo_ref, 3072, 3072 + LORA_W)
    xw = lo[:, 0:R_DECAY_LORA]
    xa = lo[:, R_DECAY_LORA:R_DECAY_LORA + R_A_LORA]
    xg = lo[:, R_DECAY_LORA + R_A_LORA:LORA_W]
    wl = -jax.nn.softplus(-(w0_ref[...] + _mm_hi(jnp.tanh(xw), w2_ref[...]))) - 0.5
    ld = -jnp.exp(wl)
    a = _sigmoid(a0_ref[...] + _mm_hi(xa, a2_ref[...]))
    gate = _mm_hi(_sigmoid(xg), g2_ref[...])
    kk = k * kkw_ref[...]
    km = k * (1.0 + (a - 1.0) * kaw_ref[...])
    if n_valid < T:
        ok = rows < n_valid
        ld = jnp.where(ok, ld, 0.0)
        kk = jnp.where(ok, kk, 0.0)
        km = jnp.where(ok, km, 0.0)
        v = jnp.where(ok, v, 0.0)

    ri = lax.broadcasted_iota(I32, (T, T), 0)
    ci = lax.broadcasted_iota(I32, (T, T), 1)
    cum = _mm_hi((ri >= ci).astype(F32), ld)
    eg = jnp.exp(cum)
    egm = jnp.exp(cum - ld)
    ei = jnp.exp(-cum)
    el = jnp.exp(cum[T - 1:T, :] - cum)
    r_eg = r * eg
    k_ei = km * ei
    k_el = km * el
    rk_sum = r * km * rk_ref[...]
    gng = gng_ref[...]
    gnb = gnb_ref[...]

    R = HG * T
    GW = HG * R_HEAD
    rr = lax.broadcasted_iota(I32, (R, R), 0)
    cc = lax.broadcasted_iota(I32, (R, R), 1)
    same = (rr // T) == (cc // T)
    strict = jnp.logical_and(same, rr > cc)
    incl = jnp.logical_and(same, rr >= cc)
    head_of = (lax.broadcasted_iota(I32, (R, GW), 0) // T) == (lax.broadcasted_iota(I32, (R, GW), 1) // R_HEAD)
    blk_r = (rr % T) // SUBLANES
    blk_c = (cc % T) // SUBLANES

    n_groups = R_HEADS // HG

    def head_slices(gi):
        return [slice((gi * HG + hh) * R_HEAD, (gi * HG + hh + 1) * R_HEAD) for hh in range(HG)]

    def stack(x, gi):
        return jnp.concatenate([x[:, s_] for s_ in head_slices(gi)], axis=0)

    def unit_kk(gi):
        kks = stack(kk, gi)
        nrm = jnp.sqrt(jnp.sum(kks * kks, axis=-1, keepdims=True))
        return kks / jnp.maximum(nrm, 1e-12)

    for gi in range(n_groups):
        kks = unit_kk(gi)
        q2 = jnp.concatenate([kks * stack(egm, gi), stack(r_eg, gi)], axis=0)
        k2 = jnp.concatenate([stack(k_ei, gi), kks * stack(a, gi) * stack(ei, gi)], axis=0)
        gm = _mm_nt(q2, k2)
        gm_ref[gi] = gm
        scat = s_ref[gi * HG:(gi + 1) * HG].reshape(GW, R_HEAD)
        hmw = _mm_nt(q2, scat)
        hk = jnp.concatenate([hmw[hh * T:(hh + 1) * T, hh * R_HEAD:(hh + 1) * R_HEAD]
                              for hh in range(HG)], axis=0)
        hr_ref[gi] = jnp.concatenate([hmw[R + hh * T:R + (hh + 1) * T, hh * R_HEAD:(hh + 1) * R_HEAD]
                                      for hh in range(HG)], axis=0)
        xs_ref[gi] = hk + _mm(jnp.where(strict, gm[0:R, 0:R], 0.0), stack(v, gi))

    for jb in range(T // SUBLANES):
        for gi in range(n_groups):
            for hh in range(HG):
                r0 = hh * T + jb * SUBLANES
                xb = xs_ref[gi, r0:r0 + SUBLANES, :]
                lrow = gm_ref[gi, r0:r0 + SUBLANES, R:2 * R]
                for s in range(1, SUBLANES):
                    below = lax.broadcasted_iota(I32, (SUBLANES, 1), 0) >= s
                    xb = xb - jnp.where(below, lrow[:, r0 + s - 1:r0 + s], 0.0) * xb[s - 1:s, :]
                xs_ref[gi, r0:r0 + SUBLANES, :] = xb
            if jb + 1 < T // SUBLANES:
                later = jnp.where(jnp.logical_and(jnp.logical_and(same, blk_c == jb), blk_r > jb),
                                  gm_ref[gi, 0:R, R:2 * R], 0.0)
                xs_ref[gi] = xs_ref[gi] - _mm(later, xs_ref[gi])

    for gi in range(n_groups):
        hsl = head_slices(gi)
        kks = unit_kk(gi)
        bs = kks * stack(a, gi)
        vs = stack(v, gi)
        u = -xs_ref[gi]
        ak = jnp.where(incl, gm_ref[gi, R:2 * R, 0:R], 0.0)
        ab = jnp.where(incl, gm_ref[gi, R:2 * R, R:2 * R], 0.0)
        scat = s_ref[gi * HG:(gi + 1) * HG].reshape(GW, R_HEAD)
        y = hr_ref[gi] + _mm(jnp.concatenate([ak, ab], axis=1), jnp.concatenate([vs, u], axis=0))
        vw = jnp.where(head_of, jnp.concatenate([v[:, gi * GW:(gi + 1) * GW]] * HG, axis=0), 0.0)
        uw = jnp.where(head_of, jnp.concatenate([u] * HG, axis=1), 0.0)
        upd = _mm_tn(jnp.concatenate([vw, uw], axis=0),
                     jnp.concatenate([stack(k_el, gi), bs * stack(el, gi)], axis=0))
        g_last = jnp.concatenate([jnp.broadcast_to(eg[T - 1:T, s_], (R_HEAD, R_HEAD)) for s_ in hsl],
                                 axis=0)
        s_ref[gi * HG:(gi + 1) * HG] = (scat * g_last + upd).reshape(HG, R_HEAD, R_HEAD)
        mu = jnp.mean(y, axis=-1, keepdims=True)
        var = jnp.mean(jnp.square(y - mu), axis=-1, keepdims=True)
        gs = jnp.concatenate([jnp.broadcast_to(gng[:, s_], (T, R_HEAD)) for s_ in hsl], axis=0)
        gb = jnp.concatenate([jnp.broadcast_to(gnb[:, s_], (T, R_HEAD)) for s_ in hsl], axis=0)
        yn = (y - mu) * lax.rsqrt(var + GN_EPS) * gs + gb
        yn = yn + jnp.sum(stack(rk_sum, gi), axis=-1, keepdims=True) * vs
        for hh in range(HG):
            yb_ref[:, hsl[hh]] = yn[hh * T:(hh + 1) * T]

    y_ref[...] = (yb_ref[...] * gate).astype(y_ref.dtype)

    @pl.when(c == nc - 1)
    def _():
        sout_ref[0, 0] = s_ref[...]


def _rwkv(p, row0, n_seq, n_chunks, T, n_valid, layer, shift0, s0, mu, w0, w2, a0, a2, g2, kkw, kaw,
          rk, gng, gnb, *, n_rows, depth, out_layer, y_buf=None, st_buf=None):
    rb0 = row0 // T

    def rmap(col):
        return lambda s, c: (rb0 + s * n_chunks + c, col)

    sw = 3072 + LORA_W
    hg = min(R_HEADS, max(1, MXU_DIM // T))
    args = [p, p, p, p, shift0, s0, mu, w0, w2, a0, a2, g2, kkw, kaw, rk, gng, gnb]
    a_specs, a_args, aliases = _alias_inputs(len(args), (y_buf, st_buf))
    kern = functools.partial(_rwkv_kernel, T=T, n_valid=n_valid, HG=hg, n_alias=len(a_args))
    return pl.pallas_call(
        kern,
        grid=(n_seq, n_chunks),
        in_specs=[pl.BlockSpec((T, 1024), rmap(C_RR // 1024)),
                  pl.BlockSpec((T, 1024), rmap(C_RK // 1024)),
                  pl.BlockSpec((T, 1024), rmap(C_RV // 1024)),
                  pl.BlockSpec((T, LORA_W), rmap(C_LORA // LORA_W)),
                  pl.BlockSpec((1, 1, 1, sw), lambda s, c: (layer, s, 0, 0)),
                  pl.BlockSpec((1, 1, R_HEADS, R_HEAD, R_HEAD), lambda s, c: (layer, s, 0, 0, 0)),
                  _full((1, sw)), _full((1, R_WIDTH)), _full((R_DECAY_LORA, R_WIDTH)),
                  _full((1, R_WIDTH)), _full((R_A_LORA, R_WIDTH)),
                  _full((LORA_W - R_DECAY_LORA - R_A_LORA, R_WIDTH)),
                  _full((1, R_WIDTH)), _full((1, R_WIDTH)), _full((1, R_WIDTH)),
                  _full((1, R_WIDTH)), _full((1, R_WIDTH))] + a_specs,
        out_specs=[pl.BlockSpec((T, R_WIDTH), rmap(0)),
                   pl.BlockSpec((1, 1, R_HEADS, R_HEAD, R_HEAD), lambda s, c: (out_layer, s, 0, 0, 0))],
        out_shape=[jax.ShapeDtypeStruct((n_rows, R_WIDTH), BF16),
                   jax.ShapeDtypeStruct((depth, n_seq, R_HEADS, R_HEAD, R_HEAD), F32)],
        scratch_shapes=[pltpu.VMEM((1, sw), F32),
                        pltpu.VMEM((R_HEADS, R_HEAD, R_HEAD), F32),
                        pltpu.VMEM((T, R_WIDTH), F32),
                        pltpu.VMEM((R_HEADS // hg, hg * T, R_HEAD), F32),
                        pltpu.VMEM((R_HEADS // hg, hg * T, R_HEAD), F32),
                        pltpu.VMEM((R_HEADS // hg, 2 * hg * T, 2 * hg * T), F32)],
        input_output_aliases=aliases,
        compiler_params=_cparams(("parallel", "arbitrary")),
        name="rwkv",
    )(*args, *a_args)


def _score_keys(acc, causal):
    bits = pltpu.bitcast(acc, I32)
    key = bits ^ ((bits >> 31) & 0x7FFFFFFF)
    return jnp.where(causal, key, INT_MIN)


def _kth_threshold(count_ge, shape, top, bits_per_step=1):
    tb = jnp.zeros(shape, I32)
    for shift in range(32 - bits_per_step, -1, -bits_per_step):
        digit = jnp.zeros(shape, I32)
        for d in range(1, 1 << bits_per_step):
            inc = d << shift
            inc = inc - (1 << 32) if inc >= (1 << 31) else inc
            cnt = count_ge((tb | jnp.int32(inc)) ^ jnp.int32(INT_MIN))
            digit = digit + jnp.where(cnt >= top, 1, 0)
        tb = tb | (digit << shift)
    return jnp.maximum(tb ^ jnp.int32(INT_MIN), INT_MIN + 1)


def _toeplitz_bias(base_row, nrows):
    return pltpu.roll(jnp.broadcast_to(base_row, (nrows, 2 * LANE)), 0, 1, stride=1, stride_axis=0)


def _dsa_prompt_kernel(q_ref, qi_ref, misc_ref, kit_ref, kt_ref, va_ref, base_ref, o_ref,
                       key_ref, mx_ref, acc_ref, *, top, TA):
    i = pl.program_id(0)
    QB = LANE
    q0 = i * QB
    n_a = (q0 + QB + TA - 1) // TA
    wi = misc_ref[:, MISC_WI:MISC_WI + IDX_HEADS] * (IDX_HEADS ** -0.5 * IDX_DIM ** -0.5)
    qi = qi_ref[...]
    qis = jnp.concatenate([qi[:, h * IDX_DIM:(h + 1) * IDX_DIM] for h in range(IDX_HEADS)],
                          axis=0).astype(BF16)
    rowq = q0 + lax.broadcasted_iota(I32, (QB, 1), 0)

    def score_tile(kt, carry):
        off = pl.multiple_of(kt * TA, TA)
        s = jnp.maximum(jnp.dot(qis, kit_ref[:, pl.ds(off, TA)], preferred_element_type=F32), 0.0)
        acc = s[0:QB] * wi[:, 0:1]
        for h in range(1, IDX_HEADS):
            acc = acc + s[h * QB:(h + 1) * QB] * wi[:, h:h + 1]
        col = off + lax.broadcasted_iota(I32, (1, TA), 1)
        key_ref[:, pl.ds(off, TA)] = _score_keys(acc, col <= rowq)
        return carry

    lax.fori_loop(0, n_a, score_tile, 0)

    def count_ge(cand):
        def body(kt, cnt):
            off = pl.multiple_of(kt * TA, TA)
            ge = jnp.where(key_ref[:, pl.ds(off, TA)] >= cand, 1.0, 0.0)
            for j in range(TA // LANE):
                cnt = cnt + ge[:, j * LANE:(j + 1) * LANE]
            return cnt
        cnt = lax.fori_loop(0, n_a, body, jnp.zeros((QB, LANE), F32))
        return jnp.sum(cnt, axis=-1, keepdims=True)

    thr = _kth_threshold(count_ge, (QB, 1), float(top))

    n_ge = count_ge(thr)

    @pl.when(jnp.max(n_ge) > top)
    def _():
        def count_tied_before(pos):
            def body(kt, cnt):
                off = pl.multiple_of(kt * TA, TA)
                col = off + lax.broadcasted_iota(I32, (1, TA), 1)
                hit = jnp.where(key_ref[:, pl.ds(off, TA)] == thr, jnp.where(col < pos, 1.0, 0.0), 0.0)
                for j in range(TA // LANE):
                    cnt = cnt + hit[:, j * LANE:(j + 1) * LANE]
                return cnt
            cnt = lax.fori_loop(0, n_a, body, jnp.zeros((QB, LANE), F32))
            return jnp.sum(cnt, axis=-1, keepdims=True)

        need = top - count_ge(thr + 1)
        last = jnp.zeros((QB, 1), I32)
        for bit in range(max(1, (kit_ref.shape[1] - 1).bit_length()) - 1, -1, -1):
            cand = last | jnp.int32(1 << bit)
            last = jnp.where(count_tied_before(cand) < need, cand, last)

        def drop(kt, carry):
            off = pl.multiple_of(kt * TA, TA)
            col = off + lax.broadcasted_iota(I32, (1, TA), 1)
            k = key_ref[:, pl.ds(off, TA)]
            key_ref[:, pl.ds(off, TA)] = jnp.where(k == thr, jnp.where(col > last, INT_MIN, k), k)
            return carry

        lax.fori_loop(0, n_a, drop, 0)

    q = q_ref[...]
    lim = (i - 1) * QB
    n_far = (jnp.maximum(lim, 0) + TA - 1) // TA
    qgs, nears = [], []
    for g in range(KV_HEADS):
        heads = range(g * GQA, (g + 1) * GQA)
        qgs.append((jnp.concatenate([q[:, h * HEAD_DIM:(h + 1) * HEAD_DIM] for h in heads], axis=0)
                    * HEAD_DIM ** -0.5).astype(BF16))
        nears.append(jnp.concatenate(
            [_toeplitz_bias(base_ref[h:h + 1, :], QB) - base_ref[h:h + 1, 2 * LANE - 1:2 * LANE]
             for h in heads], axis=0))

    def far_logits(g, off, msk):
        s = jnp.dot(qgs[g], kt_ref[g, :, pl.ds(off, TA)], preferred_element_type=F32)
        return (s.reshape(GQA, QB, TA) + msk[None]).reshape(GQA * QB, TA)

    def far_mask(off):
        col = off + lax.broadcasted_iota(I32, (1, TA), 1)
        return jnp.where(key_ref[:, pl.ds(off, TA)] >= thr,
                         jnp.where(col < lim, 0.0, NEG_BIG), NEG_BIG)

    def near_logits(g, off, lo, msk):
        s = (jnp.dot(qgs[g], kt_ref[g, :, pl.ds(off, LANE)], preferred_element_type=F32)
             + nears[g][:, lo:lo + LANE])
        return (s.reshape(GQA, QB, LANE) + msk[None]).reshape(GQA * QB, LANE)

    def near_mask(off):
        return jnp.where(key_ref[:, pl.ds(off, LANE)] >= thr, 0.0, NEG_BIG)

    mx_ref[...] = jnp.full(mx_ref.shape, NEG_BIG, F32)
    acc_ref[...] = jnp.zeros(acc_ref.shape, F32)

    def accumulate(g, s, v_tile):
        width = s.shape[1]
        m = s[:, 0:LANE]
        for j in range(1, width // LANE):
            m = jnp.maximum(m, s[:, j * LANE:(j + 1) * LANE])
        m_old = mx_ref[g]
        m_new = jnp.maximum(m_old, jnp.broadcast_to(jnp.max(m, axis=-1, keepdims=True), m.shape))
        pr = jnp.exp(s - jnp.concatenate([m_new] * (width // LANE), axis=1))
        acc_ref[g] = (acc_ref[g] * jnp.exp(m_old - m_new)
                      + jnp.dot(pr.astype(BF16), v_tile, preferred_element_type=F32))
        mx_ref[g] = m_new

    def far_acc(kt, carry):
        off = pl.multiple_of(kt * TA, TA)
        msk = far_mask(off)
        for g in range(KV_HEADS):
            accumulate(g, far_logits(g, off, msk), va_ref[g, pl.ds(off, TA), :])
        return carry

    def near_acc(kt, lo):
        off = pl.multiple_of(kt * LANE, LANE)
        msk = near_mask(off)
        for g in range(KV_HEADS):
            accumulate(g, near_logits(g, off, lo, msk), va_ref[g, pl.ds(off, LANE), :])

    lax.fori_loop(0, n_far, far_acc, 0)

    @pl.when(i >= 1)
    def _():
        near_acc(i - 1, 0)

    near_acc(i, LANE)
    for g in range(KV_HEADS):
        acc = acc_ref[g]
        out = acc[:, 0:HEAD_DIM] / acc[:, HEAD_DIM:HEAD_DIM + 1]
        for hh in range(GQA):
            h = g * GQA + hh
            o_ref[:, h * HEAD_DIM:(h + 1) * HEAD_DIM] = out[hh * QB:(hh + 1) * QB].astype(o_ref.dtype)


def _dsa_prompt(p, kit, kt, va, base, S, n_rows):
    top = min(TOPK_MAX, S // 4)
    TA = min(512, S)
    kern = functools.partial(_dsa_prompt_kernel, top=top, TA=TA)
    return pl.pallas_call(
        kern,
        grid=(S // LANE,),
        in_specs=[pl.BlockSpec((LANE, 1024), lambda i: (i, C_Q // 1024)),
                  pl.BlockSpec((LANE, 1024), lambda i: (i, C_QI // 1024)),
                  pl.BlockSpec((LANE, LANE), lambda i: (i, C_MISC // LANE)),
                  _full((IDX_DIM, S)), _full((KV_HEADS, HEAD_DIM, S)), _full((KV_HEADS, S, LANE)),
                  _full((ATT_HEADS, 2 * LANE))],
        out_specs=pl.BlockSpec((LANE, MIX_W), lambda i: (i, 0)),
        out_shape=jax.ShapeDtypeStruct((n_rows, MIX_W), BF16),
        scratch_shapes=[pltpu.VMEM((LANE, S), I32),
                        pltpu.VMEM((KV_HEADS, GQA * LANE, LANE), F32),
                        pltpu.VMEM((KV_HEADS, GQA * LANE, LANE), F32)],
        compiler_params=_cparams(("arbitrary",)),
        name="dsa_prompt",
    )(p, p, p, kit, kt, va, base)


def _dsa_sample_kernel(pt_ref, q_ref, qi_ref, misc_ref, kn_ref, vn_ref, kin_ref, base_ref, *rest,
                       n_pages, n_valid, top):
    ki_pages = rest[0:n_pages]
    k_pages = rest[n_pages:2 * n_pages]
    v_pages = rest[2 * n_pages:3 * n_pages]
    o_ref = rest[3 * n_pages + 1]
    kib_ref, kb_ref, vb_ref, msk_ref = rest[3 * n_pages + 2:]
    TQ = SEQ_PAD
    lo = n_pages * LANE
    NK = lo + LANE
    wi = misc_ref[:, MISC_WI:MISC_WI + IDX_HEADS] * (IDX_HEADS ** -0.5 * IDX_DIM ** -0.5)
    qi = qi_ref[...]
    qis = jnp.concatenate([qi[:, h * IDX_DIM:(h + 1) * IDX_DIM] for h in range(IDX_HEADS)], axis=0)

    eye = (lax.broadcasted_iota(I32, (HEAD_DIM, HEAD_DIM), 0)
           == lax.broadcasted_iota(I32, (HEAD_DIM, HEAD_DIM), 1)).astype(BF16)

    def new_slot(x):
        xt = _mm_nt(eye, x)
        return jnp.concatenate([xt, jnp.zeros((HEAD_DIM, LANE - TQ), F32)], axis=1).astype(BF16)

    for t in range(n_pages):
        kib_ref[:, t * LANE:(t + 1) * LANE] = ki_pages[t][0, 0].astype(BF16)
        for g in range(KV_HEADS):
            kb_ref[g, :, t * LANE:(t + 1) * LANE] = k_pages[t][0, 0, g].astype(BF16)
            vb_ref[g, :, t * LANE:(t + 1) * LANE] = v_pages[t][0, 0, g].astype(BF16)
    kib_ref[:, lo:NK] = new_slot(kin_ref[...])
    kn = kn_ref[...]
    vn = vn_ref[...]
    for g in range(KV_HEADS):
        kb_ref[g, :, lo:NK] = new_slot(kn[:, g * HEAD_DIM:(g + 1) * HEAD_DIM])
        vb_ref[g, :, lo:NK] = new_slot(vn[:, g * HEAD_DIM:(g + 1) * HEAD_DIM])

    s = jnp.maximum(_mm(qis, kib_ref[...]), 0.0)
    acc = s[0:TQ] * wi[:, 0:1]
    for h in range(1, IDX_HEADS):
        acc = acc + s[h * TQ:(h + 1) * TQ] * wi[:, h:h + 1]
    trow = lax.broadcasted_iota(I32, (TQ, 1), 0)
    new = lax.broadcasted_iota(I32, (1, NK), 1) - lo
    keys = _score_keys(acc, new <= jnp.minimum(trow, n_valid - 1))

    def count_ge(cand):
        return jnp.sum(jnp.where(keys >= cand, 1.0, 0.0), axis=-1, keepdims=True)

    thr = _kth_threshold(count_ge, (TQ, 1), float(top), bits_per_step=4)
    msk_ref[...] = jnp.where(keys >= thr, 0.0, NEG_BIG)

    @pl.when(jnp.max(jnp.where(trow < n_valid, count_ge(thr), 0.0)) > top)
    def _():
        pos = new + lo
        tied = keys == thr

        def count_tied_before(p):
            return jnp.sum(jnp.where(tied, jnp.where(pos < p, 1.0, 0.0), 0.0), axis=-1, keepdims=True)

        need = top - count_ge(thr + 1)
        last = jnp.zeros((TQ, 1), I32)
        for bit in range(max(1, (NK - 1).bit_length()) - 1, -1, -1):
            cand = last | jnp.int32(1 << bit)
            last = jnp.where(count_tied_before(cand) < need, cand, last)
        msk_ref[...] = jnp.where(tied, jnp.where(pos > last, NEG_BIG, 0.0), msk_ref[...])

    msk = jnp.concatenate([msk_ref[...]] * GQA, axis=0)

    q = q_ref[...]
    for g in range(KV_HEADS):
        heads = range(g * GQA, (g + 1) * GQA)
        gs = slice(g * HEAD_DIM, (g + 1) * HEAD_DIM)
        qg = (jnp.concatenate([q[:, h * HEAD_DIM:(h + 1) * HEAD_DIM] for h in heads], axis=0)
              * HEAD_DIM ** -0.5)
        near = jnp.concatenate(
            [_toeplitz_bias(base_ref[h:h + 1, :], TQ) - base_ref[h:h + 1, 2 * LANE - 1:2 * LANE]
             for h in heads], axis=0)
        bias = jnp.concatenate([jnp.zeros((GQA * TQ, lo - LANE), F32), near], axis=1)
        s = _mm(qg, kb_ref[g]) + bias + msk
        mx = jnp.max(s, axis=-1, keepdims=True)
        pr = jnp.exp(s - mx)
        pr = pr / jnp.sum(pr, axis=-1, keepdims=True)
        out = _mm_nt(pr, vb_ref[g])
        for hh, h in enumerate(heads):
            o_ref[:, h * HEAD_DIM:(h + 1) * HEAD_DIM] = out[hh * TQ:(hh + 1) * TQ].astype(o_ref.dtype)


def _dsa_sample(p, row0, n_seq, kidx_ln, base, layer, cache_k, cache_v, cache_kidx, page_table, n_valid,
                y_buf):
    n_pages = page_table.shape[1]
    past = n_pages * PAGE_SIZE
    top = min(TOPK_MAX, (past + n_valid) // 4)
    rb0 = row0 // SEQ_PAD
    ck = jnp.transpose(cache_k, (0, 1, 3, 4, 2))
    cv = jnp.transpose(cache_v, (0, 1, 3, 4, 2))
    cki = jnp.transpose(cache_kidx, (0, 1, 3, 2))

    def rmap(col):
        return lambda b, pt: (rb0 + b, col)

    def pmap(pg):
        return lambda b, pt: (layer, pt[b, pg], 0, 0)

    def pmap5(pg):
        return lambda b, pt: (layer, pt[b, pg], 0, 0, 0)

    in_specs = [pl.BlockSpec((SEQ_PAD, 1024), rmap(C_Q // 1024)),
                pl.BlockSpec((SEQ_PAD, 1024), rmap(C_QI // 1024)),
                pl.BlockSpec((SEQ_PAD, LANE), rmap(C_MISC // LANE)),
                pl.BlockSpec((SEQ_PAD, 256), rmap(C_K // 256)),
                pl.BlockSpec((SEQ_PAD, 256), rmap(C_V // 256)),
                pl.BlockSpec((SEQ_PAD, IDX_DIM), lambda b, pt: (rb0 + b, 0)),
                pl.BlockSpec((ATT_HEADS, 2 * LANE), lambda b, pt: (0, 0))]
    in_specs += [pl.BlockSpec((1, 1, IDX_DIM, PAGE_SIZE), pmap(pg)) for pg in range(n_pages)]
    in_specs += [pl.BlockSpec((1, 1, KV_HEADS, HEAD_DIM, PAGE_SIZE), pmap5(pg)) for pg in range(n_pages)]
    in_specs += [pl.BlockSpec((1, 1, KV_HEADS, HEAD_DIM, PAGE_SIZE), pmap5(pg)) for pg in range(n_pages)]
    kern = functools.partial(_dsa_sample_kernel, n_pages=n_pages, n_valid=n_valid, top=top)
    nk = (n_pages + 1) * LANE
    return pl.pallas_call(
        kern,
        grid_spec=pltpu.PrefetchScalarGridSpec(
            num_scalar_prefetch=1,
            grid=(n_seq,),
            in_specs=in_specs + [pl.BlockSpec(memory_space=pl.ANY)],
            out_specs=pl.BlockSpec((SEQ_PAD, MIX_W), lambda b, pt: (rb0 + b, 0)),
            scratch_shapes=[pltpu.VMEM((IDX_DIM, nk), BF16),
                            pltpu.VMEM((KV_HEADS, HEAD_DIM, nk), BF16),
                            pltpu.VMEM((KV_HEADS, HEAD_DIM, nk), BF16),
                            pltpu.VMEM((SEQ_PAD, nk), F32)]),
        out_shape=jax.ShapeDtypeStruct(y_buf.shape, BF16),
        input_output_aliases={1 + len(in_specs): 0},
        compiler_params=_cparams(("arbitrary",)),
        name="dsa_sample",
    )(page_table, p, p, p, p, p, kidx_ln, base,
      *([cki] * n_pages), *([ck] * n_pages), *([cv] * n_pages), y_buf)


def _t5_base(t5_table):
    x = jnp.arange(2 * LANE, dtype=jnp.int32)
    rel = jnp.where(x <= LANE, LANE - x, T5_MAX_DIST)
    n = jnp.maximum(rel, 0)
    exact = T5_BUCKETS // 2
    nf = jnp.maximum(n, exact).astype(F32)
    large = exact + (jnp.log(nf / exact) / math.log(T5_MAX_DIST / exact)
                     * (T5_BUCKETS - exact)).astype(jnp.int32)
    bucket = jnp.where(n < exact, n, jnp.minimum(large, T5_BUCKETS - 1))
    return t5_table[bucket].astype(F32).T


def _relayout_w_in(w):
    widths = (M_INNER, CONV_DIM, M_HEADS, ATT_HEADS * HEAD_DIM, KV_HEADS * HEAD_DIM,
              KV_HEADS * HEAD_DIM, IDX_HEADS * IDX_DIM, IDX_DIM, IDX_HEADS, R_PROJ, 3 * D_MODEL)
    offs = [0]
    for wd in widths:
        offs.append(offs[-1] + wd)
    z, xbc, dt, q, k, v, qi, ki, wi, pr, gates = (w[:, offs[i]:offs[i + 1]] for i in range(len(widths)))
    zeros = lambda n: jnp.zeros((w.shape[0], n), w.dtype)
    lora = pr[:, 3 * R_WIDTH:]
    cols = [z, xbc, q, qi, pr[:, :3 * R_WIDTH], gates,
            lora, zeros(LORA_W - lora.shape[1]), k, v,
            ki, dt, wi, zeros(LANE - IDX_DIM - M_HEADS - IDX_HEADS)]
    out = jnp.concatenate(cols, axis=1).astype(BF16)
    assert out.shape[1] == P_W
    return out


def _pad_cols(x, n):
    return jnp.concatenate([x, jnp.zeros(x.shape[:-1] + (n - x.shape[-1],), x.dtype)], axis=-1)


def _shift_layout(x):
    return _pad_cols(x, 3 * R_WIDTH + LORA_W)


def kernel(x_prompt, x_sample, cache_k, cache_v, cache_kidx, page_table, state_ssm, state_conv, state_wkv, state_shift, ffn1_norm, ffn1_gate, ffn1_up, ffn1_down, mix_norm, w_in, conv_w, conv_b, dt_bias, a_log, d_skip, ssm_norm, kidx_ln_g, kidx_ln_b, t5_table, rwkv_mu, rwkv_w0, rwkv_w2, rwkv_a0, rwkv_a2, rwkv_g2, rwkv_kk, rwkv_ka, rwkv_rk, rwkv_gn_g, rwkv_gn_b, w_br_a, w_br_b, w_br_c, w_out, ffn2_norm, ffn2_gate, ffn2_up, ffn2_down, final_norm):
    bp, S, d = x_prompt.shape
    nb, T_dec, _ = x_sample.shape
    depth = w_in.shape[0]
    assert bp == 1 and T_dec <= SEQ_PAD and S % SSD_CHUNK == 0
    n_p = bp * S
    n_rows = n_p + nb * SEQ_PAD
    xs_pad = jnp.concatenate([x_sample, jnp.zeros((nb, SEQ_PAD - T_dec, d), x_sample.dtype)], axis=1)
    h = jnp.concatenate([x_prompt.reshape(n_p, d), xs_pad.reshape(nb * SEQ_PAD, d)], axis=0)

    base = _t5_base(t5_table)
    e16 = jnp.repeat(jnp.eye(M_HEADS, dtype=F32), M_HEAD_DIM, axis=1)
    row = lambda x: x.reshape(1, -1).astype(F32)
    rc = min(RWKV_RC_CHUNK, S)
    shift_all = _shift_layout(state_shift).reshape(depth, nb, 1, 3 * R_WIDTH + LORA_W)
    outs = {k_: [] for k_ in ("k_p", "v_p", "ki_p", "conv_p", "sh_p", "k_s", "v_s", "ki_s", "conv_s", "sh_s")}
    ssm_p = ssm_s = wkv_p = wkv_s = None
    for l in range(depth):
        bf = lambda x: x.astype(BF16)
        h = _ffn(h, row(ffn1_norm[l]), bf(ffn1_gate[l]), bf(ffn1_up[l]), bf(ffn1_down[l]))
        p = _inproj(h, row(mix_norm[l]), _relayout_w_in(w_in[l]))
        kidx = _kidx_ln(p, row(kidx_ln_g[l]), row(kidx_ln_b[l]))

        ssd_par = (conv_w[l], row(conv_b[l]), row(dt_bias[l]), row(a_log[l]),
                   row(jnp.repeat(d_skip[l], M_HEAD_DIM)), row(ssm_norm[l]), e16)
        stack = dict(n_rows=n_rows, depth=depth, out_layer=l)
        ya, ssm_p = _ssd(p, 0, bp, S // SSD_CHUNK, SSD_CHUNK, SSD_CHUNK, 0,
                         jnp.zeros((1, bp, CONV_W - 1, CONV_DIM), F32),
                         jnp.zeros((1, bp, M_HEADS, M_HEAD_DIM, M_STATE), F32), *ssd_par,
                         st_buf=ssm_p, **stack)
        ya, ssm_s = _ssd(p, n_p, nb, 1, SEQ_PAD, T_dec, l, state_conv, state_ssm, *ssd_par,
                         y_buf=ya, st_buf=ssm_s, **stack)

        g2p = jnp.concatenate([rwkv_g2[l], jnp.zeros((LORA_W - R_DECAY_LORA - R_A_LORA - R_G_LORA,
                                                      R_WIDTH), F32)], axis=0)
        rw_par = (row(_shift_layout(rwkv_mu[l])), row(rwkv_w0[l]), rwkv_w2[l], row(rwkv_a0[l]),
                  rwkv_a2[l], g2p, row(rwkv_kk[l]), row(rwkv_ka[l]), row(rwkv_rk[l]),
                  row(rwkv_gn_g[l]), row(rwkv_gn_b[l]))
        sw = 3 * R_WIDTH + LORA_W
        yc, wkv_p = _rwkv(p, 0, bp, S // rc, rc, rc, 0, jnp.zeros((1, bp, 1, sw), F32),
                          jnp.zeros((1, bp, R_HEADS, R_HEAD, R_HEAD), F32), *rw_par,
                          st_buf=wkv_p, **stack)
        yc, wkv_s = _rwkv(p, n_p, nb, 1, SEQ_PAD, T_dec, l, shift_all, state_wkv, *rw_par,
                          y_buf=yc, st_buf=wkv_s, **stack)

        kp = p[:n_p, C_K:C_K + 256]
        vp = p[:n_p, C_V:C_V + 256]
        kit = bf(kidx[:n_p].T)
        kt = bf(kp.reshape(n_p, KV_HEADS, HEAD_DIM).transpose(1, 2, 0))
        vg = vp.reshape(n_p, KV_HEADS, HEAD_DIM).transpose(1, 0, 2)
        va = bf(jnp.concatenate([vg, jnp.ones((KV_HEADS, n_p, 1), F32),
                                 jnp.zeros((KV_HEADS, n_p, LANE - HEAD_DIM - 1), F32)], axis=-1))
        yb = _dsa_prompt(p, kit, kt, va, base, S, n_rows)
        yb = _dsa_sample(p, n_p, nb, kidx, base, l, cache_k, cache_v, cache_kidx, page_table, T_dec, yb)
        h = _merge(h, p, ya, yb, yc, bf(w_br_a[l]), bf(w_br_b[l]), bf(w_br_c[l]), bf(w_out[l]))
        h = _ffn(h, row(ffn2_norm[l]), bf(ffn2_gate[l]), bf(ffn2_up[l]), bf(ffn2_down[l]))

        ps = p[n_p:].reshape(nb, SEQ_PAD, P_W)[:, :T_dec]
        xbc = lambda t: jnp.concatenate([t[..., C_XS:C_XS + 1024], t[..., C_BC:C_BC + 1024]], axis=-1)
        prj = lambda t: jnp.concatenate([t[..., C_RR:C_RR + 3 * R_WIDTH],
                                         t[..., C_LORA:C_LORA + R_PROJ - 3 * R_WIDTH]], axis=-1)
        outs["k_p"].append(kp.reshape(bp, S, KV_HEADS, HEAD_DIM))
        outs["v_p"].append(vp.reshape(bp, S, KV_HEADS, HEAD_DIM))
        outs["ki_p"].append(kidx[:n_p].reshape(bp, S, IDX_DIM))
        outs["conv_p"].append(xbc(p[n_p - (CONV_W - 1):n_p]).reshape(bp, CONV_W - 1, CONV_DIM))
        outs["sh_p"].append(prj(p[n_p - 1:n_p]).reshape(bp, R_PROJ))
        outs["k_s"].append(ps[..., C_K:C_K + 256].reshape(nb, T_dec, KV_HEADS, HEAD_DIM))
        outs["v_s"].append(ps[..., C_V:C_V + 256].reshape(nb, T_dec, KV_HEADS, HEAD_DIM))
        outs["ki_s"].append(kidx[n_p:].reshape(nb, SEQ_PAD, IDX_DIM)[:, :T_dec])
        conv_full = jnp.concatenate([state_conv[l], xbc(ps)], axis=1)
        outs["conv_s"].append(conv_full[:, T_dec:])
        outs["sh_s"].append(prj(ps[:, T_dec - 1]))

    y = _final_norm(h, row(final_norm))
    y_prompt = y[:n_p].reshape(bp, S, d)
    y_sample = y[n_p:].reshape(nb, SEQ_PAD, d)[:, :T_dec]
    stk = lambda name: jnp.stack(outs[name], axis=0)
    return (y_prompt, y_sample, stk("k_p"), stk("v_p"), stk("ki_p"), ssm_p, stk("conv_p"),
            wkv_p, stk("sh_p"), stk("k_s"), stk("v_s"), stk("ki_s"), ssm_s,
            stk("conv_s"), wkv_s, stk("sh_s"))
```

```python
import functools
import math

import jax
import jax.numpy as jnp
from jax import lax
from jax.experimental import pallas as pl
from jax.experimental.pallas import tpu as pltpu

F32 = jnp.float32
BF16 = jnp.bfloat16
I32 = jnp.int32

D_MODEL = 2048
MIX_W = D_MODEL // 2
M_HEAD_DIM = 64
M_INNER = MIX_W
M_HEADS = M_INNER // M_HEAD_DIM
M_GROUPS = 4
M_STATE = 128
CONV_W = 4
CONV_DIM = M_INNER + 2 * M_GROUPS * M_STATE
SSD_CHUNK = 128
HEAD_DIM = 64
ATT_HEADS = MIX_W // HEAD_DIM
KV_HEADS = 4
GQA = ATT_HEADS // KV_HEADS
IDX_HEADS = 16
IDX_DIM = 64
TOPK_MAX = 256
T5_BUCKETS = 32
T5_MAX_DIST = 128
R_HEAD = 64
R_WIDTH = MIX_W
R_HEADS = R_WIDTH // R_HEAD
R_DECAY_LORA = 64
R_A_LORA = 64
R_G_LORA = 160
R_PROJ = 3 * R_WIDTH + R_DECAY_LORA + R_A_LORA + R_G_LORA
GN_EPS = 64e-5
D_FF = 256 * ((8 * D_MODEL // 3 + 255) // 256)
NORM_EPS = 1e-6
PAGE_SIZE = 128

SEQ_PAD = 8
LANE = 128
SUBLANES = 8
MXU_DIM = 256
VMEM_LIMIT = 56 * 2**20
INT_MIN = -2**31
NEG_BIG = -1e30

C_Z, C_XS, C_BC, C_Q, C_QI, C_RR, C_RK, C_RV = (i * 1024 for i in range(8))
C_GATE = 8192
C_LORA = C_GATE + 3 * D_MODEL
LORA_W = 512
C_K = C_LORA + LORA_W
C_V = C_K + 256
C_MISC = C_V + 256
P_W = C_MISC + LANE
MISC_DT = IDX_DIM
MISC_WI = IDX_DIM + M_HEADS
RWKV_RC_CHUNK = 64

HI = lax.Precision.HIGHEST


def _cparams(sem):
    return pltpu.CompilerParams(dimension_semantics=sem, vmem_limit_bytes=VMEM_LIMIT)


def _mm(a, b):
    return jnp.dot(a.astype(BF16), b.astype(BF16), preferred_element_type=F32)


def _mm_nt(a, b):
    return lax.dot_general(a.astype(BF16), b.astype(BF16), (((1,), (1,)), ((), ())),
                           preferred_element_type=F32)


def _mm_tn(a, b):
    return lax.dot_general(a.astype(BF16), b.astype(BF16), (((0,), (0,)), ((), ())),
                           preferred_element_type=F32)


def _mm_hi(a, b):
    return jnp.dot(a, b, precision=HI, preferred_element_type=F32)


def _mm_nt_hi(a, b):
    return lax.dot_general(a, b, (((1,), (1,)), ((), ())), precision=HI, preferred_element_type=F32)


def _mm_tn_hi(a, b):
    return lax.dot_general(a, b, (((0,), (0,)), ((), ())), precision=HI, preferred_element_type=F32)


def _sigmoid(x):
    return jax.nn.sigmoid(x)


def _silu(x):
    return x * jax.nn.sigmoid(x)


def _full(shape):
    nd = len(shape)
    return pl.BlockSpec(shape, lambda *_: (0,) * nd)


def _ffn_kernel(x_ref, g_ref, wg_ref, wu_ref, wd_ref, o_ref, n_ref):
    f = pl.program_id(1)

    @pl.when(f == 0)
    def _():
        x = x_ref[...]
        ms = jnp.mean(x * x, axis=-1, keepdims=True)
        n_ref[...] = (x * lax.rsqrt(ms + NORM_EPS) * g_ref[...]).astype(BF16)
        o_ref[...] = x

    n = n_ref[...]
    a = jnp.dot(n, wg_ref[...], preferred_element_type=F32)
    b = jnp.dot(n, wu_ref[...], preferred_element_type=F32)
    hid = (0.5 * _silu(a) * b).astype(BF16)
    o_ref[...] += jnp.dot(hid, wd_ref[...], preferred_element_type=F32)


def _ffn(x, g, wg, wu, wd, tm=512, tf=512):
    m = x.shape[0]
    tm = min(tm, m)
    return pl.pallas_call(
        _ffn_kernel,
        grid=(m // tm, D_FF // tf),
        in_specs=[pl.BlockSpec((tm, D_MODEL), lambda i, f: (i, 0)),
                  pl.BlockSpec((1, D_MODEL), lambda i, f: (0, 0)),
                  pl.BlockSpec((D_MODEL, tf), lambda i, f: (0, f)),
                  pl.BlockSpec((D_MODEL, tf), lambda i, f: (0, f)),
                  pl.BlockSpec((tf, D_MODEL), lambda i, f: (f, 0))],
        out_specs=pl.BlockSpec((tm, D_MODEL), lambda i, f: (i, 0)),
        out_shape=jax.ShapeDtypeStruct((m, D_MODEL), F32),
        scratch_shapes=[pltpu.VMEM((tm, D_MODEL), BF16)],
        compiler_params=_cparams(("parallel", "arbitrary")),
        name="ffn",
    )(x, g, wg, wu, wd)


def _inproj_kernel(x_ref, g_ref, w_ref, o_ref, n_ref):
    j = pl.program_id(1)

    @pl.when(j == 0)
    def _():
        x = x_ref[...]
        ms = jnp.mean(x * x, axis=-1, keepdims=True)
        n_ref[...] = (x * lax.rsqrt(ms + NORM_EPS) * g_ref[...]).astype(BF16)

    o_ref[...] = jnp.dot(n_ref[...], w_ref[...], preferred_element_type=F32)


def _inproj(x, g, w, tm=512, tn=1408):
    m = x.shape[0]
    tm = min(tm, m)
    return pl.pallas_call(
        _inproj_kernel,
        grid=(m // tm, P_W // tn),
        in_specs=[pl.BlockSpec((tm, D_MODEL), lambda i, j: (i, 0)),
                  pl.BlockSpec((1, D_MODEL), lambda i, j: (0, 0)),
                  pl.BlockSpec((D_MODEL, tn), lambda i, j: (0, j))],
        out_specs=pl.BlockSpec((tm, tn), lambda i, j: (i, j)),
        out_shape=jax.ShapeDtypeStruct((m, P_W), F32),
        scratch_shapes=[pltpu.VMEM((tm, D_MODEL), BF16)],
        compiler_params=_cparams(("parallel", "arbitrary")),
        name="inproj",
    )(x, g, w)


def _merge_kernel(h_ref, ya_ref, yb_ref, yc_ref, ga_ref, gb_ref, gc_ref,
                  wa_ref, wb_ref, wc_ref, wo_ref, o_ref):
    j = pl.program_id(1)

    @pl.when(j == 0)
    def _():
        o_ref[...] = h_ref[...]

    merged = (_sigmoid(ga_ref[...]) * jnp.dot(ya_ref[...], wa_ref[...], preferred_element_type=F32)
              + _sigmoid(gb_ref[...]) * jnp.dot(yb_ref[...], wb_ref[...], preferred_element_type=F32)
              + _sigmoid(gc_ref[...]) * jnp.dot(yc_ref[...], wc_ref[...], preferred_element_type=F32))
    o_ref[...] += jnp.dot(merged.astype(BF16), wo_ref[...], preferred_element_type=F32)


def _merge(h, p, ya, yb, yc, wa, wb, wc, wo, tm=512, tk=512):
    m = h.shape[0]
    tm = min(tm, m)
    gb0 = C_GATE // tk
    nk = D_MODEL // tk
    yspec = pl.BlockSpec((tm, MIX_W), lambda i, j: (i, 0))
    wspec = pl.BlockSpec((MIX_W, tk), lambda i, j: (0, j))
    return pl.pallas_call(
        _merge_kernel,
        grid=(m // tm, nk),
        in_specs=[pl.BlockSpec((tm, D_MODEL), lambda i, j: (i, 0)), yspec, yspec, yspec,
                  pl.BlockSpec((tm, tk), lambda i, j: (i, gb0 + j)),
                  pl.BlockSpec((tm, tk), lambda i, j: (i, gb0 + nk + j)),
                  pl.BlockSpec((tm, tk), lambda i, j: (i, gb0 + 2 * nk + j)),
                  wspec, wspec, wspec,
                  pl.BlockSpec((tk, D_MODEL), lambda i, j: (j, 0))],
        out_specs=pl.BlockSpec((tm, D_MODEL), lambda i, j: (i, 0)),
        out_shape=jax.ShapeDtypeStruct((m, D_MODEL), F32),
        compiler_params=_cparams(("parallel", "arbitrary")),
        name="merge",
    )(h, ya, yb, yc, p, p, p, wa, wb, wc, wo)


def _final_norm_kernel(x_ref, g_ref, o_ref):
    x = x_ref[...]
    ms = jnp.mean(x * x, axis=-1, keepdims=True)
    o_ref[...] = x * lax.rsqrt(ms + NORM_EPS) * g_ref[...]


def _final_norm(x, g, tm=512):
    m = x.shape[0]
    tm = min(tm, m)
    return pl.pallas_call(
        _final_norm_kernel,
        grid=(m // tm,),
        in_specs=[pl.BlockSpec((tm, D_MODEL), lambda i: (i, 0)), _full((1, D_MODEL))],
        out_specs=pl.BlockSpec((tm, D_MODEL), lambda i: (i, 0)),
        out_shape=jax.ShapeDtypeStruct((m, D_MODEL), F32),
        compiler_params=_cparams(("parallel",)),
        name="final_norm",
    )(x, g)


def _kidx_ln_kernel(m_ref, g_ref, b_ref, o_ref):
    x = m_ref[:, 0:IDX_DIM]
    mu = jnp.mean(x, axis=-1, keepdims=True)
    var = jnp.mean(jnp.square(x - mu), axis=-1, keepdims=True)
    o_ref[...] = (x - mu) * lax.rsqrt(var + NORM_EPS) * g_ref[...] + b_ref[...]


def _kidx_ln(p, g, b, tm=512):
    m = p.shape[0]
    tm = min(tm, m)
    return pl.pallas_call(
        _kidx_ln_kernel,
        grid=(m // tm,),
        in_specs=[pl.BlockSpec((tm, LANE), lambda i: (i, C_MISC // LANE)),
                  _full((1, IDX_DIM)), _full((1, IDX_DIM))],
        out_specs=pl.BlockSpec((tm, IDX_DIM), lambda i: (i, 0)),
        out_shape=jax.ShapeDtypeStruct((m, IDX_DIM), F32),
        compiler_params=_cparams(("parallel",)),
        name="kidx_ln",
    )(p, g, b)


def _shifted(x, k, prev, rows):
    y = pltpu.roll(x, k, 0)
    for r in range(k):
        y = jnp.where(rows == r, prev[3 - k + r:4 - k + r, :], y)
    return y


def _ssd_kernel(*refs, T, n_valid, n_alias):
    (z_ref, xs_ref, bc_ref, misc_ref, cst_ref, st0_ref, cw_ref, cb_ref, dtb_ref,
     alog_ref, dexp_ref, norm_ref, e16_ref) = refs[:13]
    y_ref, stout_ref, prev_ref, st_ref, yd_ref = refs[13 + n_alias:]
    c = pl.program_id(1)
    nc = pl.num_programs(1)

    @pl.when(c == 0)
    def _():
        prev_ref[...] = cst_ref[0, 0]
        st_ref[...] = st0_ref[0, 0]

    rows = lax.broadcasted_iota(I32, (T, 1), 0)
    cw = cw_ref[...]
    prev = prev_ref[...]

    def conv(x, lo, hi):
        pv = prev[:, lo:hi]
        y = cb_ref[:, lo:hi] + x * cw[3:4, lo:hi]
        for k in (1, 2, 3):
            y = y + _shifted(x, k, pv, rows) * cw[3 - k:4 - k, lo:hi]
        return _silu(y)

    xs_raw = xs_ref[...]
    bc_raw = bc_ref[...]
    xs = conv(xs_raw, 0, M_INNER)
    bc = conv(bc_raw, M_INNER, CONV_DIM)
    prev_ref[:, 0:M_INNER] = xs_raw[T - 3:T, :]
    prev_ref[:, M_INNER:CONV_DIM] = bc_raw[T - 3:T, :]

    dt = jax.nn.softplus(misc_ref[:, MISC_DT:MISC_DT + M_HEADS] + dtb_ref[...])
    if n_valid < T:
        dt = jnp.where(rows < n_valid, dt, 0.0)
    ad = dt * (-jnp.exp(alog_ref[...]))
    ri = lax.broadcasted_iota(I32, (T, T), 0)
    ci = lax.broadcasted_iota(I32, (T, T), 1)
    lower = ri >= ci
    tril = lower.astype(F32)
    triu = (ri <= ci).astype(F32)
    eye16 = (lax.broadcasted_iota(I32, (M_HEADS, M_HEADS), 0)
             == lax.broadcasted_iota(I32, (M_HEADS, M_HEADS), 1)).astype(F32)
    cs = _mm_hi(tril, ad)
    cst = _mm_hi(_mm_nt_hi(eye16, ad), triu)
    e16 = e16_ref[...]
    dt_e = _mm_hi(dt, e16)
    ecs_e = _mm_hi(jnp.exp(cs), e16)
    wl_e = _mm_hi(jnp.exp(cs[T - 1:T, :] - cs), e16)
    xd = xs * dt_e
    xdw = (xd * wl_e).astype(BF16)
    xd = xd.astype(BF16)
    bcb = bc.astype(BF16)
    for g in range(M_GROUPS):
        bg = bcb[:, g * M_STATE:(g + 1) * M_STATE]
        cg = bcb[:, (M_GROUPS + g) * M_STATE:(M_GROUPS + g + 1) * M_STATE]
        cbm = _mm_nt(cg, bg)
        for hh in range(M_HEADS // M_GROUPS):
            h = g * (M_HEADS // M_GROUPS) + hh
            hs = slice(h * M_HEAD_DIM, (h + 1) * M_HEAD_DIM)
            diff = cs[:, h:h + 1] - cst[h:h + 1, :]
            lm = jnp.exp(jnp.where(lower, diff, -jnp.inf))
            yd = _mm(cbm * lm, xd[:, hs])
            st = st_ref[h]
            yo = _mm_nt(cg, st)
            yd_ref[:, hs] = yd + yo * ecs_e[:, hs]
            upd = _mm_tn(xdw[:, hs], bg)
            dec = jnp.exp(cst[h:h + 1, T - 1:T])
            st_ref[h] = st * dec + upd

    y = yd_ref[...] + xs * dexp_ref[...]
    y = y * _silu(z_ref[...])
    gw = M_INNER // M_GROUPS
    for g in range(M_GROUPS):
        yg = y[:, g * gw:(g + 1) * gw]
        ms = jnp.mean(yg * yg, axis=-1, keepdims=True)
        y_ref[:, g * gw:(g + 1) * gw] = (yg * lax.rsqrt(ms + NORM_EPS)
                                         * norm_ref[:, g * gw:(g + 1) * gw]).astype(y_ref.dtype)

    @pl.when(c == nc - 1)
    def _():
        stout_ref[0, 0] = st_ref[...]


def _alias_inputs(n_in, bufs):
    specs, args, aliases = [], [], {}
    for k, b in enumerate(bufs):
        if b is not None:
            aliases[n_in + len(args)] = k
            specs.append(pl.BlockSpec(memory_space=pl.ANY))
            args.append(b)
    return specs, args, aliases


def _ssd(p, row0, n_seq, n_chunks, T, n_valid, layer, conv_state, ssm0, cw, cb, dtb, alog, dexp, norm,
         e16, *, n_rows, depth, out_layer, y_buf=None, st_buf=None):
    rb0 = row0 // T

    def rmap(col):
        return lambda s, c: (rb0 + s * n_chunks + c, col)

    args = [p, p, p, p, conv_state, ssm0, cw, cb, dtb, alog, dexp, norm, e16]
    a_specs, a_args, aliases = _alias_inputs(len(args), (y_buf, st_buf))
    kern = functools.partial(_ssd_kernel, T=T, n_valid=n_valid, n_alias=len(a_args))
    return pl.pallas_call(
        kern,
        grid=(n_seq, n_chunks),
        in_specs=[pl.BlockSpec((T, 1024), rmap(C_Z // 1024)),
                  pl.BlockSpec((T, 1024), rmap(C_XS // 1024)),
                  pl.BlockSpec((T, 1024), rmap(C_BC // 1024)),
                  pl.BlockSpec((T, LANE), rmap(C_MISC // LANE)),
                  pl.BlockSpec((1, 1, CONV_W - 1, CONV_DIM), lambda s, c: (layer, s, 0, 0)),
                  pl.BlockSpec((1, 1, M_HEADS, M_HEAD_DIM, M_STATE), lambda s, c: (layer, s, 0, 0, 0)),
                  _full((CONV_W, CONV_DIM)), _full((1, CONV_DIM)), _full((1, M_HEADS)),
                  _full((1, M_HEADS)), _full((1, M_INNER)), _full((1, M_INNER)),
                  _full((M_HEADS, M_INNER))] + a_specs,
        out_specs=[pl.BlockSpec((T, M_INNER), rmap(0)),
                   pl.BlockSpec((1, 1, M_HEADS, M_HEAD_DIM, M_STATE), lambda s, c: (out_layer, s, 0, 0, 0))],
        out_shape=[jax.ShapeDtypeStruct((n_rows, M_INNER), BF16),
                   jax.ShapeDtypeStruct((depth, n_seq, M_HEADS, M_HEAD_DIM, M_STATE), F32)],
        scratch_shapes=[pltpu.VMEM((CONV_W - 1, CONV_DIM), F32),
                        pltpu.VMEM((M_HEADS, M_HEAD_DIM, M_STATE), F32),
                        pltpu.VMEM((T, M_INNER), F32)],
        input_output_aliases=aliases,
        compiler_params=_cparams(("parallel", "arbitrary")),
        name="ssd",
    )(*args, *a_args)


def _rwkv_kernel(*refs, T, n_valid, HG, n_alias):
    (r_ref, k_ref, v_ref, lo_ref, sh0_ref, s0_ref, mu_ref, w0_ref, w2_ref, a0_ref,
     a2_ref, g2_ref, kkw_ref, kaw_ref, rk_ref, gng_ref, gnb_ref) = refs[:17]
    (y_ref, sout_ref, last_ref, s_ref, yb_ref, xs_ref, hr_ref,
     gm_ref) = refs[17 + n_alias:]
    c = pl.program_id(1)
    nc = pl.num_programs(1)

    @pl.when(c == 0)
    def _():
        last_ref[...] = sh0_ref[0, 0]
        s_ref[...] = s0_ref[0, 0]

    rows = lax.broadcasted_iota(I32, (T, 1), 0)

    def mix(ref, lo, hi):
        x = ref[...]
        prev = jnp.where(rows == 0, last_ref[:, lo:hi], pltpu.roll(x, 1, 0))
        last_ref[:, lo:hi] = x[T - 1:T, :]
        return x + (prev - x) * mu_ref[:, lo:hi]

    r = mix(r_ref, 0, 1024)
    k = mix(k_ref, 1024, 2048)
    v = mix(v_ref, 2048, 3072)
    lo = mix(lo_ref, 3072, 3072 + LORA_W)
    xw = lo[:, 0:R_DECAY_LORA]
    xa = lo[:, R_DECAY_LORA:R_DECAY_LORA + R_A_LORA]
    xg = lo[:, R_DECAY_LORA + R_A_LORA:LORA_W]
    wl = -jax.nn.softplus(-(w0_ref[...] + _mm_hi(jnp.tanh(xw), w2_ref[...]))) - 0.5
    ld = -jnp.exp(wl)
    a = _sigmoid(a0_ref[...] + _mm_hi(xa, a2_ref[...]))
    gate = _mm_hi(_sigmoid(xg), g2_ref[...])
    kk = k * kkw_ref[...]
    km = k * (1.0 + (a - 1.0) * kaw_ref[...])
    if n_valid < T:
        ok = rows < n_valid
        ld = jnp.where(ok, ld, 0.0)
        kk = jnp.where(ok, kk, 0.0)
        km = jnp.where(ok, km, 0.0)
        v = jnp.where(ok, v, 0.0)

    ri = lax.broadcasted_iota(I32, (T, T), 0)
    ci = lax.broadcasted_iota(I32, (T, T), 1)
    cum = _mm_hi((ri >= ci).astype(F32), ld)
    eg = jnp.exp(cum)
    egm = jnp.exp(cum - ld)
    ei = jnp.exp(-cum)
    el = jnp.exp(cum[T - 1:T, :] - cum)
    r_eg = r * eg
    k_ei = km * ei
    k_el = km * el
    rk_sum = r * km * rk_ref[...]
    gng = gng_ref[...]
    gnb = gnb_ref[...]

    R = HG * T
    GW = HG * R_HEAD
    rr = lax.broadcasted_iota(I32, (R, R), 0)
    cc = lax.broadcasted_iota(I32, (R, R), 1)
    same = (rr // T) == (cc // T)
    strict = jnp.logical_and(same, rr > cc)
    incl = jnp.logical_and(same, rr >= cc)
    head_of = (lax.broadcasted_iota(I32, (R, GW), 0) // T) == (lax.broadcasted_iota(I32, (R, GW), 1) // R_HEAD)
    blk_r = (rr % T) // SUBLANES
    blk_c = (cc % T) // SUBLANES

    n_groups = R_HEADS // HG

    def head_slices(gi):
        return [slice((gi * HG + hh) * R_HEAD, (gi * HG + hh + 1) * R_HEAD) for hh in range(HG)]

    def stack(x, gi):
        return jnp.concatenate([x[:, s_] for s_ in head_slices(gi)], axis=0)

    def unit_kk(gi):
        kks = stack(kk, gi)
        nrm = jnp.sqrt(jnp.sum(kks * kks, axis=-1, keepdims=True))
        return kks / jnp.maximum(nrm, 1e-12)

    for gi in range(n_groups):
        kks = unit_kk(gi)
        q2 = jnp.concatenate([kks * stack(egm, gi), stack(r_eg, gi)], axis=0)
        k2 = jnp.concatenate([stack(k_ei, gi), kks * stack(a, gi) * stack(ei, gi)], axis=0)
        gm = _mm_nt(q2, k2)
        gm_ref[gi] = gm
        scat = s_ref[gi * HG:(gi + 1) * HG].reshape(GW, R_HEAD)
        hmw = _mm_nt(q2, scat)
        hk = jnp.concatenate([hmw[hh * T:(hh + 1) * T, hh * R_HEAD:(hh + 1) * R_HEAD]
                              for hh in range(HG)], axis=0)
        hr_ref[gi] = jnp.concatenate([hmw[R + hh * T:R + (hh + 1) * T, hh * R_HEAD:(hh + 1) * R_HEAD]
                                      for hh in range(HG)], axis=0)
        xs_ref[gi] = hk + _mm(jnp.where(strict, gm[0:R, 0:R], 0.0), stack(v, gi))

    for jb in range(T // SUBLANES):
        for gi in range(n_groups):
            for hh in range(HG):
                r0 = hh * T + jb * SUBLANES
                xb = xs_ref[gi, r0:r0 + SUBLANES, :]
                lrow = gm_ref[gi, r0:r0 + SUBLANES, R:2 * R]
                for s in range(1, SUBLANES):
                    below = lax.broadcasted_iota(I32, (SUBLANES, 1), 0) >= s
                    xb = xb - jnp.where(below, lrow[:, r0 + s - 1:r0 + s], 0.0) * xb[s - 1:s, :]
                xs_ref[gi, r0:r0 + SUBLANES, :] = xb
            if jb + 1 < T // SUBLANES:
                later = jnp.where(jnp.logical_and(jnp.logical_and(same, blk_c == jb), blk_r > jb),
                                  gm_ref[gi, 0:R, R:2 * R], 0.0)
                xs_ref[gi] = xs_ref[gi] - _mm(later, xs_ref[gi])

    for gi in range(n_groups):
        hsl = head_slices(gi)
        kks = unit_kk(gi)
        bs = kks * stack(a, gi)
        vs = stack(v, gi)
        u = -xs_ref[gi]
        ak = jnp.where(incl, gm_ref[gi, R:2 * R, 0:R], 0.0)
        ab = jnp.where(incl, gm_ref[gi, R:2 * R, R:2 * R], 0.0)
        scat = s_ref[gi * HG:(gi + 1) * HG].reshape(GW, R_HEAD)
        y = hr_ref[gi] + _mm(jnp.concatenate([ak, ab], axis=1), jnp.concatenate([vs, u], axis=0))
        vw = jnp.where(head_of, jnp.concatenate([v[:, gi * GW:(gi + 1) * GW]] * HG, axis=0), 0.0)
        uw = jnp.where(head_of, jnp.concatenate([u] * HG, axis=1), 0.0)
        upd = _mm_tn(jnp.concatenate([vw, uw], axis=0),
                     jnp.concatenate([stack(k_el, gi), bs * stack(el, gi)], axis=0))
        g_last = jnp.concatenate([jnp.broadcast_to(eg[T - 1:T, s_], (R_HEAD, R_HEAD)) for s_ in hsl],
                                 axis=0)
        s_ref[gi * HG:(gi + 1) * HG] = (scat * g_last + upd).reshape(HG, R_HEAD, R_HEAD)
        mu = jnp.mean(y, axis=-1, keepdims=True)
        var = jnp.mean(jnp.square(y - mu), axis=-1, keepdims=True)
        gs = jnp.concatenate([jnp.broadcast_to(gng[:, s_], (T, R_HEAD)) for s_ in hsl], axis=0)
        gb = jnp.concatenate([jnp.broadcast_to(gnb[:, s_], (T, R_HEAD)) for s_ in hsl], axis=0)
        yn = (y - mu) * lax.rsqrt(var + GN_EPS) * gs + gb
        yn = yn + jnp.sum(stack(rk_sum, gi), axis=-1, keepdims=True) * vs
        for hh in range(HG):
            yb_ref[:, hsl[hh]] = yn[hh * T:(hh + 1) * T]

    y_ref[...] = (yb_ref[...] * gate).astype(y_ref.dtype)

    @pl.when(c == nc - 1)
    def _():
        sout_ref[0, 0] = s_ref[...]


def _rwkv(p, row0, n_seq, n_chunks, T, n_valid, layer, shift0, s0, mu, w0, w2, a0, a2, g2, kkw, kaw,
          rk, gng, gnb, *, n_rows, depth, out_layer, y_buf=None, st_buf=None):
    rb0 = row0 // T

    def rmap(col):
        return lambda s, c: (rb0 + s * n_chunks + c, col)

    sw = 3072 + LORA_W
    hg = min(R_HEADS, max(1, MXU_DIM // T))
    args = [p, p, p, p, shift0, s0, mu, w0, w2, a0, a2, g2, kkw, kaw, rk, gng, gnb]
    a_specs, a_args, aliases = _alias_inputs(len(args), (y_buf, st_buf))
    kern = functools.partial(_rwkv_kernel, T=T, n_valid=n_valid, HG=hg, n_alias=len(a_args))
    return pl.pallas_call(
        kern,
        grid=(n_seq, n_chunks),
        in_specs=[pl.BlockSpec((T, 1024), rmap(C_RR // 1024)),
                  pl.BlockSpec((T, 1024), rmap(C_RK // 1024)),
                  pl.BlockSpec((T, 1024), rmap(C_RV // 1024)),
                  pl.BlockSpec((T, LORA_W), rmap(C_LORA // LORA_W)),
                  pl.BlockSpec((1, 1, 1, sw), lambda s, c: (layer, s, 0, 0)),
                  pl.BlockSpec((1, 1, R_HEADS, R_HEAD, R_HEAD), lambda s, c: (layer, s, 0, 0, 0)),
                  _full((1, sw)), _full((1, R_WIDTH)), _full((R_DECAY_LORA, R_WIDTH)),
                  _full((1, R_WIDTH)), _full((R_A_LORA, R_WIDTH)),
                  _full((LORA_W - R_DECAY_LORA - R_A_LORA, R_WIDTH)),
                  _full((1, R_WIDTH)), _full((1, R_WIDTH)), _full((1, R_WIDTH)),
                  _full((1, R_WIDTH)), _full((1, R_WIDTH))] + a_specs,
        out_specs=[pl.BlockSpec((T, R_WIDTH), rmap(0)),
                   pl.BlockSpec((1, 1, R_HEADS, R_HEAD, R_HEAD), lambda s, c: (out_layer, s, 0, 0, 0))],
        out_shape=[jax.ShapeDtypeStruct((n_rows, R_WIDTH), BF16),
                   jax.ShapeDtypeStruct((depth, n_seq, R_HEADS, R_HEAD, R_HEAD), F32)],
        scratch_shapes=[pltpu.VMEM((1, sw), F32),
                        pltpu.VMEM((R_HEADS, R_HEAD, R_HEAD), F32),
                        pltpu.VMEM((T, R_WIDTH), F32),
                        pltpu.VMEM((R_HEADS // hg, hg * T, R_HEAD), F32),
                        pltpu.VMEM((R_HEADS // hg, hg * T, R_HEAD), F32),
                        pltpu.VMEM((R_HEADS // hg, 2 * hg * T, 2 * hg * T), F32)],
        input_output_aliases=aliases,
        compiler_params=_cparams(("parallel", "arbitrary")),
        name="rwkv",
    )(*args, *a_args)


def _score_keys(acc, causal):
    bits = pltpu.bitcast(acc, I32)
    key = bits ^ ((bits >> 31) & 0x7FFFFFFF)
    return jnp.where(causal, key, INT_MIN)


def _kth_threshold(count_ge, shape, top, bits_per_step=1):
    tb = jnp.zeros(shape, I32)
    for shift in range(32 - bits_per_step, -1, -bits_per_step):
        digit = jnp.zeros(shape, I32)
        for d in range(1, 1 << bits_per_step):
            inc = d << shift
            inc = inc - (1 << 32) if inc >= (1 << 31) else inc
            cnt = count_ge((tb | jnp.int32(inc)) ^ jnp.int32(INT_MIN))
            digit = digit + jnp.where(cnt >= top, 1, 0)
        tb = tb | (digit << shift)
    return jnp.maximum(tb ^ jnp.int32(INT_MIN), INT_MIN + 1)


def _toeplitz_bias(base_row, nrows):
    return pltpu.roll(jnp.broadcast_to(base_row, (nrows, 2 * LANE)), 0, 1, stride=1, stride_axis=0)


def _dsa_prompt_kernel(q_ref, qi_ref, misc_ref, kit_ref, kt_ref, va_ref, base_ref, o_ref,
                       key_ref, mx_ref, acc_ref, near_ref, *, top, TA):
    i = pl.program_id(0)
    QB = LANE
    q0 = i * QB
    n_a = (q0 + QB + TA - 1) // TA
    wi = misc_ref[:, MISC_WI:MISC_WI + IDX_HEADS] * (IDX_HEADS ** -0.5 * IDX_DIM ** -0.5)
    qi = qi_ref[...]
    qis = jnp.concatenate([qi[:, h * IDX_DIM:(h + 1) * IDX_DIM] for h in range(IDX_HEADS)],
                          axis=0).astype(BF16)
    rowq = q0 + lax.broadcasted_iota(I32, (QB, 1), 0)

    def score_tile(kt, carry):
        off = pl.multiple_of(kt * TA, TA)
        s = jnp.maximum(jnp.dot(qis, kit_ref[:, pl.ds(off, TA)], preferred_element_type=F32), 0.0)
        acc = s[0:QB] * wi[:, 0:1]
        for h in range(1, IDX_HEADS):
            acc = acc + s[h * QB:(h + 1) * QB] * wi[:, h:h + 1]
        col = off + lax.broadcasted_iota(I32, (1, TA), 1)
        key_ref[:, pl.ds(off, TA)] = _score_keys(acc, col <= rowq)
        return carry

    lax.fori_loop(0, n_a, score_tile, 0)

    def count_ge(cand):
        def body(kt, cnt):
            off = pl.multiple_of(kt * TA, TA)
            ge = jnp.where(key_ref[:, pl.ds(off, TA)] >= cand, 1.0, 0.0)
            for j in range(TA // LANE):
                cnt = cnt + ge[:, j * LANE:(j + 1) * LANE]
            return cnt
        cnt = lax.fori_loop(0, n_a, body, jnp.zeros((QB, LANE), F32))
        return jnp.sum(cnt, axis=-1, keepdims=True)

    thr = _kth_threshold(count_ge, (QB, 1), float(top))

    n_ge = count_ge(thr)

    @pl.when(jnp.max(n_ge) > top)
    def _():
        def count_tied_before(pos):
            def body(kt, cnt):
                off = pl.multiple_of(kt * TA, TA)
                col = off + lax.broadcasted_iota(I32, (1, TA), 1)
                hit = jnp.where(key_ref[:, pl.ds(off, TA)] == thr, jnp.where(col < pos, 1.0, 0.0), 0.0)
                for j in range(TA // LANE):
                    cnt = cnt + hit[:, j * LANE:(j + 1) * LANE]
                return cnt
            cnt = lax.fori_loop(0, n_a, body, jnp.zeros((QB, LANE), F32))
            return jnp.sum(cnt, axis=-1, keepdims=True)

        need = top - count_ge(thr + 1)
        last = jnp.zeros((QB, 1), I32)
        for bit in range(max(1, (kit_ref.shape[1] - 1).bit_length()) - 1, -1, -1):
            cand = last | jnp.int32(1 << bit)
            last = jnp.where(count_tied_before(cand) < need, cand, last)

        def drop(kt, carry):
            off = pl.multiple_of(kt * TA, TA)
            col = off + lax.broadcasted_iota(I32, (1, TA), 1)
            k = key_ref[:, pl.ds(off, TA)]
            key_ref[:, pl.ds(off, TA)] = jnp.where(k == thr, jnp.where(col > last, INT_MIN, k), k)
            return carry

        lax.fori_loop(0, n_a, drop, 0)

    @pl.when(i == 0)
    def _():
        for g in range(KV_HEADS):
            near_ref[g] = jnp.concatenate(
                [_toeplitz_bias(base_ref[h:h + 1, :], QB) - base_ref[h:h + 1, 2 * LANE - 1:2 * LANE]
                 for h in range(g * GQA, (g + 1) * GQA)], axis=0)

    q = q_ref[...]
    lim = (i - 1) * QB
    n_far = (jnp.maximum(lim, 0) + TA - 1) // TA
    qgs = []
    for g in range(KV_HEADS):
        heads = range(g * GQA, (g + 1) * GQA)
        qgs.append((jnp.concatenate([q[:, h * HEAD_DIM:(h + 1) * HEAD_DIM] for h in heads], axis=0)
                    * HEAD_DIM ** -0.5).astype(BF16))

    def far_logits(g, off, msk):
        s = jnp.dot(qgs[g], kt_ref[g, :, pl.ds(off, TA)], preferred_element_type=F32)
        return (s.reshape(GQA, QB, TA) + msk[None]).reshape(GQA * QB, TA)

    def far_mask(off):
        col = off + lax.broadcasted_iota(I32, (1, TA), 1)
        return jnp.where(key_ref[:, pl.ds(off, TA)] >= thr,
                         jnp.where(col < lim, 0.0, NEG_BIG), NEG_BIG)

    def near_logits(g, off, lo, width, msk):
        s = (jnp.dot(qgs[g], kt_ref[g, :, pl.ds(off, width)], preferred_element_type=F32)
             + near_ref[g, :, lo:lo + width])
        return (s.reshape(GQA, QB, width) + msk[None]).reshape(GQA * QB, width)

    def near_mask(off, width):
        return jnp.where(key_ref[:, pl.ds(off, width)] >= thr, 0.0, NEG_BIG)

    mx_ref[...] = jnp.full(mx_ref.shape, NEG_BIG, F32)
    acc_ref[...] = jnp.zeros(acc_ref.shape, F32)

    def accumulate(g, s, v_tile):
        width = s.shape[1]
        m = s[:, 0:LANE]
        for j in range(1, width // LANE):
            m = jnp.maximum(m, s[:, j * LANE:(j + 1) * LANE])
        m_old = mx_ref[g]
        m_new = jnp.maximum(m_old, jnp.broadcast_to(jnp.max(m, axis=-1, keepdims=True), m.shape))
        pr = jnp.exp(s - jnp.concatenate([m_new] * (width // LANE), axis=1))
        acc_ref[g] = (acc_ref[g] * jnp.exp(m_old - m_new)
                      + jnp.dot(pr.astype(BF16), v_tile, preferred_element_type=F32))
        mx_ref[g] = m_new

    def far_acc(kt, carry):
        off = pl.multiple_of(kt * TA, TA)
        msk = far_mask(off)
        for g in range(KV_HEADS):
            accumulate(g, far_logits(g, off, msk), va_ref[g, pl.ds(off, TA), :])
        return carry

    def near_acc(kt, lo, width):
        off = pl.multiple_of(kt * LANE, LANE)
        msk = near_mask(off, width)
        for g in range(KV_HEADS):
            accumulate(g, near_logits(g, off, lo, width, msk), va_ref[g, pl.ds(off, width), :])

    lax.fori_loop(0, n_far, far_acc, 0)

    @pl.when(i == 0)
    def _():
        near_acc(0, LANE, LANE)

    @pl.when(i >= 1)
    def _():
        near_acc(i - 1, 0, 2 * LANE)
    for g in range(KV_HEADS):
        acc = acc_ref[g]
        out = acc[:, 0:HEAD_DIM] / acc[:, HEAD_DIM:HEAD_DIM + 1]
        for hh in range(GQA):
            h = g * GQA + hh
            o_ref[:, h * HEAD_DIM:(h + 1) * HEAD_DIM] = out[hh * QB:(hh + 1) * QB].astype(o_ref.dtype)


def _dsa_prompt(p, kit, kt, va, base, S, n_rows):
    top = min(TOPK_MAX, S // 4)
    TA = min(512, S)
    kern = functools.partial(_dsa_prompt_kernel, top=top, TA=TA)
    return pl.pallas_call(
        kern,
        grid=(S // LANE,),
        in_specs=[pl.BlockSpec((LANE, 1024), lambda i: (i, C_Q // 1024)),
                  pl.BlockSpec((LANE, 1024), lambda i: (i, C_QI // 1024)),
                  pl.BlockSpec((LANE, LANE), lambda i: (i, C_MISC // LANE)),
                  _full((IDX_DIM, S)), _full((KV_HEADS, HEAD_DIM, S)), _full((KV_HEADS, S, LANE)),
                  _full((ATT_HEADS, 2 * LANE))],
        out_specs=pl.BlockSpec((LANE, MIX_W), lambda i: (i, 0)),
        out_shape=jax.ShapeDtypeStruct((n_rows, MIX_W), BF16),
        scratch_shapes=[pltpu.VMEM((LANE, S), I32),
                        pltpu.VMEM((KV_HEADS, GQA * LANE, LANE), F32),
                        pltpu.VMEM((KV_HEADS, GQA * LANE, LANE), F32),
                        pltpu.VMEM((KV_HEADS, GQA * LANE, 2 * LANE), F32)],
        compiler_params=_cparams(("arbitrary",)),
        name="dsa_prompt",
    )(p, p, p, kit, kt, va, base)


def _dsa_sample_kernel(pt_ref, q_ref, qi_ref, misc_ref, kn_ref, vn_ref, kin_ref, base_ref, *rest,
                       n_pages, n_valid, top):
    ki_pages = rest[0:n_pages]
    k_pages = rest[n_pages:2 * n_pages]
    v_pages = rest[2 * n_pages:3 * n_pages]
    o_ref = rest[3 * n_pages + 1]
    kib_ref, kb_ref, vb_ref, msk_ref = rest[3 * n_pages + 2:]
    TQ = SEQ_PAD
    lo = n_pages * LANE
    NK = lo + LANE
    wi = misc_ref[:, MISC_WI:MISC_WI + IDX_HEADS] * (IDX_HEADS ** -0.5 * IDX_DIM ** -0.5)
    qi = qi_ref[...]
    qis = jnp.concatenate([qi[:, h * IDX_DIM:(h + 1) * IDX_DIM] for h in range(IDX_HEADS)], axis=0)

    eye = (lax.broadcasted_iota(I32, (HEAD_DIM, HEAD_DIM), 0)
           == lax.broadcasted_iota(I32, (HEAD_DIM, HEAD_DIM), 1)).astype(BF16)

    def new_slot(x):
        xt = _mm_nt(eye, x)
        return jnp.concatenate([xt, jnp.zeros((HEAD_DIM, LANE - TQ), F32)], axis=1).astype(BF16)

    for t in range(n_pages):
        kib_ref[:, t * LANE:(t + 1) * LANE] = ki_pages[t][0, 0].astype(BF16)
        for g in range(KV_HEADS):
            kb_ref[g, :, t * LANE:(t + 1) * LANE] = k_pages[t][0, 0, g].astype(BF16)
            vb_ref[g, :, t * LANE:(t + 1) * LANE] = v_pages[t][0, 0, g].astype(BF16)
    kib_ref[:, lo:NK] = new_slot(kin_ref[...])
    kn = kn_ref[...]
    vn = vn_ref[...]
    for g in range(KV_HEADS):
        kb_ref[g, :, lo:NK] = new_slot(kn[:, g * HEAD_DIM:(g + 1) * HEAD_DIM])
        vb_ref[g, :, lo:NK] = new_slot(vn[:, g * HEAD_DIM:(g + 1) * HEAD_DIM])

    s = jnp.maximum(_mm(qis, kib_ref[...]), 0.0)
    acc = s[0:TQ] * wi[:, 0:1]
    for h in range(1, IDX_HEADS):
        acc = acc + s[h * TQ:(h + 1) * TQ] * wi[:, h:h + 1]
    trow = lax.broadcasted_iota(I32, (TQ, 1), 0)
    new = lax.broadcasted_iota(I32, (1, NK), 1) - lo
    keys = _score_keys(acc, new <= jnp.minimum(trow, n_valid - 1))

    def count_ge(cand):
        return jnp.sum(jnp.where(keys >= cand, 1.0, 0.0), axis=-1, keepdims=True)

    thr = _kth_threshold(count_ge, (TQ, 1), float(top), bits_per_step=4)
    msk_ref[...] = jnp.where(keys >= thr, 0.0, NEG_BIG)

    @pl.when(jnp.max(jnp.where(trow < n_valid, count_ge(thr), 0.0)) > top)
    def _():
        pos = new + lo
        tied = keys == thr

        def count_tied_before(p):
            return jnp.sum(jnp.where(tied, jnp.where(pos < p, 1.0, 0.0), 0.0), axis=-1, keepdims=True)

        need = top - count_ge(thr + 1)
        last = jnp.zeros((TQ, 1), I32)
        for bit in range(max(1, (NK - 1).bit_length()) - 1, -1, -1):
            cand = last | jnp.int32(1 << bit)
            last = jnp.where(count_tied_before(cand) < need, cand, last)
        msk_ref[...] = jnp.where(tied, jnp.where(pos > last, NEG_BIG, 0.0), msk_ref[...])

    msk = jnp.concatenate([msk_ref[...]] * ATT_HEADS, axis=0)

    q = q_ref[...] * HEAD_DIM ** -0.5
    zero = jnp.zeros((TQ, HEAD_DIM), F32)
    q_rows, nears = [], []
    for h in range(ATT_HEADS):
        qh = q[:, h * HEAD_DIM:(h + 1) * HEAD_DIM]
        q_rows.append(jnp.concatenate([qh if g == h // GQA else zero for g in range(KV_HEADS)], axis=1))
        nears.append(_toeplitz_bias(base_ref[h:h + 1, :], TQ) - base_ref[h:h + 1, 2 * LANE - 1:2 * LANE])
    qbd = jnp.concatenate(q_rows, axis=0)
    bias = jnp.concatenate([jnp.zeros((ATT_HEADS * TQ, lo - LANE), F32),
                            jnp.concatenate(nears, axis=0)], axis=1)
    s = _mm(qbd, kb_ref[...].reshape(KV_HEADS * HEAD_DIM, NK)) + bias + msk
    mx = jnp.max(s, axis=-1, keepdims=True)
    pr = jnp.exp(s - mx)
    pr = pr / jnp.sum(pr, axis=-1, keepdims=True)
    out = _mm_nt(pr, vb_ref[...].reshape(KV_HEADS * HEAD_DIM, NK))
    for h in range(ATT_HEADS):
        g = h // GQA
        o_ref[:, h * HEAD_DIM:(h + 1) * HEAD_DIM] = out[h * TQ:(h + 1) * TQ,
                                                        g * HEAD_DIM:(g + 1) * HEAD_DIM].astype(o_ref.dtype)


def _dsa_sample(p, row0, n_seq, kidx_ln, base, layer, cache_k, cache_v, cache_kidx, page_table, n_valid,
                y_buf):
    n_pages = page_table.shape[1]
    past = n_pages * PAGE_SIZE
    top = min(TOPK_MAX, (past + n_valid) // 4)
    rb0 = row0 // SEQ_PAD
    ck = jnp.transpose(cache_k, (0, 1, 3, 4, 2))
    cv = jnp.transpose(cache_v, (0, 1, 3, 4, 2))
    cki = jnp.transpose(cache_kidx, (0, 1, 3, 2))

    def rmap(col):
        return lambda b, pt: (rb0 + b, col)

    def pmap(pg):
        return lambda b, pt: (layer, pt[b, pg], 0, 0)

    def pmap5(pg):
        return lambda b, pt: (layer, pt[b, pg], 0, 0, 0)

    in_specs = [pl.BlockSpec((SEQ_PAD, 1024), rmap(C_Q // 1024)),
                pl.BlockSpec((SEQ_PAD, 1024), rmap(C_QI // 1024)),
                pl.BlockSpec((SEQ_PAD, LANE), rmap(C_MISC // LANE)),
                pl.BlockSpec((SEQ_PAD, 256), rmap(C_K // 256)),
                pl.BlockSpec((SEQ_PAD, 256), rmap(C_V // 256)),
                pl.BlockSpec((SEQ_PAD, IDX_DIM), lambda b, pt: (rb0 + b, 0)),
                pl.BlockSpec((ATT_HEADS, 2 * LANE), lambda b, pt: (0, 0))]
    in_specs += [pl.BlockSpec((1, 1, IDX_DIM, PAGE_SIZE), pmap(pg)) for pg in range(n_pages)]
    in_specs += [pl.BlockSpec((1, 1, KV_HEADS, HEAD_DIM, PAGE_SIZE), pmap5(pg)) for pg in range(n_pages)]
    in_specs += [pl.BlockSpec((1, 1, KV_HEADS, HEAD_DIM, PAGE_SIZE), pmap5(pg)) for pg in range(n_pages)]
    kern = functools.partial(_dsa_sample_kernel, n_pages=n_pages, n_valid=n_valid, top=top)
    nk = (n_pages + 1) * LANE
    return pl.pallas_call(
        kern,
        grid_spec=pltpu.PrefetchScalarGridSpec(
            num_scalar_prefetch=1,
            grid=(n_seq,),
            in_specs=in_specs + [pl.BlockSpec(memory_space=pl.ANY)],
            out_specs=pl.BlockSpec((SEQ_PAD, MIX_W), lambda b, pt: (rb0 + b, 0)),
            scratch_shapes=[pltpu.VMEM((IDX_DIM, nk), BF16),
                            pltpu.VMEM((KV_HEADS, HEAD_DIM, nk), BF16),
                            pltpu.VMEM((KV_HEADS, HEAD_DIM, nk), BF16),
                            pltpu.VMEM((SEQ_PAD, nk), F32)]),
        out_shape=jax.ShapeDtypeStruct(y_buf.shape, BF16),
        input_output_aliases={1 + len(in_specs): 0},
        compiler_params=_cparams(("arbitrary",)),
        name="dsa_sample",
    )(page_table, p, p, p, p, p, kidx_ln, base,
      *([cki] * n_pages), *([ck] * n_pages), *([cv] * n_pages), y_buf)


def _t5_base(t5_table):
    x = jnp.arange(2 * LANE, dtype=jnp.int32)
    rel = jnp.where(x <= LANE, LANE - x, T5_MAX_DIST)
    n = jnp.maximum(rel, 0)
    exact = T5_BUCKETS // 2
    nf = jnp.maximum(n, exact).astype(F32)
    large = exact + (jnp.log(nf / exact) / math.log(T5_MAX_DIST / exact)
                     * (T5_BUCKETS - exact)).astype(jnp.int32)
    bucket = jnp.where(n < exact, n, jnp.minimum(large, T5_BUCKETS - 1))
    return t5_table[bucket].astype(F32).T


def _relayout_w_in(w):
    widths = (M_INNER, CONV_DIM, M_HEADS, ATT_HEADS * HEAD_DIM, KV_HEADS * HEAD_DIM,
              KV_HEADS * HEAD_DIM, IDX_HEADS * IDX_DIM, IDX_DIM, IDX_HEADS, R_PROJ, 3 * D_MODEL)
    offs = [0]
    for wd in widths:
        offs.append(offs[-1] + wd)
    z, xbc, dt, q, k, v, qi, ki, wi, pr, gates = (w[:, offs[i]:offs[i + 1]] for i in range(len(widths)))
    zeros = lambda n: jnp.zeros((w.shape[0], n), w.dtype)
    lora = pr[:, 3 * R_WIDTH:]
    cols = [z, xbc, q, qi, pr[:, :3 * R_WIDTH], gates,
            lora, zeros(LORA_W - lora.shape[1]), k, v,
            ki, dt, wi, zeros(LANE - IDX_DIM - M_HEADS - IDX_HEADS)]
    out = jnp.concatenate(cols, axis=1).astype(BF16)
    assert out.shape[1] == P_W
    return out


def _pad_cols(x, n):
    return jnp.concatenate([x, jnp.zeros(x.shape[:-1] + (n - x.shape[-1],), x.dtype)], axis=-1)


def _shift_layout(x):
    return _pad_cols(x, 3 * R_WIDTH + LORA_W)


def kernel(x_prompt, x_sample, cache_k, cache_v, cache_kidx, page_table, state_ssm, state_conv, state_wkv, state_shift, ffn1_norm, ffn1_gate, ffn1_up, ffn1_down, mix_norm, w_in, conv_w, conv_b, dt_bias, a_log, d_skip, ssm_norm, kidx_ln_g, kidx_ln_b, t5_table, rwkv_mu, rwkv_w0, rwkv_w2, rwkv_a0, rwkv_a2, rwkv_g2, rwkv_kk, rwkv_ka, rwkv_rk, rwkv_gn_g, rwkv_gn_b, w_br_a, w_br_b, w_br_c, w_out, ffn2_norm, ffn2_gate, ffn2_up, ffn2_down, final_norm):
    bp, S, d = x_prompt.shape
    nb, T_dec, _ = x_sample.shape
    depth = w_in.shape[0]
    assert bp == 1 and T_dec <= SEQ_PAD and S % SSD_CHUNK == 0
    n_p = bp * S
    n_rows = n_p + nb * SEQ_PAD
    xs_pad = jnp.concatenate([x_sample, jnp.zeros((nb, SEQ_PAD - T_dec, d), x_sample.dtype)], axis=1)
    h = jnp.concatenate([x_prompt.reshape(n_p, d), xs_pad.reshape(nb * SEQ_PAD, d)], axis=0)

    base = _t5_base(t5_table)
    e16 = jnp.repeat(jnp.eye(M_HEADS, dtype=F32), M_HEAD_DIM, axis=1)
    row = lambda x: x.reshape(1, -1).astype(F32)
    rc = min(RWKV_RC_CHUNK, S)
    shift_all = _shift_layout(state_shift).reshape(depth, nb, 1, 3 * R_WIDTH + LORA_W)
    outs = {k_: [] for k_ in ("k_p", "v_p", "ki_p", "conv_p", "sh_p", "k_s", "v_s", "ki_s", "conv_s", "sh_s")}
    ssm_p = ssm_s = wkv_p = wkv_s = None
    for l in range(depth):
        bf = lambda x: x.astype(BF16)
        h = _ffn(h, row(ffn1_norm[l]), bf(ffn1_gate[l]), bf(ffn1_up[l]), bf(ffn1_down[l]))
        p = _inproj(h, row(mix_norm[l]), _relayout_w_in(w_in[l]))
        kidx = _kidx_ln(p, row(kidx_ln_g[l]), row(kidx_ln_b[l]))

        ssd_par = (conv_w[l], row(conv_b[l]), row(dt_bias[l]), row(a_log[l]),
                   row(jnp.repeat(d_skip[l], M_HEAD_DIM)), row(ssm_norm[l]), e16)
        stack = dict(n_rows=n_rows, depth=depth, out_layer=l)
        ya, ssm_p = _ssd(p, 0, bp, S // SSD_CHUNK, SSD_CHUNK, SSD_CHUNK, 0,
                         jnp.zeros((1, bp, CONV_W - 1, CONV_DIM), F32),
                         jnp.zeros((1, bp, M_HEADS, M_HEAD_DIM, M_STATE), F32), *ssd_par,
                         st_buf=ssm_p, **stack)
        ya, ssm_s = _ssd(p, n_p, nb, 1, SEQ_PAD, T_dec, l, state_conv, state_ssm, *ssd_par,
                         y_buf=ya, st_buf=ssm_s, **stack)

        g2p = jnp.concatenate([rwkv_g2[l], jnp.zeros((LORA_W - R_DECAY_LORA - R_A_LORA - R_G_LORA,
                                                      R_WIDTH), F32)], axis=0)
        rw_par = (row(_shift_layout(rwkv_mu[l])), row(rwkv_w0[l]), rwkv_w2[l], row(rwkv_a0[l]),
                  rwkv_a2[l], g2p, row(rwkv_kk[l]), row(rwkv_ka[l]), row(rwkv_rk[l]),
                  row(rwkv_gn_g[l]), row(rwkv_gn_b[l]))
        sw = 3 * R_WIDTH + LORA_W
        yc, wkv_p = _rwkv(p, 0, bp, S // rc, rc, rc, 0, jnp.zeros((1, bp, 1, sw), F32),
                          jnp.zeros((1, bp, R_HEADS, R_HEAD, R_HEAD), F32), *rw_par,
                          st_buf=wkv_p, **stack)
        yc, wkv_s = _rwkv(p, n_p, nb, 1, SEQ_PAD, T_dec, l, shift_all, state_wkv, *rw_par,
                          y_buf=yc, st_buf=wkv_s, **stack)

        kp = p[:n_p, C_K:C_K + 256]
        vp = p[:n_p, C_V:C_V + 256]
        kit = bf(kidx[:n_p].T)
        kt = bf(kp.reshape(n_p, KV_HEADS, HEAD_DIM).transpose(1, 2, 0))
        vg = vp.reshape(n_p, KV_HEADS, HEAD_DIM).transpose(1, 0, 2)
        va = bf(jnp.concatenate([vg, jnp.ones((KV_HEADS, n_p, 1), F32),
                                 jnp.zeros((KV_HEADS, n_p, LANE - HEAD_DIM - 1), F32)], axis=-1))
        yb = _dsa_prompt(p, kit, kt, va, base, S, n_rows)
        yb = _dsa_sample(p, n_p, nb, kidx, base, l, cache_k, cache_v, cache_kidx, page_table, T_dec, yb)
        h = _merge(h, p, ya, yb, yc, bf(w_br_a[l]), bf(w_br_b[l]), bf(w_br_c[l]), bf(w_out[l]))
        h = _ffn(h, row(ffn2_norm[l]), bf(ffn2_gate[l]), bf(ffn2_up[l]), bf(ffn2_down[l]))

        ps = p[n_p:].reshape(nb, SEQ_PAD, P_W)[:, :T_dec]
        xbc = lambda t: jnp.concatenate([t[..., C_XS:C_XS + 1024], t[..., C_BC:C_BC + 1024]], axis=-1)
        prj = lambda t: jnp.concatenate([t[..., C_RR:C_RR + 3 * R_WIDTH],
                                         t[..., C_LORA:C_LORA + R_PROJ - 3 * R_WIDTH]], axis=-1)
        outs["k_p"].append(kp.reshape(bp, S, KV_HEADS, HEAD_DIM))
        outs["v_p"].append(vp.reshape(bp, S, KV_HEADS, HEAD_DIM))
        outs["ki_p"].append(kidx[:n_p].reshape(bp, S, IDX_DIM))
        outs["conv_p"].append(xbc(p[n_p - (CONV_W - 1):n_p]).reshape(bp, CONV_W - 1, CONV_DIM))
        outs["sh_p"].append(prj(p[n_p - 1:n_p]).reshape(bp, R_PROJ))
        outs["k_s"].append(ps[..., C_K:C_K + 256].reshape(nb, T_dec, KV_HEADS, HEAD_DIM))
        outs["v_s"].append(ps[..., C_V:C_V + 256].reshape(nb, T_dec, KV_HEADS, HEAD_DIM))
        outs["ki_s"].append(kidx[n_p:].reshape(nb, SEQ_PAD, IDX_DIM)[:, :T_dec])
        conv_full = jnp.concatenate([state_conv[l], xbc(ps)], axis=1)
        outs["conv_s"].append(conv_full[:, T_dec:])
        outs["sh_s"].append(prj(ps[:, T_dec - 1]))

    y = _final_norm(h, row(final_norm))
    y_prompt = y[:n_p].reshape(bp, S, d)
    y_sample = y[n_p:].reshape(nb, SEQ_PAD, d)[:, :T_dec]
    stk = lambda name: jnp.stack(outs[name], axis=0)
    return (y_prompt, y_sample, stk("k_p"), stk("v_p"), stk("ki_p"), ssm_p, stk("conv_p"),
            wkv_p, stk("sh_p"), stk("k_s"), stk("v_s"), stk("ki_s"), ssm_s,
            stk("conv_s"), wkv_s, stk("sh_s"))
```

```python
import functools
import math

import jax
import jax.numpy as jnp
from jax import lax
from jax.experimental import pallas as pl
from jax.experimental.pallas import tpu as pltpu

F32 = jnp.float32
BF16 = jnp.bfloat16
I32 = jnp.int32

D_MODEL = 2048
MIX_W = D_MODEL // 2
M_HEAD_DIM = 64
M_INNER = MIX_W
M_HEADS = M_INNER // M_HEAD_DIM
M_GROUPS = 4
M_STATE = 128
CONV_W = 4
CONV_DIM = M_INNER + 2 * M_GROUPS * M_STATE
SSD_CHUNK = 128
HEAD_DIM = 64
ATT_HEADS = MIX_W // HEAD_DIM
KV_HEADS = 4
GQA = ATT_HEADS // KV_HEADS
IDX_HEADS = 16
IDX_DIM = 64
TOPK_MAX = 256
T5_BUCKETS = 32
T5_MAX_DIST = 128
R_HEAD = 64
R_WIDTH = MIX_W
R_HEADS = R_WIDTH // R_HEAD
R_DECAY_LORA = 64
R_A_LORA = 64
R_G_LORA = 160
R_PROJ = 3 * R_WIDTH + R_DECAY_LORA + R_A_LORA + R_G_LORA
GN_EPS = 64e-5
D_FF = 256 * ((8 * D_MODEL // 3 + 255) // 256)
NORM_EPS = 1e-6
PAGE_SIZE = 128

SEQ_PAD = 8
LANE = 128
SUBLANES = 8
MXU_DIM = 256
VMEM_LIMIT = 56 * 2**20
INT_MIN = -2**31
NEG_BIG = -1e30

C_Z, C_XS, C_BC, C_Q, C_QI, C_RR, C_RK, C_RV = (i * 1024 for i in range(8))
C_GATE = 8192
C_LORA = C_GATE + 3 * D_MODEL
LORA_W = 512
C_K = C_LORA + LORA_W
C_V = C_K + 256
C_MISC = C_V + 256
P_W = C_MISC + LANE
MISC_DT = IDX_DIM
MISC_WI = IDX_DIM + M_HEADS
RWKV_RC_CHUNK = 64

HI = lax.Precision.HIGHEST


def _cparams(sem):
    return pltpu.CompilerParams(dimension_semantics=sem, vmem_limit_bytes=VMEM_LIMIT)


def _mm(a, b):
    return jnp.dot(a.astype(BF16), b.astype(BF16), preferred_element_type=F32)


def _mm_nt(a, b):
    return lax.dot_general(a.astype(BF16), b.astype(BF16), (((1,), (1,)), ((), ())),
                           preferred_element_type=F32)


def _mm_tn(a, b):
    return lax.dot_general(a.astype(BF16), b.astype(BF16), (((0,), (0,)), ((), ())),
                           preferred_element_type=F32)


def _mm_hi(a, b):
    return jnp.dot(a, b, precision=HI, preferred_element_type=F32)


def _mm_nt_hi(a, b):
    return lax.dot_general(a, b, (((1,), (1,)), ((), ())), precision=HI, preferred_element_type=F32)


def _mm_tn_hi(a, b):
    return lax.dot_general(a, b, (((0,), (0,)), ((), ())), precision=HI, preferred_element_type=F32)


def _sigmoid(x):
    return jax.nn.sigmoid(x)


def _silu(x):
    return x * jax.nn.sigmoid(x)


def _full(shape):
    nd = len(shape)
    return pl.BlockSpec(shape, lambda *_: (0,) * nd)


def _ffn_kernel(x_ref, g_ref, wg_ref, wu_ref, wd_ref, o_ref, n_ref):
    f = pl.program_id(1)

    @pl.when(f == 0)
    def _():
        x = x_ref[...]
        ms = jnp.mean(x * x, axis=-1, keepdims=True)
        n_ref[...] = (x * lax.rsqrt(ms + NORM_EPS) * g_ref[...]).astype(BF16)
        o_ref[...] = x

    n = n_ref[...]
    a = jnp.dot(n, wg_ref[...], preferred_element_type=F32)
    b = jnp.dot(n, wu_ref[...], preferred_element_type=F32)
    hid = (0.5 * _silu(a) * b).astype(BF16)
    o_ref[...] += jnp.dot(hid, wd_ref[...], preferred_element_type=F32)


def _ffn(x, g, wg, wu, wd, tm=512, tf=512):
    m = x.shape[0]
    tm = min(tm, m)
    return pl.pallas_call(
        _ffn_kernel,
        grid=(m // tm, D_FF // tf),
        in_specs=[pl.BlockSpec((tm, D_MODEL), lambda i, f: (i, 0)),
                  pl.BlockSpec((1, D_MODEL), lambda i, f: (0, 0)),
                  pl.BlockSpec((D_MODEL, tf), lambda i, f: (0, f)),
                  pl.BlockSpec((D_MODEL, tf), lambda i, f: (0, f)),
                  pl.BlockSpec((tf, D_MODEL), lambda i, f: (f, 0))],
        out_specs=pl.BlockSpec((tm, D_MODEL), lambda i, f: (i, 0)),
        out_shape=jax.ShapeDtypeStruct((m, D_MODEL), F32),
        scratch_shapes=[pltpu.VMEM((tm, D_MODEL), BF16)],
        compiler_params=_cparams(("parallel", "arbitrary")),
        name="ffn",
    )(x, g, wg, wu, wd)


def _inproj_kernel(x_ref, g_ref, w_ref, o_ref, n_ref):
    j = pl.program_id(1)

    @pl.when(j == 0)
    def _():
        x = x_ref[...]
        ms = jnp.mean(x * x, axis=-1, keepdims=True)
        n_ref[...] = (x * lax.rsqrt(ms + NORM_EPS) * g_ref[...]).astype(BF16)

    o_ref[...] = jnp.dot(n_ref[...], w_ref[...], preferred_element_type=F32)


def _inproj(x, g, w, tm=1024, tn=1408):
    m = x.shape[0]
    tm = tm if m % tm == 0 else min(512, m)
    return pl.pallas_call(
        _inproj_kernel,
        grid=(m // tm, P_W // tn),
        in_specs=[pl.BlockSpec((tm, D_MODEL), lambda i, j: (i, 0)),
                  pl.BlockSpec((1, D_MODEL), lambda i, j: (0, 0)),
                  pl.BlockSpec((D_MODEL, tn), lambda i, j: (0, j))],
        out_specs=pl.BlockSpec((tm, tn), lambda i, j: (i, j)),
        out_shape=jax.ShapeDtypeStruct((m, P_W), F32),
        scratch_shapes=[pltpu.VMEM((tm, D_MODEL), BF16)],
        compiler_params=_cparams(("parallel", "arbitrary")),
        name="inproj",
    )(x, g, w)


def _merge_kernel(h_ref, ya_ref, yb_ref, yc_ref, ga_ref, gb_ref, gc_ref,
                  wa_ref, wb_ref, wc_ref, wo_ref, o_ref):
    j = pl.program_id(1)

    @pl.when(j == 0)
    def _():
        o_ref[...] = h_ref[...]

    merged = (_sigmoid(ga_ref[...]) * jnp.dot(ya_ref[...], wa_ref[...], preferred_element_type=F32)
              + _sigmoid(gb_ref[...]) * jnp.dot(yb_ref[...], wb_ref[...], preferred_element_type=F32)
              + _sigmoid(gc_ref[...]) * jnp.dot(yc_ref[...], wc_ref[...], preferred_element_type=F32))
    o_ref[...] += jnp.dot(merged.astype(BF16), wo_ref[...], preferred_element_type=F32)


def _merge(h, p, ya, yb, yc, wa, wb, wc, wo, tm=512, tk=512):
    m = h.shape[0]
    tm = min(tm, m)
    gb0 = C_GATE // tk
    nk = D_MODEL // tk
    yspec = pl.BlockSpec((tm, MIX_W), lambda i, j: (i, 0))
    wspec = pl.BlockSpec((MIX_W, tk), lambda i, j: (0, j))
    return pl.pallas_call(
        _merge_kernel,
        grid=(m // tm, nk),
        in_specs=[pl.BlockSpec((tm, D_MODEL), lambda i, j: (i, 0)), yspec, yspec, yspec,
                  pl.BlockSpec((tm, tk), lambda i, j: (i, gb0 + j)),
                  pl.BlockSpec((tm, tk), lambda i, j: (i, gb0 + nk + j)),
                  pl.BlockSpec((tm, tk), lambda i, j: (i, gb0 + 2 * nk + j)),
                  wspec, wspec, wspec,
                  pl.BlockSpec((tk, D_MODEL), lambda i, j: (j, 0))],
        out_specs=pl.BlockSpec((tm, D_MODEL), lambda i, j: (i, 0)),
        out_shape=jax.ShapeDtypeStruct((m, D_MODEL), F32),
        compiler_params=_cparams(("parallel", "arbitrary")),
        name="merge",
    )(h, ya, yb, yc, p, p, p, wa, wb, wc, wo)


def _final_norm_kernel(x_ref, g_ref, o_ref):
    x = x_ref[...]
    ms = jnp.mean(x * x, axis=-1, keepdims=True)
    o_ref[...] = x * lax.rsqrt(ms + NORM_EPS) * g_ref[...]


def _final_norm(x, g, tm=512):
    m = x.shape[0]
    tm = min(tm, m)
    return pl.pallas_call(
        _final_norm_kernel,
        grid=(m // tm,),
        in_specs=[pl.BlockSpec((tm, D_MODEL), lambda i: (i, 0)), _full((1, D_MODEL))],
        out_specs=pl.BlockSpec((tm, D_MODEL), lambda i: (i, 0)),
        out_shape=jax.ShapeDtypeStruct((m, D_MODEL), F32),
        compiler_params=_cparams(("parallel",)),
        name="final_norm",
    )(x, g)


def _kidx_ln_kernel(m_ref, g_ref, b_ref, o_ref):
    x = m_ref[:, 0:IDX_DIM]
    mu = jnp.mean(x, axis=-1, keepdims=True)
    var = jnp.mean(jnp.square(x - mu), axis=-1, keepdims=True)
    o_ref[...] = (x - mu) * lax.rsqrt(var + NORM_EPS) * g_ref[...] + b_ref[...]


def _kidx_ln(p, g, b, tm=512):
    m = p.shape[0]
    tm = min(tm, m)
    return pl.pallas_call(
        _kidx_ln_kernel,
        grid=(m // tm,),
        in_specs=[pl.BlockSpec((tm, LANE), lambda i: (i, C_MISC // LANE)),
                  _full((1, IDX_DIM)), _full((1, IDX_DIM))],
        out_specs=pl.BlockSpec((tm, IDX_DIM), lambda i: (i, 0)),
        out_shape=jax.ShapeDtypeStruct((m, IDX_DIM), F32),
        compiler_params=_cparams(("parallel",)),
        name="kidx_ln",
    )(p, g, b)


def _shifted(x, k, prev, rows):
    y = pltpu.roll(x, k, 0)
    for r in range(k):
        y = jnp.where(rows == r, prev[3 - k + r:4 - k + r, :], y)
    return y


def _ssd_kernel(*refs, T, n_valid, n_alias):
    (z_ref, xs_ref, bc_ref, misc_ref, cst_ref, st0_ref, cw_ref, cb_ref, dtb_ref,
     alog_ref, dexp_ref, norm_ref, e16_ref) = refs[:13]
    y_ref, stout_ref, prev_ref, st_ref, yd_ref = refs[13 + n_alias:]
    c = pl.program_id(1)
    nc = pl.num_programs(1)

    @pl.when(c == 0)
    def _():
        prev_ref[...] = cst_ref[0, 0]
        st_ref[...] = st0_ref[0, 0]

    rows = lax.broadcasted_iota(I32, (T, 1), 0)
    cw = cw_ref[...]
    prev = prev_ref[...]

    def conv(x, lo, hi):
        pv = prev[:, lo:hi]
        y = cb_ref[:, lo:hi] + x * cw[3:4, lo:hi]
        for k in (1, 2, 3):
            y = y + _shifted(x, k, pv, rows) * cw[3 - k:4 - k, lo:hi]
        return _silu(y)

    xs_raw = xs_ref[...]
    bc_raw = bc_ref[...]
    xs = conv(xs_raw, 0, M_INNER)
    bc = conv(bc_raw, M_INNER, CONV_DIM)
    prev_ref[:, 0:M_INNER] = xs_raw[T - 3:T, :]
    prev_ref[:, M_INNER:CONV_DIM] = bc_raw[T - 3:T, :]

    dt = jax.nn.softplus(misc_ref[:, MISC_DT:MISC_DT + M_HEADS] + dtb_ref[...])
    if n_valid < T:
        dt = jnp.where(rows < n_valid, dt, 0.0)
    ad = dt * (-jnp.exp(alog_ref[...]))
    ri = lax.broadcasted_iota(I32, (T, T), 0)
    ci = lax.broadcasted_iota(I32, (T, T), 1)
    lower = ri >= ci
    tril = lower.astype(F32)
    triu = (ri <= ci).astype(F32)
    eye16 = (lax.broadcasted_iota(I32, (M_HEADS, M_HEADS), 0)
             == lax.broadcasted_iota(I32, (M_HEADS, M_HEADS), 1)).astype(F32)
    cs = _mm_hi(tril, ad)
    cst = _mm_hi(_mm_nt_hi(eye16, ad), triu)
    e16 = e16_ref[...]
    dt_e = _mm_hi(dt, e16)
    ecs_e = _mm_hi(jnp.exp(cs), e16)
    wl_e = _mm_hi(jnp.exp(cs[T - 1:T, :] - cs), e16)
    xd = xs * dt_e
    xdw = (xd * wl_e).astype(BF16)
    xd = xd.astype(BF16)
    bcb = bc.astype(BF16)
    for g in range(M_GROUPS):
        bg = bcb[:, g * M_STATE:(g + 1) * M_STATE]
        cg = bcb[:, (M_GROUPS + g) * M_STATE:(M_GROUPS + g + 1) * M_STATE]
        cbm = _mm_nt(cg, bg)
        for hh in range(M_HEADS // M_GROUPS):
            h = g * (M_HEADS // M_GROUPS) + hh
            hs = slice(h * M_HEAD_DIM, (h + 1) * M_HEAD_DIM)
            diff = cs[:, h:h + 1] - cst[h:h + 1, :]
            lm = jnp.exp(jnp.where(lower, diff, -jnp.inf))
            yd = _mm(cbm * lm, xd[:, hs])
            st = st_ref[h]
            yo = _mm_nt(cg, st)
            yd_ref[:, hs] = yd + yo * ecs_e[:, hs]
            upd = _mm_tn(xdw[:, hs], bg)
            dec = jnp.exp(cst[h:h + 1, T - 1:T])
            st_ref[h] = st * dec + upd

    y = yd_ref[...] + xs * dexp_ref[...]
    y = y * _silu(z_ref[...])
    gw = M_INNER // M_GROUPS
    for g in range(M_GROUPS):
        yg = y[:, g * gw:(g + 1) * gw]
        ms = jnp.mean(yg * yg, axis=-1, keepdims=True)
        y_ref[:, g * gw:(g + 1) * gw] = (yg * lax.rsqrt(ms + NORM_EPS)
                                         * norm_ref[:, g * gw:(g + 1) * gw]).astype(y_ref.dtype)

    @pl.when(c == nc - 1)
    def _():
        stout_ref[0, 0] = st_ref[...]


def _alias_inputs(n_in, bufs):
    specs, args, aliases = [], [], {}
    for k, b in enumerate(bufs):
        if b is not None:
            aliases[n_in + len(args)] = k
            specs.append(pl.BlockSpec(memory_space=pl.ANY))
            args.append(b)
    return specs, args, aliases


def _ssd(p, row0, n_seq, n_chunks, T, n_valid, layer, conv_state, ssm0, cw, cb, dtb, alog, dexp, norm,
         e16, *, n_rows, depth, out_layer, y_buf=None, st_buf=None):
    rb0 = row0 // T

    def rmap(col):
        return lambda s, c: (rb0 + s * n_chunks + c, col)

    args = [p, p, p, p, conv_state, ssm0, cw, cb, dtb, alog, dexp, norm, e16]
    a_specs, a_args, aliases = _alias_inputs(len(args), (y_buf, st_buf))
    kern = functools.partial(_ssd_kernel, T=T, n_valid=n_valid, n_alias=len(a_args))
    return pl.pallas_call(
        kern,
        grid=(n_seq, n_chunks),
        in_specs=[pl.BlockSpec((T, 1024), rmap(C_Z // 1024)),
                  pl.BlockSpec((T, 1024), rmap(C_XS // 1024)),
                  pl.BlockSpec((T, 1024), rmap(C_BC // 1024)),
                  pl.BlockSpec((T, LANE), rmap(C_MISC // LANE)),
                  pl.BlockSpec((1, 1, CONV_W - 1, CONV_DIM), lambda s, c: (layer, s, 0, 0)),
                  pl.BlockSpec((1, 1, M_HEADS, M_HEAD_DIM, M_STATE), lambda s, c: (layer, s, 0, 0, 0)),
                  _full((CONV_W, CONV_DIM)), _full((1, CONV_DIM)), _full((1, M_HEADS)),
                  _full((1, M_HEADS)), _full((1, M_INNER)), _full((1, M_INNER)),
                  _full((M_HEADS, M_INNER))] + a_specs,
        out_specs=[pl.BlockSpec((T, M_INNER), rmap(0)),
                   pl.BlockSpec((1, 1, M_HEADS, M_HEAD_DIM, M_STATE), lambda s, c: (out_layer, s, 0, 0, 0))],
        out_shape=[jax.ShapeDtypeStruct((n_rows, M_INNER), BF16),
                   jax.ShapeDtypeStruct((depth, n_seq, M_HEADS, M_HEAD_DIM, M_STATE), F32)],
        scratch_shapes=[pltpu.VMEM((CONV_W - 1, CONV_DIM), F32),
                        pltpu.VMEM((M_HEADS, M_HEAD_DIM, M_STATE), F32),
                        pltpu.VMEM((T, M_INNER), F32)],
        input_output_aliases=aliases,
        compiler_params=_cparams(("parallel", "arbitrary")),
        name="ssd",
    )(*args, *a_args)


def _rwkv_kernel(*refs, T, n_valid, HG, n_alias):
    (r_ref, k_ref, v_ref, lo_ref, sh0_ref, s0_ref, mu_ref, w0_ref, w2_ref, a0_ref,
     a2_ref, g2_ref, kkw_ref, kaw_ref, rk_ref, gng_ref, gnb_ref) = refs[:17]
    (y_ref, sout_ref, last_ref, s_ref, yb_ref, xs_ref, hr_ref,
     gm_ref) = refs[17 + n_alias:]
    c = pl.program_id(1)
    nc = pl.num_programs(1)

    @pl.when(c == 0)
    def _():
        last_ref[...] = sh0_ref[0, 0]
        s_ref[...] = s0_ref[0, 0]

    rows = lax.broadcasted_iota(I32, (T, 1), 0)

    def mix(ref, lo, hi):
        x = ref[...]
        prev = jnp.where(rows == 0, last_ref[:, lo:hi], pltpu.roll(x, 1, 0))
        last_ref[:, lo:hi] = x[T - 1:T, :]
        return x + (prev - x) * mu_ref[:, lo:hi]

    r = mix(r_ref, 0, 1024)
    k = mix(k_ref, 1024, 2048)
    v = mix(v_ref, 2048, 3072)
    lo = mix(lo_ref, 3072, 3072 + LORA_W)
    xw = lo[:, 0:R_DECAY_LORA]
    xa = lo[:, R_DECAY_LORA:R_DECAY_LORA + R_A_LORA]
    xg = lo[:, R_DECAY_LORA + R_A_LORA:LORA_W]
    wl = -jax.nn.softplus(-(w0_ref[...] + _mm_hi(jnp.tanh(xw), w2_ref[...]))) - 0.5
    ld = -jnp.exp(wl)
    a = _sigmoid(a0_ref[...] + _mm_hi(xa, a2_ref[...]))
    gate = _mm_hi(_sigmoid(xg), g2_ref[...])
    kk = k * kkw_ref[...]
    km = k * (1.0 + (a - 1.0) * kaw_ref[...])
    if n_valid < T:
        ok = rows < n_valid
        ld = jnp.where(ok, ld, 0.0)
        kk = jnp.where(ok, kk, 0.0)
        km = jnp.where(ok, km, 0.0)
        v = jnp.where(ok, v, 0.0)

    ri = lax.broadcasted_iota(I32, (T, T), 0)
    ci = lax.broadcasted_iota(I32, (T, T), 1)
    cum = _mm_hi((ri >= ci).astype(F32), ld)
    eg = jnp.exp(cum)
    egm = jnp.exp(cum - ld)
    ei = jnp.exp(-cum)
    el = jnp.exp(cum[T - 1:T, :] - cum)
    r_eg = r * eg
    k_ei = km * ei
    k_el = km * el
    rk_sum = r * km * rk_ref[...]
    gng = gng_ref[...]
    gnb = gnb_ref[...]

    R = HG * T
    GW = HG * R_HEAD
    rr = lax.broadcasted_iota(I32, (R, R), 0)
    cc = lax.broadcasted_iota(I32, (R, R), 1)
    same = (rr // T) == (cc // T)
    strict = jnp.logical_and(same, rr > cc)
    incl = jnp.logical_and(same, rr >= cc)
    head_of = (lax.broadcasted_iota(I32, (R, GW), 0) // T) == (lax.broadcasted_iota(I32, (R, GW), 1) // R_HEAD)
    blk_r = (rr % T) // SUBLANES
    blk_c = (cc % T) // SUBLANES

    n_groups = R_HEADS // HG

    def head_slices(gi):
        return [slice((gi * HG + hh) * R_HEAD, (gi * HG + hh + 1) * R_HEAD) for hh in range(HG)]

    def stack(x, gi):
        return jnp.concatenate([x[:, s_] for s_ in head_slices(gi)], axis=0)

    def unit_kk(gi):
        kks = stack(kk, gi)
        nrm = jnp.sqrt(jnp.sum(kks * kks, axis=-1, keepdims=True))
        return kks / jnp.maximum(nrm, 1e-12)

    for gi in range(n_groups):
        kks = unit_kk(gi)
        q2 = jnp.concatenate([kks * stack(egm, gi), stack(r_eg, gi)], axis=0)
        k2 = jnp.concatenate([stack(k_ei, gi), kks * stack(a, gi) * stack(ei, gi)], axis=0)
        gm = _mm_nt(q2, k2)
        gm_ref[gi] = gm
        scat = s_ref[gi * HG:(gi + 1) * HG].reshape(GW, R_HEAD)
        hmw = _mm_nt(q2, scat)
        hk = jnp.concatenate([hmw[hh * T:(hh + 1) * T, hh * R_HEAD:(hh + 1) * R_HEAD]
                              for hh in range(HG)], axis=0)
        hr_ref[gi] = jnp.concatenate([hmw[R + hh * T:R + (hh + 1) * T, hh * R_HEAD:(hh + 1) * R_HEAD]
                                      for hh in range(HG)], axis=0)
        xs_ref[gi] = hk + _mm(jnp.where(strict, gm[0:R, 0:R], 0.0), stack(v, gi))

    for jb in range(T // SUBLANES):
        for gi in range(n_groups):
            for hh in range(HG):
                r0 = hh * T + jb * SUBLANES
                xb = xs_ref[gi, r0:r0 + SUBLANES, :]
                lrow = gm_ref[gi, r0:r0 + SUBLANES, R:2 * R]
                for s in range(1, SUBLANES):
                    below = lax.broadcasted_iota(I32, (SUBLANES, 1), 0) >= s
                    xb = xb - jnp.where(below, lrow[:, r0 + s - 1:r0 + s], 0.0) * xb[s - 1:s, :]
                xs_ref[gi, r0:r0 + SUBLANES, :] = xb
            if jb + 1 < T // SUBLANES:
                later = jnp.where(jnp.logical_and(jnp.logical_and(same, blk_c == jb), blk_r > jb),
                                  gm_ref[gi, 0:R, R:2 * R], 0.0)
                xs_ref[gi] = xs_ref[gi] - _mm(later, xs_ref[gi])

    for gi in range(n_groups):
        hsl = head_slices(gi)
        kks = unit_kk(gi)
        bs = kks * stack(a, gi)
        vs = stack(v, gi)
        u = -xs_ref[gi]
        ak = jnp.where(incl, gm_ref[gi, R:2 * R, 0:R], 0.0)
        ab = jnp.where(incl, gm_ref[gi, R:2 * R, R:2 * R], 0.0)
        scat = s_ref[gi * HG:(gi + 1) * HG].reshape(GW, R_HEAD)
        y = hr_ref[gi] + _mm(jnp.concatenate([ak, ab], axis=1), jnp.concatenate([vs, u], axis=0))
        vw = jnp.where(head_of, jnp.concatenate([v[:, gi * GW:(gi + 1) * GW]] * HG, axis=0), 0.0)
        uw = jnp.where(head_of, jnp.concatenate([u] * HG, axis=1), 0.0)
        upd = _mm_tn(jnp.concatenate([vw, uw], axis=0),
                     jnp.concatenate([stack(k_el, gi), bs * stack(el, gi)], axis=0))
        g_last = jnp.concatenate([jnp.broadcast_to(eg[T - 1:T, s_], (R_HEAD, R_HEAD)) for s_ in hsl],
                                 axis=0)
        s_ref[gi * HG:(gi + 1) * HG] = (scat * g_last + upd).reshape(HG, R_HEAD, R_HEAD)
        mu = jnp.mean(y, axis=-1, keepdims=True)
        var = jnp.mean(jnp.square(y - mu), axis=-1, keepdims=True)
        gs = jnp.concatenate([jnp.broadcast_to(gng[:, s_], (T, R_HEAD)) for s_ in hsl], axis=0)
        gb = jnp.concatenate([jnp.broadcast_to(gnb[:, s_], (T, R_HEAD)) for s_ in hsl], axis=0)
        yn = (y - mu) * lax.rsqrt(var + GN_EPS) * gs + gb
        yn = yn + jnp.sum(stack(rk_sum, gi), axis=-1, keepdims=True) * vs
        for hh in range(HG):
            yb_ref[:, hsl[hh]] = yn[hh * T:(hh + 1) * T]

    y_ref[...] = (yb_ref[...] * gate).astype(y_ref.dtype)

    @pl.when(c == nc - 1)
    def _():
        sout_ref[0, 0] = s_ref[...]


def _rwkv(p, row0, n_seq, n_chunks, T, n_valid, layer, shift0, s0, mu, w0, w2, a0, a2, g2, kkw, kaw,
          rk, gng, gnb, *, n_rows, depth, out_layer, y_buf=None, st_buf=None):
    rb0 = row0 // T

    def rmap(col):
        return lambda s, c: (rb0 + s * n_chunks + c, col)

    sw = 3072 + LORA_W
    hg = min(R_HEADS, max(1, MXU_DIM // T))
    args = [p, p, p, p, shift0, s0, mu, w0, w2, a0, a2, g2, kkw, kaw, rk, gng, gnb]
    a_specs, a_args, aliases = _alias_inputs(len(args), (y_buf, st_buf))
    kern = functools.partial(_rwkv_kernel, T=T, n_valid=n_valid, HG=hg, n_alias=len(a_args))
    return pl.pallas_call(
        kern,
        grid=(n_seq, n_chunks),
        in_specs=[pl.BlockSpec((T, 1024), rmap(C_RR // 1024)),
                  pl.BlockSpec((T, 1024), rmap(C_RK // 1024)),
                  pl.BlockSpec((T, 1024), rmap(C_RV // 1024)),
                  pl.BlockSpec((T, LORA_W), rmap(C_LORA // LORA_W)),
                  pl.BlockSpec((1, 1, 1, sw), lambda s, c: (layer, s, 0, 0)),
                  pl.BlockSpec((1, 1, R_HEADS, R_HEAD, R_HEAD), lambda s, c: (layer, s, 0, 0, 0)),
                  _full((1, sw)), _full((1, R_WIDTH)), _full((R_DECAY_LORA, R_WIDTH)),
                  _full((1, R_WIDTH)), _full((R_A_LORA, R_WIDTH)),
                  _full((LORA_W - R_DECAY_LORA - R_A_LORA, R_WIDTH)),
                  _full((1, R_WIDTH)), _full((1, R_WIDTH)), _full((1, R_WIDTH)),
                  _full((1, R_WIDTH)), _full((1, R_WIDTH))] + a_specs,
        out_specs=[pl.BlockSpec((T, R_WIDTH), rmap(0)),
                   pl.BlockSpec((1, 1, R_HEADS, R_HEAD, R_HEAD), lambda s, c: (out_layer, s, 0, 0, 0))],
        out_shape=[jax.ShapeDtypeStruct((n_rows, R_WIDTH), BF16),
                   jax.ShapeDtypeStruct((depth, n_seq, R_HEADS, R_HEAD, R_HEAD), F32)],
        scratch_shapes=[pltpu.VMEM((1, sw), F32),
                        pltpu.VMEM((R_HEADS, R_HEAD, R_HEAD), F32),
                        pltpu.VMEM((T, R_WIDTH), F32),
                        pltpu.VMEM((R_HEADS // hg, hg * T, R_HEAD), F32),
                        pltpu.VMEM((R_HEADS // hg, hg * T, R_HEAD), F32),
                        pltpu.VMEM((R_HEADS // hg, 2 * hg * T, 2 * hg * T), F32)],
        input_output_aliases=aliases,
        compiler_params=_cparams(("parallel", "arbitrary")),
        name="rwkv",
    )(*args, *a_args)


def _score_keys(acc, causal):
    bits = pltpu.bitcast(acc, I32)
    key = bits ^ ((bits >> 31) & 0x7FFFFFFF)
    return jnp.where(causal, key, INT_MIN)


def _kth_threshold(count_ge, shape, top, bits_per_step=1, halves=None):
    tb = jnp.zeros(shape, I32)
    for shift in range(32 - bits_per_step, -1, -bits_per_step):
        if halves is not None and shift == 16 - bits_per_step:
            halves[1]((tb ^ jnp.int32(INT_MIN)) >> 16)
        digit = jnp.zeros(shape, I32)
        for d in range(1, 1 << bits_per_step):
            inc = d << shift
            inc = inc - (1 << 32) if inc >= (1 << 31) else inc
            cand = (tb | jnp.int32(inc)) ^ jnp.int32(INT_MIN)
            if halves is None:
                cnt = count_ge(cand)
            elif shift >= 16:
                cnt = halves[0](cand >> 16)
            else:
                cnt = halves[2](cand & 0xFFFF)
            digit = digit + jnp.where(cnt >= top, 1, 0)
        tb = tb | (digit << shift)
    return jnp.maximum(tb ^ jnp.int32(INT_MIN), INT_MIN + 1)


def _toeplitz_bias(base_row, nrows):
    return pltpu.roll(jnp.broadcast_to(base_row, (nrows, 2 * LANE)), 0, 1, stride=1, stride_axis=0)


def _dsa_prompt_kernel(q_ref, qi_ref, misc_ref, kit_ref, kt_ref, va_ref, base_ref, o_ref,
                       key_ref, mx_ref, acc_ref, near_ref, hi_ref, *, top, TA):
    i = pl.program_id(0)
    QB = LANE
    q0 = i * QB
    n_a = (q0 + QB + TA - 1) // TA
    wi = misc_ref[:, MISC_WI:MISC_WI + IDX_HEADS] * (IDX_HEADS ** -0.5 * IDX_DIM ** -0.5)
    qi = qi_ref[...]
    qis = jnp.concatenate([qi[:, h * IDX_DIM:(h + 1) * IDX_DIM] for h in range(IDX_HEADS)],
                          axis=0).astype(BF16)
    rowq = q0 + lax.broadcasted_iota(I32, (QB, 1), 0)

    def score_tile(kt, carry):
        off = pl.multiple_of(kt * TA, TA)
        s = jnp.maximum(jnp.dot(qis, kit_ref[:, pl.ds(off, TA)], preferred_element_type=F32), 0.0)
        acc = s[0:QB] * wi[:, 0:1]
        for h in range(1, IDX_HEADS):
            acc = acc + s[h * QB:(h + 1) * QB] * wi[:, h:h + 1]
        col = off + lax.broadcasted_iota(I32, (1, TA), 1)
        key = _score_keys(acc, col <= rowq)
        key_ref[:, pl.ds(off, TA)] = key
        hi_ref[:, pl.ds(off, TA)] = (key >> 16).astype(jnp.int16)
        return carry

    lax.fori_loop(0, n_a, score_tile, 0)

    def count_ge(cand):
        def body(kt, cnt):
            off = pl.multiple_of(kt * TA, TA)
            ge = jnp.where(key_ref[:, pl.ds(off, TA)] >= cand, 1.0, 0.0)
            for j in range(TA // LANE):
                cnt = cnt + ge[:, j * LANE:(j + 1) * LANE]
            return cnt
        cnt = lax.fori_loop(0, n_a, body, jnp.zeros((QB, LANE), F32))
        return jnp.sum(cnt, axis=-1, keepdims=True)

    def count_half(cand16):
        c16 = cand16.astype(jnp.int16)

        def body(kt, cnt):
            off = pl.multiple_of(kt * TA, TA)
            ge = jnp.where(hi_ref[:, pl.ds(off, TA)] >= c16, jnp.int16(1), jnp.int16(0))
            for j in range(TA // LANE):
                cnt = cnt + ge[:, j * LANE:(j + 1) * LANE]
            return cnt
        cnt = lax.fori_loop(0, n_a, body, jnp.zeros((QB, LANE), jnp.int16))
        return jnp.sum(cnt.astype(F32), axis=-1, keepdims=True)

    def prepare_lo(thr_hi):
        t16 = thr_hi.astype(jnp.int16)

        def body(kt, carry):
            off = pl.multiple_of(kt * TA, TA)
            hi = hi_ref[:, pl.ds(off, TA)]
            lo = ((key_ref[:, pl.ds(off, TA)] & 0xFFFF) - 32768).astype(jnp.int16)
            hi_ref[:, pl.ds(off, TA)] = jnp.where(hi > t16, jnp.int16(32767),
                                                  jnp.where(hi < t16, jnp.int16(-32768), lo))
            return carry
        lax.fori_loop(0, n_a, body, 0)

    thr = _kth_threshold(count_ge, (QB, 1), float(top),
                         halves=(count_half, prepare_lo, lambda l: count_half(l - 32768)))

    n_ge = count_ge(thr)

    @pl.when(jnp.max(n_ge) > top)
    def _():
        def count_tied_before(pos):
            def body(kt, cnt):
                off = pl.multiple_of(kt * TA, TA)
                col = off + lax.broadcasted_iota(I32, (1, TA), 1)
                hit = jnp.where(key_ref[:, pl.ds(off, TA)] == thr, jnp.where(col < pos, 1.0, 0.0), 0.0)
                for j in range(TA // LANE):
                    cnt = cnt + hit[:, j * LANE:(j + 1) * LANE]
                return cnt
            cnt = lax.fori_loop(0, n_a, body, jnp.zeros((QB, LANE), F32))
            return jnp.sum(cnt, axis=-1, keepdims=True)

        need = top - count_ge(thr + 1)
        last = jnp.zeros((QB, 1), I32)
        for bit in range(max(1, (kit_ref.shape[1] - 1).bit_length()) - 1, -1, -1):
            cand = last | jnp.int32(1 << bit)
            last = jnp.where(count_tied_before(cand) < need, cand, last)

        def drop(kt, carry):
            off = pl.multiple_of(kt * TA, TA)
            col = off + lax.broadcasted_iota(I32, (1, TA), 1)
            k = key_ref[:, pl.ds(off, TA)]
            key_ref[:, pl.ds(off, TA)] = jnp.where(k == thr, jnp.where(col > last, INT_MIN, k), k)
            return carry

        lax.fori_loop(0, n_a, drop, 0)

    @pl.when(i == 0)
    def _():
        for g in range(KV_HEADS):
            near_ref[g] = jnp.concatenate(
                [_toeplitz_bias(base_ref[h:h + 1, :], QB) - base_ref[h:h + 1, 2 * LANE - 1:2 * LANE]
                 for h in range(g * GQA, (g + 1) * GQA)], axis=0)

    q = q_ref[...]
    lim = (i - 1) * QB
    n_far = (jnp.maximum(lim, 0) + TA - 1) // TA
    qgs = []
    for g in range(KV_HEADS):
        heads = range(g * GQA, (g + 1) * GQA)
        qgs.append((jnp.concatenate([q[:, h * HEAD_DIM:(h + 1) * HEAD_DIM] for h in heads], axis=0)
                    * HEAD_DIM ** -0.5).astype(BF16))

    def far_logits(g, off, msk):
        s = jnp.dot(qgs[g], kt_ref[g, :, pl.ds(off, TA)], preferred_element_type=F32)
        return (s.reshape(GQA, QB, TA) + msk[None]).reshape(GQA * QB, TA)

    def far_mask(off):
        col = off + lax.broadcasted_iota(I32, (1, TA), 1)
        return jnp.where(key_ref[:, pl.ds(off, TA)] >= thr,
                         jnp.where(col < lim, 0.0, NEG_BIG), NEG_BIG)

    def near_logits(g, off, lo, width, msk):
        s = (jnp.dot(qgs[g], kt_ref[g, :, pl.ds(off, width)], preferred_element_type=F32)
             + near_ref[g, :, lo:lo + width])
        return (s.reshape(GQA, QB, width) + msk[None]).reshape(GQA * QB, width)

    def near_mask(off, width):
        return jnp.where(key_ref[:, pl.ds(off, width)] >= thr, 0.0, NEG_BIG)

    mx_ref[...] = jnp.full(mx_ref.shape, NEG_BIG, F32)
    acc_ref[...] = jnp.zeros(acc_ref.shape, F32)

    def accumulate(g, s, v_tile):
        width = s.shape[1]
        m = s[:, 0:LANE]
        for j in range(1, width // LANE):
            m = jnp.maximum(m, s[:, j * LANE:(j + 1) * LANE])
        m_old = mx_ref[g]
        m_new = jnp.maximum(m_old, jnp.broadcast_to(jnp.max(m, axis=-1, keepdims=True), m.shape))
        pr = jnp.exp(s - jnp.concatenate([m_new] * (width // LANE), axis=1))
        acc_ref[g] = (acc_ref[g] * jnp.exp(m_old - m_new)
                      + jnp.dot(pr.astype(BF16), v_tile, preferred_element_type=F32))
        mx_ref[g] = m_new

    def far_acc(kt, carry):
        off = pl.multiple_of(kt * TA, TA)
        msk = far_mask(off)
        for g in range(KV_HEADS):
            accumulate(g, far_logits(g, off, msk), va_ref[g, pl.ds(off, TA), :])
        return carry

    def near_acc(kt, lo, width):
        off = pl.multiple_of(kt * LANE, LANE)
        msk = near_mask(off, width)
        for g in range(KV_HEADS):
            accumulate(g, near_logits(g, off, lo, width, msk), va_ref[g, pl.ds(off, width), :])

    lax.fori_loop(0, n_far, far_acc, 0)

    @pl.when(i == 0)
    def _():
        near_acc(0, LANE, LANE)

    @pl.when(i >= 1)
    def _():
        near_acc(i - 1, 0, 2 * LANE)
    for g in range(KV_HEADS):
        acc = acc_ref[g]
        out = acc[:, 0:HEAD_DIM] / acc[:, HEAD_DIM:HEAD_DIM + 1]
        for hh in range(GQA):
            h = g * GQA + hh
            o_ref[:, h * HEAD_DIM:(h + 1) * HEAD_DIM] = out[hh * QB:(hh + 1) * QB].astype(o_ref.dtype)


def _dsa_prompt(p, kit, kt, va, base, S, n_rows):
    top = min(TOPK_MAX, S // 4)
    TA = min(512, S)
    kern = functools.partial(_dsa_prompt_kernel, top=top, TA=TA)
    return pl.pallas_call(
        kern,
        grid=(S // LANE,),
        in_specs=[pl.BlockSpec((LANE, 1024), lambda i: (i, C_Q // 1024)),
                  pl.BlockSpec((LANE, 1024), lambda i: (i, C_QI // 1024)),
                  pl.BlockSpec((LANE, LANE), lambda i: (i, C_MISC // LANE)),
                  _full((IDX_DIM, S)), _full((KV_HEADS, HEAD_DIM, S)), _full((KV_HEADS, S, LANE)),
                  _full((ATT_HEADS, 2 * LANE))],
        out_specs=pl.BlockSpec((LANE, MIX_W), lambda i: (i, 0)),
        out_shape=jax.ShapeDtypeStruct((n_rows, MIX_W), BF16),
        scratch_shapes=[pltpu.VMEM((LANE, S), I32),
                        pltpu.VMEM((KV_HEADS, GQA * LANE, LANE), F32),
                        pltpu.VMEM((KV_HEADS, GQA * LANE, LANE), F32),
                        pltpu.VMEM((KV_HEADS, GQA * LANE, 2 * LANE), F32),
                        pltpu.VMEM((LANE, S), jnp.int16)],
        compiler_params=_cparams(("arbitrary",)),
        name="dsa_prompt",
    )(p, p, p, kit, kt, va, base)


def _dsa_sample_kernel(pt_ref, q_ref, qi_ref, misc_ref, kn_ref, vn_ref, kin_ref, base_ref, *rest,
                       n_pages, n_valid, top):
    ki_pages = rest[0:n_pages]
    k_pages = rest[n_pages:2 * n_pages]
    v_pages = rest[2 * n_pages:3 * n_pages]
    o_ref = rest[3 * n_pages + 1]
    kib_ref, kb_ref, vb_ref, msk_ref = rest[3 * n_pages + 2:]
    TQ = SEQ_PAD
    lo = n_pages * LANE
    NK = lo + LANE
    wi = misc_ref[:, MISC_WI:MISC_WI + IDX_HEADS] * (IDX_HEADS ** -0.5 * IDX_DIM ** -0.5)
    qi = qi_ref[...]
    qis = jnp.concatenate([qi[:, h * IDX_DIM:(h + 1) * IDX_DIM] for h in range(IDX_HEADS)], axis=0)

    eye = (lax.broadcasted_iota(I32, (HEAD_DIM, HEAD_DIM), 0)
           == lax.broadcasted_iota(I32, (HEAD_DIM, HEAD_DIM), 1)).astype(BF16)

    def new_slot(x):
        xt = _mm_nt(eye, x)
        return jnp.concatenate([xt, jnp.zeros((HEAD_DIM, LANE - TQ), F32)], axis=1).astype(BF16)

    for t in range(n_pages):
        kib_ref[:, t * LANE:(t + 1) * LANE] = ki_pages[t][0, 0].astype(BF16)
        for g in range(KV_HEADS):
            kb_ref[g, :, t * LANE:(t + 1) * LANE] = k_pages[t][0, 0, g].astype(BF16)
            vb_ref[g, :, t * LANE:(t + 1) * LANE] = v_pages[t][0, 0, g].astype(BF16)
    kib_ref[:, lo:NK] = new_slot(kin_ref[...])
    kn = kn_ref[...]
    vn = vn_ref[...]
    for g in range(KV_HEADS):
        kb_ref[g, :, lo:NK] = new_slot(kn[:, g * HEAD_DIM:(g + 1) * HEAD_DIM])
        vb_ref[g, :, lo:NK] = new_slot(vn[:, g * HEAD_DIM:(g + 1) * HEAD_DIM])

    s = jnp.maximum(_mm(qis, kib_ref[...]), 0.0)
    acc = s[0:TQ] * wi[:, 0:1]
    for h in range(1, IDX_HEADS):
        acc = acc + s[h * TQ:(h + 1) * TQ] * wi[:, h:h + 1]
    trow = lax.broadcasted_iota(I32, (TQ, 1), 0)
    new = lax.broadcasted_iota(I32, (1, NK), 1) - lo
    keys = _score_keys(acc, new <= jnp.minimum(trow, n_valid - 1))

    def count_ge(cand):
        return jnp.sum(jnp.where(keys >= cand, 1.0, 0.0), axis=-1, keepdims=True)

    thr = _kth_threshold(count_ge, (TQ, 1), float(top), bits_per_step=4)
    msk_ref[...] = jnp.where(keys >= thr, 0.0, NEG_BIG)

    @pl.when(jnp.max(jnp.where(trow < n_valid, count_ge(thr), 0.0)) > top)
    def _():
        pos = new + lo
        tied = keys == thr

        def count_tied_before(p):
            return jnp.sum(jnp.where(tied, jnp.where(pos < p, 1.0, 0.0), 0.0), axis=-1, keepdims=True)

        need = top - count_ge(thr + 1)
        last = jnp.zeros((TQ, 1), I32)
        for bit in range(max(1, (NK - 1).bit_length()) - 1, -1, -1):
            cand = last | jnp.int32(1 << bit)
            last = jnp.where(count_tied_before(cand) < need, cand, last)
        msk_ref[...] = jnp.where(tied, jnp.where(pos > last, NEG_BIG, 0.0), msk_ref[...])

    msk = jnp.concatenate([msk_ref[...]] * ATT_HEADS, axis=0)

    q = q_ref[...] * HEAD_DIM ** -0.5
    zero = jnp.zeros((TQ, HEAD_DIM), F32)
    q_rows, nears = [], []
    for h in range(ATT_HEADS):
        qh = q[:, h * HEAD_DIM:(h + 1) * HEAD_DIM]
        q_rows.append(jnp.concatenate([qh if g == h // GQA else zero for g in range(KV_HEADS)], axis=1))
        nears.append(_toeplitz_bias(base_ref[h:h + 1, :], TQ) - base_ref[h:h + 1, 2 * LANE - 1:2 * LANE])
    qbd = jnp.concatenate(q_rows, axis=0)
    bias = jnp.concatenate([jnp.zeros((ATT_HEADS * TQ, lo - LANE), F32),
                            jnp.concatenate(nears, axis=0)], axis=1)
    s = _mm(qbd, kb_ref[...].reshape(KV_HEADS * HEAD_DIM, NK)) + bias + msk
    mx = jnp.max(s, axis=-1, keepdims=True)
    pr = jnp.exp(s - mx)
    pr = pr / jnp.sum(pr, axis=-1, keepdims=True)
    out = _mm_nt(pr, vb_ref[...].reshape(KV_HEADS * HEAD_DIM, NK))
    for h in range(ATT_HEADS):
        g = h // GQA
        o_ref[:, h * HEAD_DIM:(h + 1) * HEAD_DIM] = out[h * TQ:(h + 1) * TQ,
                                                        g * HEAD_DIM:(g + 1) * HEAD_DIM].astype(o_ref.dtype)


def _dsa_sample(p, row0, n_seq, kidx_ln, base, layer, cache_k, cache_v, cache_kidx, page_table, n_valid,
                y_buf):
    n_pages = page_table.shape[1]
    past = n_pages * PAGE_SIZE
    top = min(TOPK_MAX, (past + n_valid) // 4)
    rb0 = row0 // SEQ_PAD
    ck = jnp.transpose(cache_k, (0, 1, 3, 4, 2))
    cv = jnp.transpose(cache_v, (0, 1, 3, 4, 2))
    cki = jnp.transpose(cache_kidx, (0, 1, 3, 2))

    def rmap(col):
        return lambda b, pt: (rb0 + b, col)

    def pmap(pg):
        return lambda b, pt: (layer, pt[b, pg], 0, 0)

    def pmap5(pg):
        return lambda b, pt: (layer, pt[b, pg], 0, 0, 0)

    in_specs = [pl.BlockSpec((SEQ_PAD, 1024), rmap(C_Q // 1024)),
                pl.BlockSpec((SEQ_PAD, 1024), rmap(C_QI // 1024)),
                pl.BlockSpec((SEQ_PAD, LANE), rmap(C_MISC // LANE)),
                pl.BlockSpec((SEQ_PAD, 256), rmap(C_K // 256)),
                pl.BlockSpec((SEQ_PAD, 256), rmap(C_V // 256)),
                pl.BlockSpec((SEQ_PAD, IDX_DIM), lambda b, pt: (rb0 + b, 0)),
                pl.BlockSpec((ATT_HEADS, 2 * LANE), lambda b, pt: (0, 0))]
    in_specs += [pl.BlockSpec((1, 1, IDX_DIM, PAGE_SIZE), pmap(pg)) for pg in range(n_pages)]
    in_specs += [pl.BlockSpec((1, 1, KV_HEADS, HEAD_DIM, PAGE_SIZE), pmap5(pg)) for pg in range(n_pages)]
    in_specs += [pl.BlockSpec((1, 1, KV_HEADS, HEAD_DIM, PAGE_SIZE), pmap5(pg)) for pg in range(n_pages)]
    kern = functools.partial(_dsa_sample_kernel, n_pages=n_pages, n_valid=n_valid, top=top)
    nk = (n_pages + 1) * LANE
    return pl.pallas_call(
        kern,
        grid_spec=pltpu.PrefetchScalarGridSpec(
            num_scalar_prefetch=1,
            grid=(n_seq,),
            in_specs=in_specs + [pl.BlockSpec(memory_space=pl.ANY)],
            out_specs=pl.BlockSpec((SEQ_PAD, MIX_W), lambda b, pt: (rb0 + b, 0)),
            scratch_shapes=[pltpu.VMEM((IDX_DIM, nk), BF16),
                            pltpu.VMEM((KV_HEADS, HEAD_DIM, nk), BF16),
                            pltpu.VMEM((KV_HEADS, HEAD_DIM, nk), BF16),
                            pltpu.VMEM((SEQ_PAD, nk), F32)]),
        out_shape=jax.ShapeDtypeStruct(y_buf.shape, BF16),
        input_output_aliases={1 + len(in_specs): 0},
        compiler_params=_cparams(("arbitrary",)),
        name="dsa_sample",
    )(page_table, p, p, p, p, p, kidx_ln, base,
      *([cki] * n_pages), *([ck] * n_pages), *([cv] * n_pages), y_buf)


def _t5_base(t5_table):
    x = jnp.arange(2 * LANE, dtype=jnp.int32)
    rel = jnp.where(x <= LANE, LANE - x, T5_MAX_DIST)
    n = jnp.maximum(rel, 0)
    exact = T5_BUCKETS // 2
    nf = jnp.maximum(n, exact).astype(F32)
    large = exact + (jnp.log(nf / exact) / math.log(T5_MAX_DIST / exact)
                     * (T5_BUCKETS - exact)).astype(jnp.int32)
    bucket = jnp.where(n < exact, n, jnp.minimum(large, T5_BUCKETS - 1))
    return t5_table[bucket].astype(F32).T


def _relayout_w_in(w):
    widths = (M_INNER, CONV_DIM, M_HEADS, ATT_HEADS * HEAD_DIM, KV_HEADS * HEAD_DIM,
              KV_HEADS * HEAD_DIM, IDX_HEADS * IDX_DIM, IDX_DIM, IDX_HEADS, R_PROJ, 3 * D_MODEL)
    offs = [0]
    for wd in widths:
        offs.append(offs[-1] + wd)
    z, xbc, dt, q, k, v, qi, ki, wi, pr, gates = (w[:, offs[i]:offs[i + 1]] for i in range(len(widths)))
    zeros = lambda n: jnp.zeros((w.shape[0], n), w.dtype)
    lora = pr[:, 3 * R_WIDTH:]
    cols = [z, xbc, q, qi, pr[:, :3 * R_WIDTH], gates,
            lora, zeros(LORA_W - lora.shape[1]), k, v,
            ki, dt, wi, zeros(LANE - IDX_DIM - M_HEADS - IDX_HEADS)]
    out = jnp.concatenate(cols, axis=1).astype(BF16)
    assert out.shape[1] == P_W
    return out


def _pad_cols(x, n):
    return jnp.concatenate([x, jnp.zeros(x.shape[:-1] + (n - x.shape[-1],), x.dtype)], axis=-1)


def _shift_layout(x):
    return _pad_cols(x, 3 * R_WIDTH + LORA_W)


def kernel(x_prompt, x_sample, cache_k, cache_v, cache_kidx, page_table, state_ssm, state_conv, state_wkv, state_shift, ffn1_norm, ffn1_gate, ffn1_up, ffn1_down, mix_norm, w_in, conv_w, conv_b, dt_bias, a_log, d_skip, ssm_norm, kidx_ln_g, kidx_ln_b, t5_table, rwkv_mu, rwkv_w0, rwkv_w2, rwkv_a0, rwkv_a2, rwkv_g2, rwkv_kk, rwkv_ka, rwkv_rk, rwkv_gn_g, rwkv_gn_b, w_br_a, w_br_b, w_br_c, w_out, ffn2_norm, ffn2_gate, ffn2_up, ffn2_down, final_norm):
    bp, S, d = x_prompt.shape
    nb, T_dec, _ = x_sample.shape
    depth = w_in.shape[0]
    assert bp == 1 and T_dec <= SEQ_PAD and S % SSD_CHUNK == 0
    n_p = bp * S
    n_rows = n_p + nb * SEQ_PAD
    xs_pad = jnp.concatenate([x_sample, jnp.zeros((nb, SEQ_PAD - T_dec, d), x_sample.dtype)], axis=1)
    h = jnp.concatenate([x_prompt.reshape(n_p, d), xs_pad.reshape(nb * SEQ_PAD, d)], axis=0)

    base = _t5_base(t5_table)
    e16 = jnp.repeat(jnp.eye(M_HEADS, dtype=F32), M_HEAD_DIM, axis=1)
    row = lambda x: x.reshape(1, -1).astype(F32)
    rc = min(RWKV_RC_CHUNK, S)
    shift_all = _shift_layout(state_shift).reshape(depth, nb, 1, 3 * R_WIDTH + LORA_W)
    outs = {k_: [] for k_ in ("k_p", "v_p", "ki_p", "conv_p", "sh_p", "k_s", "v_s", "ki_s", "conv_s", "sh_s")}
    ssm_p = ssm_s = wkv_p = wkv_s = None
    for l in range(depth):
        bf = lambda x: x.astype(BF16)
        h = _ffn(h, row(ffn1_norm[l]), bf(ffn1_gate[l]), bf(ffn1_up[l]), bf(ffn1_down[l]))
        p = _inproj(h, row(mix_norm[l]), _relayout_w_in(w_in[l]))
        kidx = _kidx_ln(p, row(kidx_ln_g[l]), row(kidx_ln_b[l]))

        ssd_par = (conv_w[l], row(conv_b[l]), row(dt_bias[l]), row(a_log[l]),
                   row(jnp.repeat(d_skip[l], M_HEAD_DIM)), row(ssm_norm[l]), e16)
        stack = dict(n_rows=n_rows, depth=depth, out_layer=l)
        ya, ssm_p = _ssd(p, 0, bp, S // SSD_CHUNK, SSD_CHUNK, SSD_CHUNK, 0,
                         jnp.zeros((1, bp, CONV_W - 1, CONV_DIM), F32),
                         jnp.zeros((1, bp, M_HEADS, M_HEAD_DIM, M_STATE), F32), *ssd_par,
                         st_buf=ssm_p, **stack)
        ya, ssm_s = _ssd(p, n_p, nb, 1, SEQ_PAD, T_dec, l, state_conv, state_ssm, *ssd_par,
                         y_buf=ya, st_buf=ssm_s, **stack)

        g2p = jnp.concatenate([rwkv_g2[l], jnp.zeros((LORA_W - R_DECAY_LORA - R_A_LORA - R_G_LORA,
                                                      R_WIDTH), F32)], axis=0)
        rw_par = (row(_shift_layout(rwkv_mu[l])), row(rwkv_w0[l]), rwkv_w2[l], row(rwkv_a0[l]),
                  rwkv_a2[l], g2p, row(rwkv_kk[l]), row(rwkv_ka[l]), row(rwkv_rk[l]),
                  row(rwkv_gn_g[l]), row(rwkv_gn_b[l]))
        sw = 3 * R_WIDTH + LORA_W
        yc, wkv_p = _rwkv(p, 0, bp, S // rc, rc, rc, 0, jnp.zeros((1, bp, 1, sw), F32),
                          jnp.zeros((1, bp, R_HEADS, R_HEAD, R_HEAD), F32), *rw_par,
                          st_buf=wkv_p, **stack)
        yc, wkv_s = _rwkv(p, n_p, nb, 1, SEQ_PAD, T_dec, l, shift_all, state_wkv, *rw_par,
                          y_buf=yc, st_buf=wkv_s, **stack)

        kp = p[:n_p, C_K:C_K + 256]
        vp = p[:n_p, C_V:C_V + 256]
        kit = bf(kidx[:n_p].T)
        kt = bf(kp.reshape(n_p, KV_HEADS, HEAD_DIM).transpose(1, 2, 0))
        vg = vp.reshape(n_p, KV_HEADS, HEAD_DIM).transpose(1, 0, 2)
        va = bf(jnp.concatenate([vg, jnp.ones((KV_HEADS, n_p, 1), F32),
                                 jnp.zeros((KV_HEADS, n_p, LANE - HEAD_DIM - 1), F32)], axis=-1))
        yb = _dsa_prompt(p, kit, kt, va, base, S, n_rows)
        yb = _dsa_sample(p, n_p, nb, kidx, base, l, cache_k, cache_v, cache_kidx, page_table, T_dec, yb)
        h = _merge(h, p, ya, yb, yc, bf(w_br_a[l]), bf(w_br_b[l]), bf(w_br_c[l]), bf(w_out[l]))
        h = _ffn(h, row(ffn2_norm[l]), bf(ffn2_gate[l]), bf(ffn2_up[l]), bf(ffn2_down[l]))

        ps = p[n_p:].reshape(nb, SEQ_PAD, P_W)[:, :T_dec]
        xbc = lambda t: jnp.concatenate([t[..., C_XS:C_XS + 1024], t[..., C_BC:C_BC + 1024]], axis=-1)
        prj = lambda t: jnp.concatenate([t[..., C_RR:C_RR + 3 * R_WIDTH],
                                         t[..., C_LORA:C_LORA + R_PROJ - 3 * R_WIDTH]], axis=-1)
        outs["k_p"].append(kp.reshape(bp, S, KV_HEADS, HEAD_DIM))
        outs["v_p"].append(vp.reshape(bp, S, KV_HEADS, HEAD_DIM))
        outs["ki_p"].append(kidx[:n_p].reshape(bp, S, IDX_DIM))
        outs["conv_p"].append(xbc(p[n_p - (CONV_W - 1):n_p]).reshape(bp, CONV_W - 1, CONV_DIM))
        outs["sh_p"].append(prj(p[n_p - 1:n_p]).reshape(bp, R_PROJ))
        outs["k_s"].append(ps[..., C_K:C_K + 256].reshape(nb, T_dec, KV_HEADS, HEAD_DIM))
        outs["v_s"].append(ps[..., C_V:C_V + 256].reshape(nb, T_dec, KV_HEADS, HEAD_DIM))
        outs["ki_s"].append(kidx[n_p:].reshape(nb, SEQ_PAD, IDX_DIM)[:, :T_dec])
        conv_full = jnp.concatenate([state_conv[l], xbc(ps)], axis=1)
        outs["conv_s"].append(conv_full[:, T_dec:])
        outs["sh_s"].append(prj(ps[:, T_dec - 1]))

    y = _final_norm(h, row(final_norm))
    y_prompt = y[:n_p].reshape(bp, S, d)
    y_sample = y[n_p:].reshape(nb, SEQ_PAD, d)[:, :T_dec]
    stk = lambda name: jnp.stack(outs[name], axis=0)
    return (y_prompt, y_sample, stk("k_p"), stk("v_p"), stk("ki_p"), ssm_p, stk("conv_p"),
            wkv_p, stk("sh_p"), stk("k_s"), stk("v_s"), stk("ki_s"), ssm_s,
            stk("conv_s"), wkv_s, stk("sh_s"))
```

```python
import functools
import math

import jax
import jax.numpy as jnp
from jax import lax
from jax.experimental import pallas as pl
from jax.experimental.pallas import tpu as pltpu

F32 = jnp.float32
BF16 = jnp.bfloat16
I32 = jnp.int32

D_MODEL = 2048
MIX_W = D_MODEL // 2
M_HEAD_DIM = 64
M_INNER = MIX_W
M_HEADS = M_INNER // M_HEAD_DIM
M_GROUPS = 4
M_STATE = 128
CONV_W = 4
CONV_DIM = M_INNER + 2 * M_GROUPS * M_STATE
SSD_CHUNK = 128
HEAD_DIM = 64
ATT_HEADS = MIX_W // HEAD_DIM
KV_HEADS = 4
GQA = ATT_HEADS // KV_HEADS
IDX_HEADS = 16
IDX_DIM = 64
TOPK_MAX = 256
T5_BUCKETS = 32
T5_MAX_DIST = 128
R_HEAD = 64
R_WIDTH = MIX_W
R_HEADS = R_WIDTH // R_HEAD
R_DECAY_LORA = 64
R_A_LORA = 64
R_G_LORA = 160
R_PROJ = 3 * R_WIDTH + R_DECAY_LORA + R_A_LORA + R_G_LORA
GN_EPS = 64e-5
D_FF = 256 * ((8 * D_MODEL // 3 + 255) // 256)
NORM_EPS = 1e-6
PAGE_SIZE = 128

SEQ_PAD = 8
LANE = 128
SUBLANES = 8
MXU_DIM = 256
VMEM_LIMIT = 56 * 2**20
INT_MIN = -2**31
NEG_BIG = -1e30

C_Z, C_XS, C_BC, C_Q, C_QI, C_RR, C_RK, C_RV = (i * 1024 for i in range(8))
C_GATE = 8192
C_LORA = C_GATE + 3 * D_MODEL
LORA_W = 512
C_K = C_LORA + LORA_W
C_V = C_K + 256
C_MISC = C_V + 256
P_W = C_MISC + LANE
MISC_DT = IDX_DIM
MISC_WI = IDX_DIM + M_HEADS
RWKV_RC_CHUNK = 64

HI = lax.Precision.HIGHEST


def _cparams(sem):
    return pltpu.CompilerParams(dimension_semantics=sem, vmem_limit_bytes=VMEM_LIMIT)


def _mm(a, b):
    return jnp.dot(a.astype(BF16), b.astype(BF16), preferred_element_type=F32)


def _mm_nt(a, b):
    return lax.dot_general(a.astype(BF16), b.astype(BF16), (((1,), (1,)), ((), ())),
                           preferred_element_type=F32)


def _mm_tn(a, b):
    return lax.dot_general(a.astype(BF16), b.astype(BF16), (((0,), (0,)), ((), ())),
                           preferred_element_type=F32)


def _mm_hi(a, b):
    return jnp.dot(a, b, precision=HI, preferred_element_type=F32)


def _mm_nt_hi(a, b):
    return lax.dot_general(a, b, (((1,), (1,)), ((), ())), precision=HI, preferred_element_type=F32)


def _sigmoid(x):
    return jax.nn.sigmoid(x)


def _silu(x):
    return x * jax.nn.sigmoid(x)


def _full(shape):
    nd = len(shape)
    return pl.BlockSpec(shape, lambda *_: (0,) * nd)


def _ffn_kernel(x_ref, g_ref, wg_ref, wu_ref, wd_ref, o_ref, n_ref):
    f = pl.program_id(1)

    @pl.when(f == 0)
    def _():
        x = x_ref[...]
        ms = jnp.mean(x * x, axis=-1, keepdims=True)
        n_ref[...] = (x * lax.rsqrt(ms + NORM_EPS) * g_ref[...]).astype(BF16)
        o_ref[...] = x

    n = n_ref[...]
    a = jnp.dot(n, wg_ref[...], preferred_element_type=F32)
    b = jnp.dot(n, wu_ref[...], preferred_element_type=F32)
    hid = (0.5 * _silu(a) * b).astype(BF16)
    o_ref[...] += jnp.dot(hid, wd_ref[...], preferred_element_type=F32)


def _ffn(x, g, wg, wu, wd, tm=512, tf=512):
    m = x.shape[0]
    tm = min(tm, m)
    return pl.pallas_call(
        _ffn_kernel,
        grid=(m // tm, D_FF // tf),
        in_specs=[pl.BlockSpec((tm, D_MODEL), lambda i, f: (i, 0)),
                  pl.BlockSpec((1, D_MODEL), lambda i, f: (0, 0)),
                  pl.BlockSpec((D_MODEL, tf), lambda i, f: (0, f)),
                  pl.BlockSpec((D_MODEL, tf), lambda i, f: (0, f)),
                  pl.BlockSpec((tf, D_MODEL), lambda i, f: (f, 0))],
        out_specs=pl.BlockSpec((tm, D_MODEL), lambda i, f: (i, 0)),
        out_shape=jax.ShapeDtypeStruct((m, D_MODEL), F32),
        scratch_shapes=[pltpu.VMEM((tm, D_MODEL), BF16)],
        compiler_params=_cparams(("parallel", "arbitrary")),
        name="ffn",
    )(x, g, wg, wu, wd)


def _inproj_kernel(x_ref, g_ref, w_ref, o_ref, n_ref):
    j = pl.program_id(1)

    @pl.when(j == 0)
    def _():
        x = x_ref[...]
        ms = jnp.mean(x * x, axis=-1, keepdims=True)
        n_ref[...] = (x * lax.rsqrt(ms + NORM_EPS) * g_ref[...]).astype(BF16)

    o_ref[...] = jnp.dot(n_ref[...], w_ref[...], preferred_element_type=F32)


def _inproj(x, g, w, tm=1024, tn=1408):
    m = x.shape[0]
    tm = tm if m % tm == 0 else min(512, m)
    return pl.pallas_call(
        _inproj_kernel,
        grid=(m // tm, P_W // tn),
        in_specs=[pl.BlockSpec((tm, D_MODEL), lambda i, j: (i, 0)),
                  pl.BlockSpec((1, D_MODEL), lambda i, j: (0, 0)),
                  pl.BlockSpec((D_MODEL, tn), lambda i, j: (0, j))],
        out_specs=pl.BlockSpec((tm, tn), lambda i, j: (i, j)),
        out_shape=jax.ShapeDtypeStruct((m, P_W), F32),
        scratch_shapes=[pltpu.VMEM((tm, D_MODEL), BF16)],
        compiler_params=_cparams(("parallel", "arbitrary")),
        name="inproj",
    )(x, g, w)


def _merge_kernel(h_ref, ya_ref, yb_ref, yc_ref, ga_ref, gb_ref, gc_ref,
                  wa_ref, wb_ref, wc_ref, wo_ref, o_ref):
    j = pl.program_id(1)

    @pl.when(j == 0)
    def _():
        o_ref[...] = h_ref[...]

    merged = (_sigmoid(ga_ref[...]) * jnp.dot(ya_ref[...], wa_ref[...], preferred_element_type=F32)
              + _sigmoid(gb_ref[...]) * jnp.dot(yb_ref[...], wb_ref[...], preferred_element_type=F32)
              + _sigmoid(gc_ref[...]) * jnp.dot(yc_ref[...], wc_ref[...], preferred_element_type=F32))
    o_ref[...] += jnp.dot(merged.astype(BF16), wo_ref[...], preferred_element_type=F32)


def _merge(h, p, ya, yb, yc, wa, wb, wc, wo, tm=512, tk=512):
    m = h.shape[0]
    tm = min(tm, m)
    gb0 = C_GATE // tk
    nk = D_MODEL // tk
    yspec = pl.BlockSpec((tm, MIX_W), lambda i, j: (i, 0))
    wspec = pl.BlockSpec((MIX_W, tk), lambda i, j: (0, j))
    return pl.pallas_call(
        _merge_kernel,
        grid=(m // tm, nk),
        in_specs=[pl.BlockSpec((tm, D_MODEL), lambda i, j: (i, 0)), yspec, yspec, yspec,
                  pl.BlockSpec((tm, tk), lambda i, j: (i, gb0 + j)),
                  pl.BlockSpec((tm, tk), lambda i, j: (i, gb0 + nk + j)),
                  pl.BlockSpec((tm, tk), lambda i, j: (i, gb0 + 2 * nk + j)),
                  wspec, wspec, wspec,
                  pl.BlockSpec((tk, D_MODEL), lambda i, j: (j, 0))],
        out_specs=pl.BlockSpec((tm, D_MODEL), lambda i, j: (i, 0)),
        out_shape=jax.ShapeDtypeStruct((m, D_MODEL), F32),
        compiler_params=_cparams(("parallel", "arbitrary")),
        name="merge",
    )(h, ya, yb, yc, p, p, p, wa, wb, wc, wo)


def _final_norm_kernel(x_ref, g_ref, o_ref):
    x = x_ref[...]
    ms = jnp.mean(x * x, axis=-1, keepdims=True)
    o_ref[...] = x * lax.rsqrt(ms + NORM_EPS) * g_ref[...]


def _final_norm(x, g, tm=512):
    m = x.shape[0]
    tm = min(tm, m)
    return pl.pallas_call(
        _final_norm_kernel,
        grid=(m // tm,),
        in_specs=[pl.BlockSpec((tm, D_MODEL), lambda i: (i, 0)), _full((1, D_MODEL))],
        out_specs=pl.BlockSpec((tm, D_MODEL), lambda i: (i, 0)),
        out_shape=jax.ShapeDtypeStruct((m, D_MODEL), F32),
        compiler_params=_cparams(("parallel",)),
        name="final_norm",
    )(x, g)


def _kidx_ln_kernel(m_ref, g_ref, b_ref, o_ref):
    x = m_ref[:, 0:IDX_DIM]
    mu = jnp.mean(x, axis=-1, keepdims=True)
    var = jnp.mean(jnp.square(x - mu), axis=-1, keepdims=True)
    o_ref[...] = (x - mu) * lax.rsqrt(var + NORM_EPS) * g_ref[...] + b_ref[...]


def _kidx_ln(p, g, b, tm=512):
    m = p.shape[0]
    tm = min(tm, m)
    return pl.pallas_call(
        _kidx_ln_kernel,
        grid=(m // tm,),
        in_specs=[pl.BlockSpec((tm, LANE), lambda i: (i, C_MISC // LANE)),
                  _full((1, IDX_DIM)), _full((1, IDX_DIM))],
        out_specs=pl.BlockSpec((tm, IDX_DIM), lambda i: (i, 0)),
        out_shape=jax.ShapeDtypeStruct((m, IDX_DIM), F32),
        compiler_params=_cparams(("parallel",)),
        name="kidx_ln",
    )(p, g, b)


def _shifted(x, k, prev, rows):
    y = pltpu.roll(x, k, 0)
    for r in range(k):
        y = jnp.where(rows == r, prev[3 - k + r:4 - k + r, :], y)
    return y


def _ssd_kernel(*refs, T, n_valid, n_alias):
    (z_ref, xs_ref, bc_ref, misc_ref, cst_ref, st0_ref, cw_ref, cb_ref, dtb_ref,
     alog_ref, dexp_ref, norm_ref, e16_ref) = refs[:13]
    y_ref, stout_ref, prev_ref, st_ref, yd_ref = refs[13 + n_alias:]
    c = pl.program_id(1)
    nc = pl.num_programs(1)

    @pl.when(c == 0)
    def _():
        prev_ref[...] = cst_ref[0, 0]
        st_ref[...] = st0_ref[0, 0]

    rows = lax.broadcasted_iota(I32, (T, 1), 0)
    cw = cw_ref[...]
    prev = prev_ref[...]

    def conv(x, lo, hi):
        pv = prev[:, lo:hi]
        y = cb_ref[:, lo:hi] + x * cw[3:4, lo:hi]
        for k in (1, 2, 3):
            y = y + _shifted(x, k, pv, rows) * cw[3 - k:4 - k, lo:hi]
        return _silu(y)

    xs_raw = xs_ref[...]
    bc_raw = bc_ref[...]
    xs = conv(xs_raw, 0, M_INNER)
    bc = conv(bc_raw, M_INNER, CONV_DIM)
    prev_ref[:, 0:M_INNER] = xs_raw[T - 3:T, :]
    prev_ref[:, M_INNER:CONV_DIM] = bc_raw[T - 3:T, :]

    dt = jax.nn.softplus(misc_ref[:, MISC_DT:MISC_DT + M_HEADS] + dtb_ref[...])
    if n_valid < T:
        dt = jnp.where(rows < n_valid, dt, 0.0)
    ad = dt * (-jnp.exp(alog_ref[...]))
    ri = lax.broadcasted_iota(I32, (T, T), 0)
    ci = lax.broadcasted_iota(I32, (T, T), 1)
    lower = ri >= ci
    tril = lower.astype(F32)
    triu = (ri <= ci).astype(F32)
    eye16 = (lax.broadcasted_iota(I32, (M_HEADS, M_HEADS), 0)
             == lax.broadcasted_iota(I32, (M_HEADS, M_HEADS), 1)).astype(F32)
    cs = _mm_hi(tril, ad)
    cst = _mm_hi(_mm_nt_hi(eye16, ad), triu)
    e16 = e16_ref[...]
    dt_e = _mm_hi(dt, e16)
    ecs_e = _mm_hi(jnp.exp(cs), e16)
    wl_e = _mm_hi(jnp.exp(cs[T - 1:T, :] - cs), e16)
    xd = xs * dt_e
    xdw = (xd * wl_e).astype(BF16)
    xd = xd.astype(BF16)
    bcb = bc.astype(BF16)
    for g in range(M_GROUPS):
        bg = bcb[:, g * M_STATE:(g + 1) * M_STATE]
        cg = bcb[:, (M_GROUPS + g) * M_STATE:(M_GROUPS + g + 1) * M_STATE]
        cbm = _mm_nt(cg, bg)
        for hh in range(M_HEADS // M_GROUPS):
            h = g * (M_HEADS // M_GROUPS) + hh
            hs = slice(h * M_HEAD_DIM, (h + 1) * M_HEAD_DIM)
            diff = cs[:, h:h + 1] - cst[h:h + 1, :]
            lm = jnp.exp(jnp.where(lower, diff, -jnp.inf))
            yd = _mm(cbm * lm, xd[:, hs])
            st = st_ref[h]
            yo = _mm_nt(cg, st)
            yd_ref[:, hs] = yd + yo * ecs_e[:, hs]
            upd = _mm_tn(xdw[:, hs], bg)
            dec = jnp.exp(cst[h:h + 1, T - 1:T])
            st_ref[h] = st * dec + upd

    y = yd_ref[...] + xs * dexp_ref[...]
    y = y * _silu(z_ref[...])
    gw = M_INNER // M_GROUPS
    for g in range(M_GROUPS):
        yg = y[:, g * gw:(g + 1) * gw]
        ms = jnp.mean(yg * yg, axis=-1, keepdims=True)
        y_ref[:, g * gw:(g + 1) * gw] = (yg * lax.rsqrt(ms + NORM_EPS)
                                         * norm_ref[:, g * gw:(g + 1) * gw]).astype(y_ref.dtype)

    @pl.when(c == nc - 1)
    def _():
        stout_ref[0, 0] = st_ref[...]


def _alias_inputs(n_in, bufs):
    specs, args, aliases = [], [], {}
    for k, b in enumerate(bufs):
        if b is not None:
            aliases[n_in + len(args)] = k
            specs.append(pl.BlockSpec(memory_space=pl.ANY))
            args.append(b)
    return specs, args, aliases


def _ssd(p, row0, n_seq, n_chunks, T, n_valid, layer, conv_state, ssm0, cw, cb, dtb, alog, dexp, norm,
         e16, *, n_rows, depth, out_layer, y_buf=None, st_buf=None):
    rb0 = row0 // T

    def rmap(col):
        return lambda s, c: (rb0 + s * n_chunks + c, col)

    args = [p, p, p, p, conv_state, ssm0, cw, cb, dtb, alog, dexp, norm, e16]
    a_specs, a_args, aliases = _alias_inputs(len(args), (y_buf, st_buf))
    kern = functools.partial(_ssd_kernel, T=T, n_valid=n_valid, n_alias=len(a_args))
    return pl.pallas_call(
        kern,
        grid=(n_seq, n_chunks),
        in_specs=[pl.BlockSpec((T, 1024), rmap(C_Z // 1024)),
                  pl.BlockSpec((T, 1024), rmap(C_XS // 1024)),
                  pl.BlockSpec((T, 1024), rmap(C_BC // 1024)),
                  pl.BlockSpec((T, LANE), rmap(C_MISC // LANE)),
                  pl.BlockSpec((1, 1, CONV_W - 1, CONV_DIM), lambda s, c: (layer, s, 0, 0)),
                  pl.BlockSpec((1, 1, M_HEADS, M_HEAD_DIM, M_STATE), lambda s, c: (layer, s, 0, 0, 0)),
                  _full((CONV_W, CONV_DIM)), _full((1, CONV_DIM)), _full((1, M_HEADS)),
                  _full((1, M_HEADS)), _full((1, M_INNER)), _full((1, M_INNER)),
                  _full((M_HEADS, M_INNER))] + a_specs,
        out_specs=[pl.BlockSpec((T, M_INNER), rmap(0)),
                   pl.BlockSpec((1, 1, M_HEADS, M_HEAD_DIM, M_STATE), lambda s, c: (out_layer, s, 0, 0, 0))],
        out_shape=[jax.ShapeDtypeStruct((n_rows, M_INNER), BF16),
                   jax.ShapeDtypeStruct((depth, n_seq, M_HEADS, M_HEAD_DIM, M_STATE), F32)],
        scratch_shapes=[pltpu.VMEM((CONV_W - 1, CONV_DIM), F32),
                        pltpu.VMEM((M_HEADS, M_HEAD_DIM, M_STATE), F32),
                        pltpu.VMEM((T, M_INNER), F32)],
        input_output_aliases=aliases,
        compiler_params=_cparams(("parallel", "arbitrary")),
        name="ssd",
    )(*args, *a_args)


def _rwkv_kernel(*refs, T, n_valid, HG, n_alias):
    (r_ref, k_ref, v_ref, lo_ref, sh0_ref, s0_ref, mu_ref, w0_ref, w2_ref, a0_ref,
     a2_ref, g2_ref, kkw_ref, kaw_ref, rk_ref, gng_ref, gnb_ref) = refs[:17]
    (y_ref, sout_ref, last_ref, s_ref, yb_ref, xs_ref, hr_ref,
     gm_ref) = refs[17 + n_alias:]
    c = pl.program_id(1)
    nc = pl.num_programs(1)

    @pl.when(c == 0)
    def _():
        last_ref[...] = sh0_ref[0, 0]
        s_ref[...] = s0_ref[0, 0]

    rows = lax.broadcasted_iota(I32, (T, 1), 0)

    def mix(ref, lo, hi):
        x = ref[...]
        prev = jnp.where(rows == 0, last_ref[:, lo:hi], pltpu.roll(x, 1, 0))
        last_ref[:, lo:hi] = x[T - 1:T, :]
        return x + (prev - x) * mu_ref[:, lo:hi]

    r = mix(r_ref, 0, 1024)
    k = mix(k_ref, 1024, 2048)
    v = mix(v_ref, 2048, 3072)
    lo = mix(lo_ref, 3072, 3072 + LORA_W)
    xw = lo[:, 0:R_DECAY_LORA]
    xa = lo[:, R_DECAY_LORA:R_DECAY_LORA + R_A_LORA]
    xg = lo[:, R_DECAY_LORA + R_A_LORA:LORA_W]
    wl = -jax.nn.softplus(-(w0_ref[...] + _mm_hi(jnp.tanh(xw), w2_ref[...]))) - 0.5
    ld = -jnp.exp(wl)
    a = _sigmoid(a0_ref[...] + _mm_hi(xa, a2_ref[...]))
    gate = _mm_hi(_sigmoid(xg), g2_ref[...])
    kk = k * kkw_ref[...]
    km = k * (1.0 + (a - 1.0) * kaw_ref[...])
    if n_valid < T:
        ok = rows < n_valid
        ld = jnp.where(ok, ld, 0.0)
        kk = jnp.where(ok, kk, 0.0)
        km = jnp.where(ok, km, 0.0)
        v = jnp.where(ok, v, 0.0)

    ri = lax.broadcasted_iota(I32, (T, T), 0)
    ci = lax.broadcasted_iota(I32, (T, T), 1)
    cum = _mm_hi((ri >= ci).astype(F32), ld)
    eg = jnp.exp(cum)
    egm = jnp.exp(cum - ld)
    ei = jnp.exp(-cum)
    el = jnp.exp(cum[T - 1:T, :] - cum)
    r_eg = r * eg
    k_ei = km * ei
    k_el = km * el
    rk_sum = r * km * rk_ref[...]
    gng = gng_ref[...]
    gnb = gnb_ref[...]

    R = HG * T
    GW = HG * R_HEAD
    rr = lax.broadcasted_iota(I32, (R, R), 0)
    cc = lax.broadcasted_iota(I32, (R, R), 1)
    same = (rr // T) == (cc // T)
    strict = jnp.logical_and(same, rr > cc)
    incl = jnp.logical_and(same, rr >= cc)
    head_of = (lax.broadcasted_iota(I32, (R, GW), 0) // T) == (lax.broadcasted_iota(I32, (R, GW), 1) // R_HEAD)
    blk_r = (rr % T) // SUBLANES
    blk_c = (cc % T) // SUBLANES

    n_groups = R_HEADS // HG

    def head_slices(gi):
        return [slice((gi * HG + hh) * R_HEAD, (gi * HG + hh + 1) * R_HEAD) for hh in range(HG)]

    def stack(x, gi):
        return jnp.concatenate([x[:, s_] for s_ in head_slices(gi)], axis=0)

    def unit_kk(gi):
        kks = stack(kk, gi)
        nrm = jnp.sqrt(jnp.sum(kks * kks, axis=-1, keepdims=True))
        return kks / jnp.maximum(nrm, 1e-12)

    for gi in range(n_groups):
        kks = unit_kk(gi)
        q2 = jnp.concatenate([kks * stack(egm, gi), stack(r_eg, gi)], axis=0)
        k2 = jnp.concatenate([stack(k_ei, gi), kks * stack(a, gi) * stack(ei, gi)], axis=0)
        gm = _mm_nt(q2, k2)
        gm_ref[gi] = gm
        scat = s_ref[gi * HG:(gi + 1) * HG].reshape(GW, R_HEAD)
        hmw = _mm_nt(q2, scat)
        hk = jnp.concatenate([hmw[hh * T:(hh + 1) * T, hh * R_HEAD:(hh + 1) * R_HEAD]
                              for hh in range(HG)], axis=0)
        hr_ref[gi] = jnp.concatenate([hmw[R + hh * T:R + (hh + 1) * T, hh * R_HEAD:(hh + 1) * R_HEAD]
                                      for hh in range(HG)], axis=0)
        xs_ref[gi] = hk + _mm(jnp.where(strict, gm[0:R, 0:R], 0.0), stack(v, gi))

    for jb in range(T // SUBLANES):
        for gi in range(n_groups):
            for hh in range(HG):
                r0 = hh * T + jb * SUBLANES
                xb = xs_ref[gi, r0:r0 + SUBLANES, :]
                lrow = gm_ref[gi, r0:r0 + SUBLANES, R:2 * R]
                for s in range(1, SUBLANES):
                    below = lax.broadcasted_iota(I32, (SUBLANES, 1), 0) >= s
                    xb = xb - jnp.where(below, lrow[:, r0 + s - 1:r0 + s], 0.0) * xb[s - 1:s, :]
                xs_ref[gi, r0:r0 + SUBLANES, :] = xb
            if jb + 1 < T // SUBLANES:
                later = jnp.where(jnp.logical_and(jnp.logical_and(same, blk_c == jb), blk_r > jb),
                                  gm_ref[gi, 0:R, R:2 * R], 0.0)
                xs_ref[gi] = xs_ref[gi] - _mm(later, xs_ref[gi])

    for gi in range(n_groups):
        hsl = head_slices(gi)
        kks = unit_kk(gi)
        bs = kks * stack(a, gi)
        vs = stack(v, gi)
        u = -xs_ref[gi]
        ak = jnp.where(incl, gm_ref[gi, R:2 * R, 0:R], 0.0)
        ab = jnp.where(incl, gm_ref[gi, R:2 * R, R:2 * R], 0.0)
        scat = s_ref[gi * HG:(gi + 1) * HG].reshape(GW, R_HEAD)
        y = hr_ref[gi] + _mm(jnp.concatenate([ak, ab], axis=1), jnp.concatenate([vs, u], axis=0))
        vw = jnp.where(head_of, jnp.concatenate([v[:, gi * GW:(gi + 1) * GW]] * HG, axis=0), 0.0)
        uw = jnp.where(head_of, jnp.concatenate([u] * HG, axis=1), 0.0)
        upd = _mm_tn(jnp.concatenate([vw, uw], axis=0),
                     jnp.concatenate([stack(k_el, gi), bs * stack(el, gi)], axis=0))
        g_last = jnp.concatenate([jnp.broadcast_to(eg[T - 1:T, s_], (R_HEAD, R_HEAD)) for s_ in hsl],
                                 axis=0)
        s_ref[gi * HG:(gi + 1) * HG] = (scat * g_last + upd).reshape(HG, R_HEAD, R_HEAD)
        mu = jnp.mean(y, axis=-1, keepdims=True)
        var = jnp.mean(jnp.square(y - mu), axis=-1, keepdims=True)
        gs = jnp.concatenate([jnp.broadcast_to(gng[:, s_], (T, R_HEAD)) for s_ in hsl], axis=0)
        gb = jnp.concatenate([jnp.broadcast_to(gnb[:, s_], (T, R_HEAD)) for s_ in hsl], axis=0)
        yn = (y - mu) * lax.rsqrt(var + GN_EPS) * gs + gb
        yn = yn + jnp.sum(stack(rk_sum, gi), axis=-1, keepdims=True) * vs
        for hh in range(HG):
            yb_ref[:, hsl[hh]] = yn[hh * T:(hh + 1) * T]

    y_ref[...] = (yb_ref[...] * gate).astype(y_ref.dtype)

    @pl.when(c == nc - 1)
    def _():
        sout_ref[0, 0] = s_ref[...]


def _rwkv(p, row0, n_seq, n_chunks, T, n_valid, layer, shift0, s0, mu, w0, w2, a0, a2, g2, kkw, kaw,
          rk, gng, gnb, *, n_rows, depth, out_layer, y_buf=None, st_buf=None):
    rb0 = row0 // T

    def rmap(col):
        return lambda s, c: (rb0 + s * n_chunks + c, col)

    sw = 3072 + LORA_W
    hg = min(R_HEADS, max(1, MXU_DIM // T))
    args = [p, p, p, p, shift0, s0, mu, w0, w2, a0, a2, g2, kkw, kaw, rk, gng, gnb]
    a_specs, a_args, aliases = _alias_inputs(len(args), (y_buf, st_buf))
    kern = functools.partial(_rwkv_kernel, T=T, n_valid=n_valid, HG=hg, n_alias=len(a_args))
    return pl.pallas_call(
        kern,
        grid=(n_seq, n_chunks),
        in_specs=[pl.BlockSpec((T, 1024), rmap(C_RR // 1024)),
                  pl.BlockSpec((T, 1024), rmap(C_RK // 1024)),
                  pl.BlockSpec((T, 1024), rmap(C_RV // 1024)),
                  pl.BlockSpec((T, LORA_W), rmap(C_LORA // LORA_W)),
                  pl.BlockSpec((1, 1, 1, sw), lambda s, c: (layer, s, 0, 0)),
                  pl.BlockSpec((1, 1, R_HEADS, R_HEAD, R_HEAD), lambda s, c: (layer, s, 0, 0, 0)),
                  _full((1, sw)), _full((1, R_WIDTH)), _full((R_DECAY_LORA, R_WIDTH)),
                  _full((1, R_WIDTH)), _full((R_A_LORA, R_WIDTH)),
                  _full((LORA_W - R_DECAY_LORA - R_A_LORA, R_WIDTH)),
                  _full((1, R_WIDTH)), _full((1, R_WIDTH)), _full((1, R_WIDTH)),
                  _full((1, R_WIDTH)), _full((1, R_WIDTH))] + a_specs,
        out_specs=[pl.BlockSpec((T, R_WIDTH), rmap(0)),
                   pl.BlockSpec((1, 1, R_HEADS, R_HEAD, R_HEAD), lambda s, c: (out_layer, s, 0, 0, 0))],
        out_shape=[jax.ShapeDtypeStruct((n_rows, R_WIDTH), BF16),
                   jax.ShapeDtypeStruct((depth, n_seq, R_HEADS, R_HEAD, R_HEAD), F32)],
        scratch_shapes=[pltpu.VMEM((1, sw), F32),
                        pltpu.VMEM((R_HEADS, R_HEAD, R_HEAD), F32),
                        pltpu.VMEM((T, R_WIDTH), F32),
                        pltpu.VMEM((R_HEADS // hg, hg * T, R_HEAD), F32),
                        pltpu.VMEM((R_HEADS // hg, hg * T, R_HEAD), F32),
                        pltpu.VMEM((R_HEADS // hg, 2 * hg * T, 2 * hg * T), F32)],
        input_output_aliases=aliases,
        compiler_params=_cparams(("parallel", "arbitrary")),
        name="rwkv",
    )(*args, *a_args)


def _score_keys(acc, causal):
    bits = pltpu.bitcast(acc, I32)
    key = bits ^ ((bits >> 31) & 0x7FFFFFFF)
    return jnp.where(causal, key, INT_MIN)


def _kth_threshold(count_ge, shape, top, bits_per_step=1):
    tb = jnp.zeros(shape, I32)
    for shift in range(32 - bits_per_step, -1, -bits_per_step):
        digit = jnp.zeros(shape, I32)
        for d in range(1, 1 << bits_per_step):
            inc = d << shift
            inc = inc - (1 << 32) if inc >= (1 << 31) else inc
            cnt = count_ge((tb | jnp.int32(inc)) ^ jnp.int32(INT_MIN))
            digit = digit + jnp.where(cnt >= top, 1, 0)
        tb = tb | (digit << shift)
    return jnp.maximum(tb ^ jnp.int32(INT_MIN), INT_MIN + 1)


def _toeplitz_bias(base_row, nrows):
    return pltpu.roll(jnp.broadcast_to(base_row, (nrows, 2 * LANE)), 0, 1, stride=1, stride_axis=0)


def _dsa_prompt_kernel(q_ref, qi_ref, misc_ref, kit_ref, kt_ref, va_ref, base_ref, o_ref,
                       key_ref, mx_ref, acc_ref, near_ref, *, top, TA):
    i = pl.program_id(0)
    QB = LANE
    q0 = i * QB
    n_a = (q0 + QB + TA - 1) // TA
    wi = misc_ref[:, MISC_WI:MISC_WI + IDX_HEADS] * (IDX_HEADS ** -0.5 * IDX_DIM ** -0.5)
    qi = qi_ref[...]
    qis = jnp.concatenate([qi[:, h * IDX_DIM:(h + 1) * IDX_DIM] for h in range(IDX_HEADS)],
                          axis=0).astype(BF16)
    rowq = q0 + lax.broadcasted_iota(I32, (QB, 1), 0)

    def score_tile(kt, carry):
        off = pl.multiple_of(kt * TA, TA)
        s = jnp.maximum(jnp.dot(qis, kit_ref[:, pl.ds(off, TA)], preferred_element_type=F32), 0.0)
        acc = s[0:QB] * wi[:, 0:1]
        for h in range(1, IDX_HEADS):
            acc = acc + s[h * QB:(h + 1) * QB] * wi[:, h:h + 1]
        col = off + lax.broadcasted_iota(I32, (1, TA), 1)
        key_ref[:, pl.ds(off, TA)] = _score_keys(acc, col <= rowq)
        return carry

    lax.fori_loop(0, n_a, score_tile, 0)

    def count_ge(cand):
        def body(kt, cnt):
            off = pl.multiple_of(kt * TA, TA)
            ge = jnp.where(key_ref[:, pl.ds(off, TA)] >= cand, 1.0, 0.0)
            for j in range(TA // LANE):
                cnt = cnt + ge[:, j * LANE:(j + 1) * LANE]
            return cnt
        cnt = lax.fori_loop(0, n_a, body, jnp.zeros((QB, LANE), F32))
        return jnp.sum(cnt, axis=-1, keepdims=True)

    thr = _kth_threshold(count_ge, (QB, 1), float(top))

    n_ge = count_ge(thr)

    @pl.when(jnp.max(n_ge) > top)
    def _():
        def count_tied_before(pos):
            def body(kt, cnt):
                off = pl.multiple_of(kt * TA, TA)
                col = off + lax.broadcasted_iota(I32, (1, TA), 1)
                hit = jnp.where(key_ref[:, pl.ds(off, TA)] == thr, jnp.where(col < pos, 1.0, 0.0), 0.0)
                for j in range(TA // LANE):
                    cnt = cnt + hit[:, j * LANE:(j + 1) * LANE]
                return cnt
            cnt = lax.fori_loop(0, n_a, body, jnp.zeros((QB, LANE), F32))
            return jnp.sum(cnt, axis=-1, keepdims=True)

        need = top - count_ge(thr + 1)
        last = jnp.zeros((QB, 1), I32)
        for bit in range(max(1, (kit_ref.shape[1] - 1).bit_length()) - 1, -1, -1):
            cand = last | jnp.int32(1 << bit)
            last = jnp.where(count_tied_before(cand) < need, cand, last)

        def drop(kt, carry):
            off = pl.multiple_of(kt * TA, TA)
            col = off + lax.broadcasted_iota(I32, (1, TA), 1)
            k = key_ref[:, pl.ds(off, TA)]
            key_ref[:, pl.ds(off, TA)] = jnp.where(k == thr, jnp.where(col > last, INT_MIN, k), k)
            return carry

        lax.fori_loop(0, n_a, drop, 0)

    @pl.when(i == 0)
    def _():
        for g in range(KV_HEADS):
            near_ref[g] = jnp.concatenate(
                [_toeplitz_bias(base_ref[h:h + 1, :], QB) - base_ref[h:h + 1, 2 * LANE - 1:2 * LANE]
                 for h in range(g * GQA, (g + 1) * GQA)], axis=0)

    q = q_ref[...]
    lim = (i - 1) * QB
    n_far = (jnp.maximum(lim, 0) + TA - 1) // TA
    qgs = []
    for g in range(KV_HEADS):
        heads = range(g * GQA, (g + 1) * GQA)
        qgs.append((jnp.concatenate([q[:, h * HEAD_DIM:(h + 1) * HEAD_DIM] for h in heads], axis=0)
                    * HEAD_DIM ** -0.5).astype(BF16))

    def far_logits(g, off, msk):
        s = jnp.dot(qgs[g], kt_ref[g, :, pl.ds(off, TA)], preferred_element_type=F32)
        return (s.reshape(GQA, QB, TA) + msk[None]).reshape(GQA * QB, TA)

    def far_mask(off):
        col = off + lax.broadcasted_iota(I32, (1, TA), 1)
        return jnp.where(key_ref[:, pl.ds(off, TA)] >= thr,
                         jnp.where(col < lim, 0.0, NEG_BIG), NEG_BIG)

    def near_logits(g, off, lo, width, msk):
        s = (jnp.dot(qgs[g], kt_ref[g, :, pl.ds(off, width)], preferred_element_type=F32)
             + near_ref[g, :, lo:lo + width])
        return (s.reshape(GQA, QB, width) + msk[None]).reshape(GQA * QB, width)

    def near_mask(off, width):
        return jnp.where(key_ref[:, pl.ds(off, width)] >= thr, 0.0, NEG_BIG)

    mx_ref[...] = jnp.full(mx_ref.shape, NEG_BIG, F32)
    acc_ref[...] = jnp.zeros(acc_ref.shape, F32)

    def accumulate(g, s, v_tile):
        width = s.shape[1]
        m = s[:, 0:LANE]
        for j in range(1, width // LANE):
            m = jnp.maximum(m, s[:, j * LANE:(j + 1) * LANE])
        m_old = mx_ref[g]
        m_new = jnp.maximum(m_old, jnp.broadcast_to(jnp.max(m, axis=-1, keepdims=True), m.shape))
        pr = jnp.exp((s - jnp.concatenate([m_new] * (width // LANE), axis=1)).astype(BF16))
        acc_ref[g] = (acc_ref[g] * jnp.exp(m_old - m_new)
                      + jnp.dot(pr, v_tile, preferred_element_type=F32))
        mx_ref[g] = m_new

    def far_acc(kt, carry):
        off = pl.multiple_of(kt * TA, TA)
        msk = far_mask(off)
        for g in range(KV_HEADS):
            accumulate(g, far_logits(g, off, msk), va_ref[g, pl.ds(off, TA), :])
        return carry

    def near_acc(kt, lo, width):
        off = pl.multiple_of(kt * LANE, LANE)
        msk = near_mask(off, width)
        for g in range(KV_HEADS):
            accumulate(g, near_logits(g, off, lo, width, msk), va_ref[g, pl.ds(off, width), :])

    lax.fori_loop(0, n_far, far_acc, 0)

    @pl.when(i == 0)
    def _():
        near_acc(0, LANE, LANE)

    @pl.when(i >= 1)
    def _():
        near_acc(i - 1, 0, 2 * LANE)
    for g in range(KV_HEADS):
        acc = acc_ref[g]
        out = acc[:, 0:HEAD_DIM] / acc[:, HEAD_DIM:HEAD_DIM + 1]
        for hh in range(GQA):
            h = g * GQA + hh
            o_ref[:, h * HEAD_DIM:(h + 1) * HEAD_DIM] = out[hh * QB:(hh + 1) * QB].astype(o_ref.dtype)


def _dsa_prompt(p, kit, kt, va, base, S, n_rows):
    top = min(TOPK_MAX, S // 4)
    TA = min(512, S)
    kern = functools.partial(_dsa_prompt_kernel, top=top, TA=TA)
    return pl.pallas_call(
        kern,
        grid=(S // LANE,),
        in_specs=[pl.BlockSpec((LANE, 1024), lambda i: (i, C_Q // 1024)),
                  pl.BlockSpec((LANE, 1024), lambda i: (i, C_QI // 1024)),
                  pl.BlockSpec((LANE, LANE), lambda i: (i, C_MISC // LANE)),
                  _full((IDX_DIM, S)), _full((KV_HEADS, HEAD_DIM, S)), _full((KV_HEADS, S, LANE)),
                  _full((ATT_HEADS, 2 * LANE))],
        out_specs=pl.BlockSpec((LANE, MIX_W), lambda i: (i, 0)),
        out_shape=jax.ShapeDtypeStruct((n_rows, MIX_W), BF16),
        scratch_shapes=[pltpu.VMEM((LANE, S), I32),
                        pltpu.VMEM((KV_HEADS, GQA * LANE, LANE), F32),
                        pltpu.VMEM((KV_HEADS, GQA * LANE, LANE), F32),
                        pltpu.VMEM((KV_HEADS, GQA * LANE, 2 * LANE), F32)],
        compiler_params=_cparams(("arbitrary",)),
        name="dsa_prompt",
    )(p, p, p, kit, kt, va, base)


def _dsa_sample_kernel(pt_ref, q_ref, qi_ref, misc_ref, kn_ref, vn_ref, kin_ref, base_ref, *rest,
                       n_pages, n_valid, top):
    ki_pages = rest[0:n_pages]
    k_pages = rest[n_pages:2 * n_pages]
    v_pages = rest[2 * n_pages:3 * n_pages]
    o_ref = rest[3 * n_pages + 1]
    kib_ref, kb_ref, vb_ref, msk_ref = rest[3 * n_pages + 2:]
    TQ = SEQ_PAD
    lo = n_pages * LANE
    NK = lo + LANE
    wi = misc_ref[:, MISC_WI:MISC_WI + IDX_HEADS] * (IDX_HEADS ** -0.5 * IDX_DIM ** -0.5)
    qi = qi_ref[...]
    qis = jnp.concatenate([qi[:, h * IDX_DIM:(h + 1) * IDX_DIM] for h in range(IDX_HEADS)], axis=0)

    eye = (lax.broadcasted_iota(I32, (HEAD_DIM, HEAD_DIM), 0)
           == lax.broadcasted_iota(I32, (HEAD_DIM, HEAD_DIM), 1)).astype(BF16)

    def new_slot(x):
        xt = _mm_nt(eye, x)
        return jnp.concatenate([xt, jnp.zeros((HEAD_DIM, LANE - TQ), F32)], axis=1).astype(BF16)

    for t in range(n_pages):
        kib_ref[:, t * LANE:(t + 1) * LANE] = ki_pages[t][0, 0].astype(BF16)
        for g in range(KV_HEADS):
            kb_ref[g, :, t * LANE:(t + 1) * LANE] = k_pages[t][0, 0, g].astype(BF16)
            vb_ref[g, :, t * LANE:(t + 1) * LANE] = v_pages[t][0, 0, g].astype(BF16)
    kib_ref[:, lo:NK] = new_slot(kin_ref[...])
    kn = kn_ref[...]
    vn = vn_ref[...]
    for g in range(KV_HEADS):
        kb_ref[g, :, lo:NK] = new_slot(kn[:, g * HEAD_DIM:(g + 1) * HEAD_DIM])
        vb_ref[g, :, lo:NK] = new_slot(vn[:, g * HEAD_DIM:(g + 1) * HEAD_DIM])

    s = jnp.maximum(_mm(qis, kib_ref[...]), 0.0)
    acc = s[0:TQ] * wi[:, 0:1]
    for h in range(1, IDX_HEADS):
        acc = acc + s[h * TQ:(h + 1) * TQ] * wi[:, h:h + 1]
    trow = lax.broadcasted_iota(I32, (TQ, 1), 0)
    new = lax.broadcasted_iota(I32, (1, NK), 1) - lo
    keys = _score_keys(acc, new <= jnp.minimum(trow, n_valid - 1))

    def count_ge(cand):
        return jnp.sum(jnp.where(keys >= cand, 1.0, 0.0), axis=-1, keepdims=True)

    thr = _kth_threshold(count_ge, (TQ, 1), float(top), bits_per_step=4)
    msk_ref[...] = jnp.where(keys >= thr, 0.0, NEG_BIG)

    @pl.when(jnp.max(jnp.where(trow < n_valid, count_ge(thr), 0.0)) > top)
    def _():
        pos = new + lo
        tied = keys == thr

        def count_tied_before(p):
            return jnp.sum(jnp.where(tied, jnp.where(pos < p, 1.0, 0.0), 0.0), axis=-1, keepdims=True)

        need = top - count_ge(thr + 1)
        last = jnp.zeros((TQ, 1), I32)
        for bit in range(max(1, (NK - 1).bit_length()) - 1, -1, -1):
            cand = last | jnp.int32(1 << bit)
            last = jnp.where(count_tied_before(cand) < need, cand, last)
        msk_ref[...] = jnp.where(tied, jnp.where(pos > last, NEG_BIG, 0.0), msk_ref[...])

    msk = jnp.concatenate([msk_ref[...]] * ATT_HEADS, axis=0)

    q = q_ref[...] * HEAD_DIM ** -0.5
    zero = jnp.zeros((TQ, HEAD_DIM), F32)
    q_rows, nears = [], []
    for h in range(ATT_HEADS):
        qh = q[:, h * HEAD_DIM:(h + 1) * HEAD_DIM]
        q_rows.append(jnp.concatenate([qh if g == h // GQA else zero for g in range(KV_HEADS)], axis=1))
        nears.append(_toeplitz_bias(base_ref[h:h + 1, :], TQ) - base_ref[h:h + 1, 2 * LANE - 1:2 * LANE])
    qbd = jnp.concatenate(q_rows, axis=0)
    bias = jnp.concatenate([jnp.zeros((ATT_HEADS * TQ, lo - LANE), F32),
                            jnp.concatenate(nears, axis=0)], axis=1)
    s = _mm(qbd, kb_ref[...].reshape(KV_HEADS * HEAD_DIM, NK)) + bias + msk
    mx = jnp.max(s, axis=-1, keepdims=True)
    pr = jnp.exp(s - mx)
    pr = pr / jnp.sum(pr, axis=-1, keepdims=True)
    out = _mm_nt(pr, vb_ref[...].reshape(KV_HEADS * HEAD_DIM, NK))
    for h in range(ATT_HEADS):
        g = h // GQA
        o_ref[:, h * HEAD_DIM:(h + 1) * HEAD_DIM] = out[h * TQ:(h + 1) * TQ,
                                                        g * HEAD_DIM:(g + 1) * HEAD_DIM].astype(o_ref.dtype)


def _dsa_sample(p, row0, n_seq, kidx_ln, base, layer, cache_k, cache_v, cache_kidx, page_table, n_valid,
                y_buf):
    n_pages = page_table.shape[1]
    past = n_pages * PAGE_SIZE
    top = min(TOPK_MAX, (past + n_valid) // 4)
    rb0 = row0 // SEQ_PAD
    ck = jnp.transpose(cache_k, (0, 1, 3, 4, 2))
    cv = jnp.transpose(cache_v, (0, 1, 3, 4, 2))
    cki = jnp.transpose(cache_kidx, (0, 1, 3, 2))

    def rmap(col):
        return lambda b, pt: (rb0 + b, col)

    def pmap(pg):
        return lambda b, pt: (layer, pt[b, pg], 0, 0)

    def pmap5(pg):
        return lambda b, pt: (layer, pt[b, pg], 0, 0, 0)

    in_specs = [pl.BlockSpec((SEQ_PAD, 1024), rmap(C_Q // 1024)),
                pl.BlockSpec((SEQ_PAD, 1024), rmap(C_QI // 1024)),
                pl.BlockSpec((SEQ_PAD, LANE), rmap(C_MISC // LANE)),
                pl.BlockSpec((SEQ_PAD, 256), rmap(C_K // 256)),
                pl.BlockSpec((SEQ_PAD, 256), rmap(C_V // 256)),
                pl.BlockSpec((SEQ_PAD, IDX_DIM), lambda b, pt: (rb0 + b, 0)),
                pl.BlockSpec((ATT_HEADS, 2 * LANE), lambda b, pt: (0, 0))]
    in_specs += [pl.BlockSpec((1, 1, IDX_DIM, PAGE_SIZE), pmap(pg)) for pg in range(n_pages)]
    in_specs += [pl.BlockSpec((1, 1, KV_HEADS, HEAD_DIM, PAGE_SIZE), pmap5(pg)) for pg in range(n_pages)]
    in_specs += [pl.BlockSpec((1, 1, KV_HEADS, HEAD_DIM, PAGE_SIZE), pmap5(pg)) for pg in range(n_pages)]
    kern = functools.partial(_dsa_sample_kernel, n_pages=n_pages, n_valid=n_valid, top=top)
    nk = (n_pages + 1) * LANE
    return pl.pallas_call(
        kern,
        grid_spec=pltpu.PrefetchScalarGridSpec(
            num_scalar_prefetch=1,
            grid=(n_seq,),
            in_specs=in_specs + [pl.BlockSpec(memory_space=pl.ANY)],
            out_specs=pl.BlockSpec((SEQ_PAD, MIX_W), lambda b, pt: (rb0 + b, 0)),
            scratch_shapes=[pltpu.VMEM((IDX_DIM, nk), BF16),
                            pltpu.VMEM((KV_HEADS, HEAD_DIM, nk), BF16),
                            pltpu.VMEM((KV_HEADS, HEAD_DIM, nk), BF16),
                            pltpu.VMEM((SEQ_PAD, nk), F32)]),
        out_shape=jax.ShapeDtypeStruct(y_buf.shape, BF16),
        input_output_aliases={1 + len(in_specs): 0},
        compiler_params=_cparams(("arbitrary",)),
        name="dsa_sample",
    )(page_table, p, p, p, p, p, kidx_ln, base,
      *([cki] * n_pages), *([ck] * n_pages), *([cv] * n_pages), y_buf)


def _t5_base(t5_table):
    x = jnp.arange(2 * LANE, dtype=jnp.int32)
    rel = jnp.where(x <= LANE, LANE - x, T5_MAX_DIST)
    n = jnp.maximum(rel, 0)
    exact = T5_BUCKETS // 2
    nf = jnp.maximum(n, exact).astype(F32)
    large = exact + (jnp.log(nf / exact) / math.log(T5_MAX_DIST / exact)
                     * (T5_BUCKETS - exact)).astype(jnp.int32)
    bucket = jnp.where(n < exact, n, jnp.minimum(large, T5_BUCKETS - 1))
    return t5_table[bucket].astype(F32).T


def _relayout_w_in(w):
    widths = (M_INNER, CONV_DIM, M_HEADS, ATT_HEADS * HEAD_DIM, KV_HEADS * HEAD_DIM,
              KV_HEADS * HEAD_DIM, IDX_HEADS * IDX_DIM, IDX_DIM, IDX_HEADS, R_PROJ, 3 * D_MODEL)
    offs = [0]
    for wd in widths:
        offs.append(offs[-1] + wd)
    z, xbc, dt, q, k, v, qi, ki, wi, pr, gates = (w[:, offs[i]:offs[i + 1]] for i in range(len(widths)))
    zeros = lambda n: jnp.zeros((w.shape[0], n), w.dtype)
    lora = pr[:, 3 * R_WIDTH:]
    cols = [z, xbc, q, qi, pr[:, :3 * R_WIDTH], gates,
            lora, zeros(LORA_W - lora.shape[1]), k, v,
            ki, dt, wi, zeros(LANE - IDX_DIM - M_HEADS - IDX_HEADS)]
    out = jnp.concatenate(cols, axis=1).astype(BF16)
    assert out.shape[1] == P_W
    return out


def _pad_cols(x, n):
    return jnp.concatenate([x, jnp.zeros(x.shape[:-1] + (n - x.shape[-1],), x.dtype)], axis=-1)


def _shift_layout(x):
    return _pad_cols(x, 3 * R_WIDTH + LORA_W)


def kernel(x_prompt, x_sample, cache_k, cache_v, cache_kidx, page_table, state_ssm, state_conv, state_wkv, state_shift, ffn1_norm, ffn1_gate, ffn1_up, ffn1_down, mix_norm, w_in, conv_w, conv_b, dt_bias, a_log, d_skip, ssm_norm, kidx_ln_g, kidx_ln_b, t5_table, rwkv_mu, rwkv_w0, rwkv_w2, rwkv_a0, rwkv_a2, rwkv_g2, rwkv_kk, rwkv_ka, rwkv_rk, rwkv_gn_g, rwkv_gn_b, w_br_a, w_br_b, w_br_c, w_out, ffn2_norm, ffn2_gate, ffn2_up, ffn2_down, final_norm):
    bp, S, d = x_prompt.shape
    nb, T_dec, _ = x_sample.shape
    depth = w_in.shape[0]
    assert bp == 1 and T_dec <= SEQ_PAD and S % SSD_CHUNK == 0
    n_p = bp * S
    n_rows = n_p + nb * SEQ_PAD
    xs_pad = jnp.concatenate([x_sample, jnp.zeros((nb, SEQ_PAD - T_dec, d), x_sample.dtype)], axis=1)
    h = jnp.concatenate([x_prompt.reshape(n_p, d), xs_pad.reshape(nb * SEQ_PAD, d)], axis=0)

    base = _t5_base(t5_table)
    e16 = jnp.repeat(jnp.eye(M_HEADS, dtype=F32), M_HEAD_DIM, axis=1)
    row = lambda x: x.reshape(1, -1).astype(F32)
    rc = min(RWKV_RC_CHUNK, S)
    shift_all = _shift_layout(state_shift).reshape(depth, nb, 1, 3 * R_WIDTH + LORA_W)
    outs = {k_: [] for k_ in ("k_p", "v_p", "ki_p", "conv_p", "sh_p", "k_s", "v_s", "ki_s", "conv_s", "sh_s")}
    ssm_p = ssm_s = wkv_p = wkv_s = None
    for l in range(depth):
        bf = lambda x: x.astype(BF16)
        h = _ffn(h, row(ffn1_norm[l]), bf(ffn1_gate[l]), bf(ffn1_up[l]), bf(ffn1_down[l]))
        p = _inproj(h, row(mix_norm[l]), _relayout_w_in(w_in[l]))
        kidx = _kidx_ln(p, row(kidx_ln_g[l]), row(kidx_ln_b[l]))

        ssd_par = (conv_w[l], row(conv_b[l]), row(dt_bias[l]), row(a_log[l]),
                   row(jnp.repeat(d_skip[l], M_HEAD_DIM)), row(ssm_norm[l]), e16)
        stack = dict(n_rows=n_rows, depth=depth, out_layer=l)
        ya, ssm_p = _ssd(p, 0, bp, S // SSD_CHUNK, SSD_CHUNK, SSD_CHUNK, 0,
                         jnp.zeros((1, bp, CONV_W - 1, CONV_DIM), F32),
                         jnp.zeros((1, bp, M_HEADS, M_HEAD_DIM, M_STATE), F32), *ssd_par,
                         st_buf=ssm_p, **stack)
        ya, ssm_s = _ssd(p, n_p, nb, 1, SEQ_PAD, T_dec, l, state_conv, state_ssm, *ssd_par,
                         y_buf=ya, st_buf=ssm_s, **stack)

        g2p = jnp.concatenate([rwkv_g2[l], jnp.zeros((LORA_W - R_DECAY_LORA - R_A_LORA - R_G_LORA,
                                                      R_WIDTH), F32)], axis=0)
        rw_par = (row(_shift_layout(rwkv_mu[l])), row(rwkv_w0[l]), rwkv_w2[l], row(rwkv_a0[l]),
                  rwkv_a2[l], g2p, row(rwkv_kk[l]), row(rwkv_ka[l]), row(rwkv_rk[l]),
                  row(rwkv_gn_g[l]), row(rwkv_gn_b[l]))
        sw = 3 * R_WIDTH + LORA_W
        yc, wkv_p = _rwkv(p, 0, bp, S // rc, rc, rc, 0, jnp.zeros((1, bp, 1, sw), F32),
                          jnp.zeros((1, bp, R_HEADS, R_HEAD, R_HEAD), F32), *rw_par,
                          st_buf=wkv_p, **stack)
        yc, wkv_s = _rwkv(p, n_p, nb, 1, SEQ_PAD, T_dec, l, shift_all, state_wkv, *rw_par,
                          y_buf=yc, st_buf=wkv_s, **stack)

        kp = p[:n_p, C_K:C_K + 256]
        vp = p[:n_p, C_V:C_V + 256]
        kit = bf(kidx[:n_p].T)
        kt = bf(kp.reshape(n_p, KV_HEADS, HEAD_DIM).transpose(1, 2, 0))
        vg = vp.reshape(n_p, KV_HEADS, HEAD_DIM).transpose(1, 0, 2)
        va = bf(jnp.concatenate([vg, jnp.ones((KV_HEADS, n_p, 1), F32),
                                 jnp.zeros((KV_HEADS, n_p, LANE - HEAD_DIM - 1), F32)], axis=-1))
        yb = _dsa_prompt(p, kit, kt, va, base, S, n_rows)
        yb = _dsa_sample(p, n_p, nb, kidx, base, l, cache_k, cache_v, cache_kidx, page_table, T_dec, yb)
        h = _merge(h, p, ya, yb, yc, bf(w_br_a[l]), bf(w_br_b[l]), bf(w_br_c[l]), bf(w_out[l]))
        h = _ffn(h, row(ffn2_norm[l]), bf(ffn2_gate[l]), bf(ffn2_up[l]), bf(ffn2_down[l]))

        ps = p[n_p:].reshape(nb, SEQ_PAD, P_W)[:, :T_dec]
        xbc = lambda t: jnp.concatenate([t[..., C_XS:C_XS + 1024], t[..., C_BC:C_BC + 1024]], axis=-1)
        prj = lambda t: jnp.concatenate([t[..., C_RR:C_RR + 3 * R_WIDTH],
                                         t[..., C_LORA:C_LORA + R_PROJ - 3 * R_WIDTH]], axis=-1)
        outs["k_p"].append(kp.reshape(bp, S, KV_HEADS, HEAD_DIM))
        outs["v_p"].append(vp.reshape(bp, S, KV_HEADS, HEAD_DIM))
        outs["ki_p"].append(kidx[:n_p].reshape(bp, S, IDX_DIM))
        outs["conv_p"].append(xbc(p[n_p - (CONV_W - 1):n_p]).reshape(bp, CONV_W - 1, CONV_DIM))
        outs["sh_p"].append(prj(p[n_p - 1:n_p]).reshape(bp, R_PROJ))
        outs["k_s"].append(ps[..., C_K:C_K + 256].reshape(nb, T_dec, KV_HEADS, HEAD_DIM))
        outs["v_s"].append(ps[..., C_V:C_V + 256].reshape(nb, T_dec, KV_HEADS, HEAD_DIM))
        outs["ki_s"].append(kidx[n_p:].reshape(nb, SEQ_PAD, IDX_DIM)[:, :T_dec])
        conv_full = jnp.concatenate([state_conv[l], xbc(ps)], axis=1)
        outs["conv_s"].append(conv_full[:, T_dec:])
        outs["sh_s"].append(prj(ps[:, T_dec - 1]))

    y = _final_norm(h, row(final_norm))
    y_prompt = y[:n_p].reshape(bp, S, d)
    y_sample = y[n_p:].reshape(nb, SEQ_PAD, d)[:, :T_dec]
    stk = lambda name: jnp.stack(outs[name], axis=0)
    return (y_prompt, y_sample, stk("k_p"), stk("v_p"), stk("ki_p"), ssm_p, stk("conv_p"),
            wkv_p, stk("sh_p"), stk("k_s"), stk("v_s"), stk("ki_s"), ssm_s,
            stk("conv_s"), wkv_s, stk("sh_s"))
```

```python
import functools
import math

import jax
import jax.numpy as jnp
from jax import lax
from jax.experimental import pallas as pl
from jax.experimental.pallas import tpu as pltpu

F32 = jnp.float32
BF16 = jnp.bfloat16
I32 = jnp.int32

D_MODEL = 2048
MIX_W = D_MODEL // 2
M_HEAD_DIM = 64
M_INNER = MIX_W
M_HEADS = M_INNER // M_HEAD_DIM
M_GROUPS = 4
M_STATE = 128
CONV_W = 4
CONV_DIM = M_INNER + 2 * M_GROUPS * M_STATE
SSD_CHUNK = 128
HEAD_DIM = 64
ATT_HEADS = MIX_W // HEAD_DIM
KV_HEADS = 4
GQA = ATT_HEADS // KV_HEADS
IDX_HEADS = 16
IDX_DIM = 64
TOPK_MAX = 256
T5_BUCKETS = 32
T5_MAX_DIST = 128
R_HEAD = 64
R_WIDTH = MIX_W
R_HEADS = R_WIDTH // R_HEAD
R_DECAY_LORA = 64
R_A_LORA = 64
R_G_LORA = 160
R_PROJ = 3 * R_WIDTH + R_DECAY_LORA + R_A_LORA + R_G_LORA
GN_EPS = 64e-5
D_FF = 256 * ((8 * D_MODEL // 3 + 255) // 256)
NORM_EPS = 1e-6
PAGE_SIZE = 128

SEQ_PAD = 8
LANE = 128
SUBLANES = 8
MXU_DIM = 256
VMEM_LIMIT = 56 * 2**20
INT_MIN = -2**31
NEG_BIG = -1e30

C_Z, C_XS, C_BC, C_Q, C_QI, C_RR, C_RK, C_RV = (i * 1024 for i in range(8))
C_GATE = 8192
C_LORA = C_GATE + 3 * D_MODEL
LORA_W = 512
C_K = C_LORA + LORA_W
C_V = C_K + 256
C_MISC = C_V + 256
P_W = C_MISC + LANE
MISC_DT = IDX_DIM
MISC_WI = IDX_DIM + M_HEADS
RWKV_RC_CHUNK = 64

HI = lax.Precision.HIGHEST


def _cparams(sem):
    return pltpu.CompilerParams(dimension_semantics=sem, vmem_limit_bytes=VMEM_LIMIT)


def _mm(a, b):
    return jnp.dot(a.astype(BF16), b.astype(BF16), preferred_element_type=F32)


def _mm_nt(a, b):
    return lax.dot_general(a.astype(BF16), b.astype(BF16), (((1,), (1,)), ((), ())),
                           preferred_element_type=F32)


def _mm_tn(a, b):
    return lax.dot_general(a.astype(BF16), b.astype(BF16), (((0,), (0,)), ((), ())),
                           preferred_element_type=F32)


def _mm_hi(a, b):
    return jnp.dot(a, b, precision=HI, preferred_element_type=F32)


def _mm_nt_hi(a, b):
    return lax.dot_general(a, b, (((1,), (1,)), ((), ())), precision=HI, preferred_element_type=F32)


def _sigmoid(x):
    return jax.nn.sigmoid(x)


def _silu(x):
    return x * jax.nn.sigmoid(x)


def _full(shape):
    nd = len(shape)
    return pl.BlockSpec(shape, lambda *_: (0,) * nd)


def _ffn_kernel(x_ref, g_ref, wg_ref, wu_ref, wd_ref, o_ref, n_ref):
    f = pl.program_id(1)

    @pl.when(f == 0)
    def _():
        x = x_ref[...]
        ms = jnp.mean(x * x, axis=-1, keepdims=True)
        n_ref[...] = (x * lax.rsqrt(ms + NORM_EPS) * g_ref[...]).astype(BF16)
        o_ref[...] = x

    n = n_ref[...]
    a = jnp.dot(n, wg_ref[...], preferred_element_type=F32)
    b = jnp.dot(n, wu_ref[...], preferred_element_type=F32)
    hid = (0.5 * _silu(a) * b).astype(BF16)
    o_ref[...] += jnp.dot(hid, wd_ref[...], preferred_element_type=F32)


def _ffn(x, g, wg, wu, wd, tm=512, tf=512):
    m = x.shape[0]
    tm = min(tm, m)
    return pl.pallas_call(
        _ffn_kernel,
        grid=(m // tm, D_FF // tf),
        in_specs=[pl.BlockSpec((tm, D_MODEL), lambda i, f: (i, 0)),
                  pl.BlockSpec((1, D_MODEL), lambda i, f: (0, 0)),
                  pl.BlockSpec((D_MODEL, tf), lambda i, f: (0, f)),
                  pl.BlockSpec((D_MODEL, tf), lambda i, f: (0, f)),
                  pl.BlockSpec((tf, D_MODEL), lambda i, f: (f, 0))],
        out_specs=pl.BlockSpec((tm, D_MODEL), lambda i, f: (i, 0)),
        out_shape=jax.ShapeDtypeStruct((m, D_MODEL), F32),
        scratch_shapes=[pltpu.VMEM((tm, D_MODEL), BF16)],
        compiler_params=_cparams(("parallel", "arbitrary")),
        name="ffn",
    )(x, g, wg, wu, wd)


def _inproj_kernel(x_ref, g_ref, w_ref, o_ref, n_ref):
    j = pl.program_id(1)

    @pl.when(j == 0)
    def _():
        x = x_ref[...]
        ms = jnp.mean(x * x, axis=-1, keepdims=True)
        n_ref[...] = (x * lax.rsqrt(ms + NORM_EPS) * g_ref[...]).astype(BF16)

    o_ref[...] = jnp.dot(n_ref[...], w_ref[...], preferred_element_type=F32)


def _inproj(x, g, w, tm=1024, tn=1408):
    m = x.shape[0]
    tm = tm if m % tm == 0 else min(512, m)
    return pl.pallas_call(
        _inproj_kernel,
        grid=(m // tm, P_W // tn),
        in_specs=[pl.BlockSpec((tm, D_MODEL), lambda i, j: (i, 0)),
                  pl.BlockSpec((1, D_MODEL), lambda i, j: (0, 0)),
                  pl.BlockSpec((D_MODEL, tn), lambda i, j: (0, j))],
        out_specs=pl.BlockSpec((tm, tn), lambda i, j: (i, j)),
        out_shape=jax.ShapeDtypeStruct((m, P_W), F32),
        scratch_shapes=[pltpu.VMEM((tm, D_MODEL), BF16)],
        compiler_params=_cparams(("parallel", "arbitrary")),
        name="inproj",
    )(x, g, w)


def _merge_kernel(h_ref, ya_ref, yb_ref, yc_ref, ga_ref, gb_ref, gc_ref,
                  wa_ref, wb_ref, wc_ref, wo_ref, o_ref):
    j = pl.program_id(1)

    @pl.when(j == 0)
    def _():
        o_ref[...] = h_ref[...]

    merged = (_sigmoid(ga_ref[...]) * jnp.dot(ya_ref[...], wa_ref[...], preferred_element_type=F32)
              + _sigmoid(gb_ref[...]) * jnp.dot(yb_ref[...], wb_ref[...], preferred_element_type=F32)
              + _sigmoid(gc_ref[...]) * jnp.dot(yc_ref[...], wc_ref[...], preferred_element_type=F32))
    o_ref[...] += jnp.dot(merged.astype(BF16), wo_ref[...], preferred_element_type=F32)


def _merge(h, p, ya, yb, yc, wa, wb, wc, wo, tm=512, tk=512):
    m = h.shape[0]
    tm = min(tm, m)
    gb0 = C_GATE // tk
    nk = D_MODEL // tk
    yspec = pl.BlockSpec((tm, MIX_W), lambda i, j: (i, 0))
    wspec = pl.BlockSpec((MIX_W, tk), lambda i, j: (0, j))
    return pl.pallas_call(
        _merge_kernel,
        grid=(m // tm, nk),
        in_specs=[pl.BlockSpec((tm, D_MODEL), lambda i, j: (i, 0)), yspec, yspec, yspec,
                  pl.BlockSpec((tm, tk), lambda i, j: (i, gb0 + j)),
                  pl.BlockSpec((tm, tk), lambda i, j: (i, gb0 + nk + j)),
                  pl.BlockSpec((tm, tk), lambda i, j: (i, gb0 + 2 * nk + j)),
                  wspec, wspec, wspec,
                  pl.BlockSpec((tk, D_MODEL), lambda i, j: (j, 0))],
        out_specs=pl.BlockSpec((tm, D_MODEL), lambda i, j: (i, 0)),
        out_shape=jax.ShapeDtypeStruct((m, D_MODEL), F32),
        compiler_params=_cparams(("parallel", "arbitrary")),
        name="merge",
    )(h, ya, yb, yc, p, p, p, wa, wb, wc, wo)


def _final_norm_kernel(x_ref, g_ref, o_ref):
    x = x_ref[...]
    ms = jnp.mean(x * x, axis=-1, keepdims=True)
    o_ref[...] = x * lax.rsqrt(ms + NORM_EPS) * g_ref[...]


def _final_norm(x, g, tm=512):
    m = x.shape[0]
    tm = min(tm, m)
    return pl.pallas_call(
        _final_norm_kernel,
        grid=(m // tm,),
        in_specs=[pl.BlockSpec((tm, D_MODEL), lambda i: (i, 0)), _full((1, D_MODEL))],
        out_specs=pl.BlockSpec((tm, D_MODEL), lambda i: (i, 0)),
        out_shape=jax.ShapeDtypeStruct((m, D_MODEL), F32),
        compiler_params=_cparams(("parallel",)),
        name="final_norm",
    )(x, g)


def _kidx_ln_kernel(m_ref, g_ref, b_ref, o_ref):
    x = m_ref[:, 0:IDX_DIM]
    mu = jnp.mean(x, axis=-1, keepdims=True)
    var = jnp.mean(jnp.square(x - mu), axis=-1, keepdims=True)
    o_ref[...] = (x - mu) * lax.rsqrt(var + NORM_EPS) * g_ref[...] + b_ref[...]


def _kidx_ln(p, g, b, tm=512):
    m = p.shape[0]
    tm = min(tm, m)
    return pl.pallas_call(
        _kidx_ln_kernel,
        grid=(m // tm,),
        in_specs=[pl.BlockSpec((tm, LANE), lambda i: (i, C_MISC // LANE)),
                  _full((1, IDX_DIM)), _full((1, IDX_DIM))],
        out_specs=pl.BlockSpec((tm, IDX_DIM), lambda i: (i, 0)),
        out_shape=jax.ShapeDtypeStruct((m, IDX_DIM), F32),
        compiler_params=_cparams(("parallel",)),
        name="kidx_ln",
    )(p, g, b)


def _shifted(x, k, prev, rows):
    y = pltpu.roll(x, k, 0)
    for r in range(k):
        y = jnp.where(rows == r, prev[3 - k + r:4 - k + r, :], y)
    return y


def _ssd_kernel(*refs, T, n_valid, n_alias):
    (z_ref, xs_ref, bc_ref, misc_ref, cst_ref, st0_ref, cw_ref, cb_ref, dtb_ref,
     alog_ref, dexp_ref, norm_ref, e16_ref) = refs[:13]
    y_ref, stout_ref, prev_ref, st_ref, yd_ref = refs[13 + n_alias:]
    c = pl.program_id(1)
    nc = pl.num_programs(1)

    @pl.when(c == 0)
    def _():
        prev_ref[...] = cst_ref[0, 0]
        st_ref[...] = st0_ref[0, 0]

    rows = lax.broadcasted_iota(I32, (T, 1), 0)
    cw = cw_ref[...]
    prev = prev_ref[...]

    def conv(x, lo, hi):
        pv = prev[:, lo:hi]
        y = cb_ref[:, lo:hi] + x * cw[3:4, lo:hi]
        for k in (1, 2, 3):
            y = y + _shifted(x, k, pv, rows) * cw[3 - k:4 - k, lo:hi]
        return _silu(y)

    xs_raw = xs_ref[...]
    bc_raw = bc_ref[...]
    xs = conv(xs_raw, 0, M_INNER)
    bc = conv(bc_raw, M_INNER, CONV_DIM)
    prev_ref[:, 0:M_INNER] = xs_raw[T - 3:T, :]
    prev_ref[:, M_INNER:CONV_DIM] = bc_raw[T - 3:T, :]

    dt = jax.nn.softplus(misc_ref[:, MISC_DT:MISC_DT + M_HEADS] + dtb_ref[...])
    if n_valid < T:
        dt = jnp.where(rows < n_valid, dt, 0.0)
    ad = dt * (-jnp.exp(alog_ref[...]))
    ri = lax.broadcasted_iota(I32, (T, T), 0)
    ci = lax.broadcasted_iota(I32, (T, T), 1)
    lower = ri >= ci
    tril = lower.astype(F32)
    triu = (ri <= ci).astype(F32)
    eye16 = (lax.broadcasted_iota(I32, (M_HEADS, M_HEADS), 0)
             == lax.broadcasted_iota(I32, (M_HEADS, M_HEADS), 1)).astype(F32)
    cs = _mm_hi(tril, ad)
    cst = _mm_hi(_mm_nt_hi(eye16, ad), triu)
    e16 = e16_ref[...]
    dt_e = _mm_hi(dt, e16)
    ecs_e = _mm_hi(jnp.exp(cs), e16)
    wl_e = _mm_hi(jnp.exp(cs[T - 1:T, :] - cs), e16)
    xd = xs * dt_e
    xdw = (xd * wl_e).astype(BF16)
    xd = xd.astype(BF16)
    bcb = bc.astype(BF16)
    for g in range(M_GROUPS):
        bg = bcb[:, g * M_STATE:(g + 1) * M_STATE]
        cg = bcb[:, (M_GROUPS + g) * M_STATE:(M_GROUPS + g + 1) * M_STATE]
        cbm = _mm_nt(cg, bg)
        for hh in range(M_HEADS // M_GROUPS):
            h = g * (M_HEADS // M_GROUPS) + hh
            hs = slice(h * M_HEAD_DIM, (h + 1) * M_HEAD_DIM)
            diff = cs[:, h:h + 1] - cst[h:h + 1, :]
            lm = jnp.exp(jnp.where(lower, diff, -jnp.inf))
            yd = _mm(cbm * lm, xd[:, hs])
            st = st_ref[h]
            yo = _mm_nt(cg, st)
            yd_ref[:, hs] = yd + yo * ecs_e[:, hs]
            upd = _mm_tn(xdw[:, hs], bg)
            dec = jnp.exp(cst[h:h + 1, T - 1:T])
            st_ref[h] = st * dec + upd

    y = yd_ref[...] + xs * dexp_ref[...]
    y = y * _silu(z_ref[...])
    gw = M_INNER // M_GROUPS
    for g in range(M_GROUPS):
        yg = y[:, g * gw:(g + 1) * gw]
        ms = jnp.mean(yg * yg, axis=-1, keepdims=True)
        y_ref[:, g * gw:(g + 1) * gw] = (yg * lax.rsqrt(ms + NORM_EPS)
                                         * norm_ref[:, g * gw:(g + 1) * gw]).astype(y_ref.dtype)

    @pl.when(c == nc - 1)
    def _():
        stout_ref[0, 0] = st_ref[...]


def _alias_inputs(n_in, bufs):
    specs, args, aliases = [], [], {}
    for k, b in enumerate(bufs):
        if b is not None:
            aliases[n_in + len(args)] = k
            specs.append(pl.BlockSpec(memory_space=pl.ANY))
            args.append(b)
    return specs, args, aliases


def _ssd(p, row0, n_seq, n_chunks, T, n_valid, layer, conv_state, ssm0, cw, cb, dtb, alog, dexp, norm,
         e16, *, n_rows, depth, out_layer, y_buf=None, st_buf=None):
    rb0 = row0 // T

    def rmap(col):
        return lambda s, c: (rb0 + s * n_chunks + c, col)

    args = [p, p, p, p, conv_state, ssm0, cw, cb, dtb, alog, dexp, norm, e16]
    a_specs, a_args, aliases = _alias_inputs(len(args), (y_buf, st_buf))
    kern = functools.partial(_ssd_kernel, T=T, n_valid=n_valid, n_alias=len(a_args))
    return pl.pallas_call(
        kern,
        grid=(n_seq, n_chunks),
        in_specs=[pl.BlockSpec((T, 1024), rmap(C_Z // 1024)),
                  pl.BlockSpec((T, 1024), rmap(C_XS // 1024)),
                  pl.BlockSpec((T, 1024), rmap(C_BC // 1024)),
                  pl.BlockSpec((T, LANE), rmap(C_MISC // LANE)),
                  pl.BlockSpec((1, 1, CONV_W - 1, CONV_DIM), lambda s, c: (layer, s, 0, 0)),
                  pl.BlockSpec((1, 1, M_HEADS, M_HEAD_DIM, M_STATE), lambda s, c: (layer, s, 0, 0, 0)),
                  _full((CONV_W, CONV_DIM)), _full((1, CONV_DIM)), _full((1, M_HEADS)),
                  _full((1, M_HEADS)), _full((1, M_INNER)), _full((1, M_INNER)),
                  _full((M_HEADS, M_INNER))] + a_specs,
        out_specs=[pl.BlockSpec((T, M_INNER), rmap(0)),
                   pl.BlockSpec((1, 1, M_HEADS, M_HEAD_DIM, M_STATE), lambda s, c: (out_layer, s, 0, 0, 0))],
        out_shape=[jax.ShapeDtypeStruct((n_rows, M_INNER), BF16),
                   jax.ShapeDtypeStruct((depth, n_seq, M_HEADS, M_HEAD_DIM, M_STATE), F32)],
        scratch_shapes=[pltpu.VMEM((CONV_W - 1, CONV_DIM), F32),
                        pltpu.VMEM((M_HEADS, M_HEAD_DIM, M_STATE), F32),
                        pltpu.VMEM((T, M_INNER), F32)],
        input_output_aliases=aliases,
        compiler_params=_cparams(("parallel", "arbitrary")),
        name="ssd",
    )(*args, *a_args)


def _rwkv_kernel(*refs, T, n_valid, HG, n_alias):
    (r_ref, k_ref, v_ref, lo_ref, sh0_ref, s0_ref, mu_ref, w0_ref, w2_ref, a0_ref,
     a2_ref, g2_ref, kkw_ref, kaw_ref, rk_ref, gng_ref, gnb_ref) = refs[:17]
    (y_ref, sout_ref, last_ref, s_ref, yb_ref, xs_ref, hr_ref,
     gm_ref) = refs[17 + n_alias:]
    c = pl.program_id(1)
    nc = pl.num_programs(1)

    @pl.when(c == 0)
    def _():
        last_ref[...] = sh0_ref[0, 0]
        s_ref[...] = s0_ref[0, 0]

    rows = lax.broadcasted_iota(I32, (T, 1), 0)

    def mix(ref, lo, hi):
        x = ref[...]
        prev = jnp.where(rows == 0, last_ref[:, lo:hi], pltpu.roll(x, 1, 0))
        last_ref[:, lo:hi] = x[T - 1:T, :]
        return x + (prev - x) * mu_ref[:, lo:hi]

    r = mix(r_ref, 0, 1024)
    k = mix(k_ref, 1024, 2048)
    v = mix(v_ref, 2048, 3072)
    lo = mix(lo_ref, 3072, 3072 + LORA_W)
    xw = lo[:, 0:R_DECAY_LORA]
    xa = lo[:, R_DECAY_LORA:R_DECAY_LORA + R_A_LORA]
    xg = lo[:, R_DECAY_LORA + R_A_LORA:LORA_W]
    wl = -jax.nn.softplus(-(w0_ref[...] + _mm(jnp.tanh(xw), w2_ref[...]))) - 0.5
    ld = -jnp.exp(wl)
    a = _sigmoid(a0_ref[...] + _mm(xa, a2_ref[...]))
    gate = _mm(_sigmoid(xg), g2_ref[...])
    kk = k * kkw_ref[...]
    km = k * (1.0 + (a - 1.0) * kaw_ref[...])
    if n_valid < T:
        ok = rows < n_valid
        ld = jnp.where(ok, ld, 0.0)
        kk = jnp.where(ok, kk, 0.0)
        km = jnp.where(ok, km, 0.0)
        v = jnp.where(ok, v, 0.0)

    ri = lax.broadcasted_iota(I32, (T, T), 0)
    ci = lax.broadcasted_iota(I32, (T, T), 1)
    cum = _mm_hi((ri >= ci).astype(F32), ld)
    eg = jnp.exp(cum)
    egm = jnp.exp(cum - ld)
    ei = jnp.exp(-cum)
    el = jnp.exp(cum[T - 1:T, :] - cum)
    r_eg = r * eg
    k_ei = km * ei
    k_el = km * el
    rk_sum = r * km * rk_ref[...]
    gng = gng_ref[...]
    gnb = gnb_ref[...]

    R = HG * T
    GW = HG * R_HEAD
    rr = lax.broadcasted_iota(I32, (R, R), 0)
    cc = lax.broadcasted_iota(I32, (R, R), 1)
    same = (rr // T) == (cc // T)
    strict = jnp.logical_and(same, rr > cc)
    incl = jnp.logical_and(same, rr >= cc)
    head_of = (lax.broadcasted_iota(I32, (R, GW), 0) // T) == (lax.broadcasted_iota(I32, (R, GW), 1) // R_HEAD)
    blk_r = (rr % T) // SUBLANES
    blk_c = (cc % T) // SUBLANES

    n_groups = R_HEADS // HG

    def head_slices(gi):
        return [slice((gi * HG + hh) * R_HEAD, (gi * HG + hh + 1) * R_HEAD) for hh in range(HG)]

    def stack(x, gi):
        return jnp.concatenate([x[:, s_] for s_ in head_slices(gi)], axis=0)

    def unit_kk(gi):
        kks = stack(kk, gi)
        nrm = jnp.sqrt(jnp.sum(kks * kks, axis=-1, keepdims=True))
        return kks / jnp.maximum(nrm, 1e-12)

    for gi in range(n_groups):
        kks = unit_kk(gi)
        q2 = jnp.concatenate([kks * stack(egm, gi), stack(r_eg, gi)], axis=0)
        k2 = jnp.concatenate([stack(k_ei, gi), kks * stack(a, gi) * stack(ei, gi)], axis=0)
        gm = _mm_nt(q2, k2)
        gm_ref[gi] = gm
        scat = s_ref[gi * HG:(gi + 1) * HG].reshape(GW, R_HEAD)
        hmw = _mm_nt(q2, scat)
        hk = jnp.concatenate([hmw[hh * T:(hh + 1) * T, hh * R_HEAD:(hh + 1) * R_HEAD]
                              for hh in range(HG)], axis=0)
        hr_ref[gi] = jnp.concatenate([hmw[R + hh * T:R + (hh + 1) * T, hh * R_HEAD:(hh + 1) * R_HEAD]
                                      for hh in range(HG)], axis=0)
        xs_ref[gi] = hk + _mm(jnp.where(strict, gm[0:R, 0:R], 0.0), stack(v, gi))

    for jb in range(T // SUBLANES):
        for gi in range(n_groups):
            for hh in range(HG):
                r0 = hh * T + jb * SUBLANES
                xb = xs_ref[gi, r0:r0 + SUBLANES, :]
                lrow = gm_ref[gi, r0:r0 + SUBLANES, R:2 * R]
                for s in range(1, SUBLANES):
                    below = lax.broadcasted_iota(I32, (SUBLANES, 1), 0) >= s
                    xb = xb - jnp.where(below, lrow[:, r0 + s - 1:r0 + s], 0.0) * xb[s - 1:s, :]
                xs_ref[gi, r0:r0 + SUBLANES, :] = xb
            if jb + 1 < T // SUBLANES:
                later = jnp.where(jnp.logical_and(jnp.logical_and(same, blk_c == jb), blk_r > jb),
                                  gm_ref[gi, 0:R, R:2 * R], 0.0)
                xs_ref[gi] = xs_ref[gi] - _mm(later, xs_ref[gi])

    for gi in range(n_groups):
        hsl = head_slices(gi)
        kks = unit_kk(gi)
        bs = kks * stack(a, gi)
        vs = stack(v, gi)
        u = -xs_ref[gi]
        ak = jnp.where(incl, gm_ref[gi, R:2 * R, 0:R], 0.0)
        ab = jnp.where(incl, gm_ref[gi, R:2 * R, R:2 * R], 0.0)
        scat = s_ref[gi * HG:(gi + 1) * HG].reshape(GW, R_HEAD)
        y = hr_ref[gi] + _mm(jnp.concatenate([ak, ab], axis=1), jnp.concatenate([vs, u], axis=0))
        vw = jnp.where(head_of, jnp.concatenate([v[:, gi * GW:(gi + 1) * GW]] * HG, axis=0), 0.0)
        uw = jnp.where(head_of, jnp.concatenate([u] * HG, axis=1), 0.0)
        upd = _mm_tn(jnp.concatenate([vw, uw], axis=0),
                     jnp.concatenate([stack(k_el, gi), bs * stack(el, gi)], axis=0))
        g_last = jnp.concatenate([jnp.broadcast_to(eg[T - 1:T, s_], (R_HEAD, R_HEAD)) for s_ in hsl],
                                 axis=0)
        s_ref[gi * HG:(gi + 1) * HG] = (scat * g_last + upd).reshape(HG, R_HEAD, R_HEAD)
        mu = jnp.mean(y, axis=-1, keepdims=True)
        var = jnp.mean(jnp.square(y - mu), axis=-1, keepdims=True)
        gs = jnp.concatenate([jnp.broadcast_to(gng[:, s_], (T, R_HEAD)) for s_ in hsl], axis=0)
        gb = jnp.concatenate([jnp.broadcast_to(gnb[:, s_], (T, R_HEAD)) for s_ in hsl], axis=0)
        yn = (y - mu) * lax.rsqrt(var + GN_EPS) * gs + gb
        yn = yn + jnp.sum(stack(rk_sum, gi), axis=-1, keepdims=True) * vs
        for hh in range(HG):
            yb_ref[:, hsl[hh]] = yn[hh * T:(hh + 1) * T]

    y_ref[...] = (yb_ref[...] * gate).astype(y_ref.dtype)

    @pl.when(c == nc - 1)
    def _():
        sout_ref[0, 0] = s_ref[...]


def _rwkv(p, row0, n_seq, n_chunks, T, n_valid, layer, shift0, s0, mu, w0, w2, a0, a2, g2, kkw, kaw,
          rk, gng, gnb, *, n_rows, depth, out_layer, y_buf=None, st_buf=None):
    rb0 = row0 // T

    def rmap(col):
        return lambda s, c: (rb0 + s * n_chunks + c, col)

    sw = 3072 + LORA_W
    hg = min(R_HEADS, max(1, MXU_DIM // T))
    args = [p, p, p, p, shift0, s0, mu, w0, w2, a0, a2, g2, kkw, kaw, rk, gng, gnb]
    a_specs, a_args, aliases = _alias_inputs(len(args), (y_buf, st_buf))
    kern = functools.partial(_rwkv_kernel, T=T, n_valid=n_valid, HG=hg, n_alias=len(a_args))
    return pl.pallas_call(
        kern,
        grid=(n_seq, n_chunks),
        in_specs=[pl.BlockSpec((T, 1024), rmap(C_RR // 1024)),
                  pl.BlockSpec((T, 1024), rmap(C_RK // 1024)),
                  pl.BlockSpec((T, 1024), rmap(C_RV // 1024)),
                  pl.BlockSpec((T, LORA_W), rmap(C_LORA // LORA_W)),
                  pl.BlockSpec((1, 1, 1, sw), lambda s, c: (layer, s, 0, 0)),
                  pl.BlockSpec((1, 1, R_HEADS, R_HEAD, R_HEAD), lambda s, c: (layer, s, 0, 0, 0)),
                  _full((1, sw)), _full((1, R_WIDTH)), _full((R_DECAY_LORA, R_WIDTH)),
                  _full((1, R_WIDTH)), _full((R_A_LORA, R_WIDTH)),
                  _full((LORA_W - R_DECAY_LORA - R_A_LORA, R_WIDTH)),
                  _full((1, R_WIDTH)), _full((1, R_WIDTH)), _full((1, R_WIDTH)),
                  _full((1, R_WIDTH)), _full((1, R_WIDTH))] + a_specs,
        out_specs=[pl.BlockSpec((T, R_WIDTH), rmap(0)),
                   pl.BlockSpec((1, 1, R_HEADS, R_HEAD, R_HEAD), lambda s, c: (out_layer, s, 0, 0, 0))],
        out_shape=[jax.ShapeDtypeStruct((n_rows, R_WIDTH), BF16),
                   jax.ShapeDtypeStruct((depth, n_seq, R_HEADS, R_HEAD, R_HEAD), F32)],
        scratch_shapes=[pltpu.VMEM((1, sw), F32),
                        pltpu.VMEM((R_HEADS, R_HEAD, R_HEAD), F32),
                        pltpu.VMEM((T, R_WIDTH), F32),
                        pltpu.VMEM((R_HEADS // hg, hg * T, R_HEAD), F32),
                        pltpu.VMEM((R_HEADS // hg, hg * T, R_HEAD), F32),
                        pltpu.VMEM((R_HEADS // hg, 2 * hg * T, 2 * hg * T), F32)],
        input_output_aliases=aliases,
        compiler_params=_cparams(("parallel", "arbitrary")),
        name="rwkv",
    )(*args, *a_args)


def _score_keys(acc, causal):
    bits = pltpu.bitcast(acc, I32)
    key = bits ^ ((bits >> 31) & 0x7FFFFFFF)
    return jnp.where(causal, key, INT_MIN)


def _kth_threshold(count_ge, shape, top, bits_per_step=1):
    tb = jnp.zeros(shape, I32)
    for shift in range(32 - bits_per_step, -1, -bits_per_step):
        digit = jnp.zeros(shape, I32)
        for d in range(1, 1 << bits_per_step):
            inc = d << shift
            inc = inc - (1 << 32) if inc >= (1 << 31) else inc
            cnt = count_ge((tb | jnp.int32(inc)) ^ jnp.int32(INT_MIN))
            digit = digit + jnp.where(cnt >= top, 1, 0)
        tb = tb | (digit << shift)
    return jnp.maximum(tb ^ jnp.int32(INT_MIN), INT_MIN + 1)


def _toeplitz_bias(base_row, nrows):
    return pltpu.roll(jnp.broadcast_to(base_row, (nrows, 2 * LANE)), 0, 1, stride=1, stride_axis=0)


def _dsa_prompt_kernel(q_ref, qi_ref, misc_ref, kit_ref, kt_ref, va_ref, base_ref, o_ref,
                       key_ref, mx_ref, acc_ref, near_ref, *, top, TA):
    i = pl.program_id(0)
    QB = LANE
    q0 = i * QB
    n_a = (q0 + QB + TA - 1) // TA
    wi = misc_ref[:, MISC_WI:MISC_WI + IDX_HEADS] * (IDX_HEADS ** -0.5 * IDX_DIM ** -0.5)
    qi = qi_ref[...]
    qis = jnp.concatenate([qi[:, h * IDX_DIM:(h + 1) * IDX_DIM] for h in range(IDX_HEADS)],
                          axis=0).astype(BF16)
    rowq = q0 + lax.broadcasted_iota(I32, (QB, 1), 0)

    def score_tile(kt, carry):
        off = pl.multiple_of(kt * TA, TA)
        s = jnp.maximum(jnp.dot(qis, kit_ref[:, pl.ds(off, TA)], preferred_element_type=F32), 0.0)
        acc = s[0:QB] * wi[:, 0:1]
        for h in range(1, IDX_HEADS):
            acc = acc + s[h * QB:(h + 1) * QB] * wi[:, h:h + 1]
        col = off + lax.broadcasted_iota(I32, (1, TA), 1)
        key_ref[:, pl.ds(off, TA)] = _score_keys(acc, col <= rowq)
        return carry

    lax.fori_loop(0, n_a, score_tile, 0)

    def count_ge(cand):
        def body(kt, cnt):
            off = pl.multiple_of(kt * TA, TA)
            ge = jnp.where(key_ref[:, pl.ds(off, TA)] >= cand, 1.0, 0.0)
            for j in range(TA // LANE):
                cnt = cnt + ge[:, j * LANE:(j + 1) * LANE]
            return cnt
        cnt = lax.fori_loop(0, n_a, body, jnp.zeros((QB, LANE), F32))
        return jnp.sum(cnt, axis=-1, keepdims=True)

    thr = _kth_threshold(count_ge, (QB, 1), float(top))

    n_ge = count_ge(thr)

    @pl.when(jnp.max(n_ge) > top)
    def _():
        def count_tied_before(pos):
            def body(kt, cnt):
                off = pl.multiple_of(kt * TA, TA)
                col = off + lax.broadcasted_iota(I32, (1, TA), 1)
                hit = jnp.where(key_ref[:, pl.ds(off, TA)] == thr, jnp.where(col < pos, 1.0, 0.0), 0.0)
                for j in range(TA // LANE):
                    cnt = cnt + hit[:, j * LANE:(j + 1) * LANE]
                return cnt
            cnt = lax.fori_loop(0, n_a, body, jnp.zeros((QB, LANE), F32))
            return jnp.sum(cnt, axis=-1, keepdims=True)

        need = top - count_ge(thr + 1)
        last = jnp.zeros((QB, 1), I32)
        for bit in range(max(1, (kit_ref.shape[1] - 1).bit_length()) - 1, -1, -1):
            cand = last | jnp.int32(1 << bit)
            last = jnp.where(count_tied_before(cand) < need, cand, last)

        def drop(kt, carry):
            off = pl.multiple_of(kt * TA, TA)
            col = off + lax.broadcasted_iota(I32, (1, TA), 1)
            k = key_ref[:, pl.ds(off, TA)]
            key_ref[:, pl.ds(off, TA)] = jnp.where(k == thr, jnp.where(col > last, INT_MIN, k), k)
            return carry

        lax.fori_loop(0, n_a, drop, 0)

    @pl.when(i == 0)
    def _():
        for g in range(KV_HEADS):
            near_ref[g] = jnp.concatenate(
                [_toeplitz_bias(base_ref[h:h + 1, :], QB) - base_ref[h:h + 1, 2 * LANE - 1:2 * LANE]
                 for h in range(g * GQA, (g + 1) * GQA)], axis=0)

    q = q_ref[...]
    lim = (i - 1) * QB
    n_far = (jnp.maximum(lim, 0) + TA - 1) // TA
    qgs = []
    for g in range(KV_HEADS):
        heads = range(g * GQA, (g + 1) * GQA)
        qgs.append((jnp.concatenate([q[:, h * HEAD_DIM:(h + 1) * HEAD_DIM] for h in heads], axis=0)
                    * HEAD_DIM ** -0.5).astype(BF16))

    def far_logits(g, off, msk):
        s = jnp.dot(qgs[g], kt_ref[g, :, pl.ds(off, TA)], preferred_element_type=F32)
        return (s.reshape(GQA, QB, TA) + msk[None]).reshape(GQA * QB, TA)

    def far_mask(off):
        col = off + lax.broadcasted_iota(I32, (1, TA), 1)
        return jnp.where(key_ref[:, pl.ds(off, TA)] >= thr,
                         jnp.where(col < lim, 0.0, NEG_BIG), NEG_BIG)

    def near_logits(g, off, lo, width, msk):
        s = (jnp.dot(qgs[g], kt_ref[g, :, pl.ds(off, width)], preferred_element_type=F32)
             + near_ref[g, :, lo:lo + width])
        return (s.reshape(GQA, QB, width) + msk[None]).reshape(GQA * QB, width)

    def near_mask(off, width):
        return jnp.where(key_ref[:, pl.ds(off, width)] >= thr, 0.0, NEG_BIG)

    mx_ref[...] = jnp.full(mx_ref.shape, NEG_BIG, F32)
    acc_ref[...] = jnp.zeros(acc_ref.shape, F32)

    def accumulate(g, s, v_tile):
        width = s.shape[1]
        m = s[:, 0:LANE]
        for j in range(1, width // LANE):
            m = jnp.maximum(m, s[:, j * LANE:(j + 1) * LANE])
        m_old = mx_ref[g]
        m_new = jnp.maximum(m_old, jnp.broadcast_to(jnp.max(m, axis=-1, keepdims=True), m.shape))
        pr = jnp.exp((s - jnp.concatenate([m_new] * (width // LANE), axis=1)).astype(BF16))
        acc_ref[g] = (acc_ref[g] * jnp.exp(m_old - m_new)
                      + jnp.dot(pr, v_tile, preferred_element_type=F32))
        mx_ref[g] = m_new

    def far_acc(kt, carry):
        off = pl.multiple_of(kt * TA, TA)
        msk = far_mask(off)
        for g in range(KV_HEADS):
            accumulate(g, far_logits(g, off, msk), va_ref[g, pl.ds(off, TA), :])
        return carry

    def near_acc(kt, lo, width):
        off = pl.multiple_of(kt * LANE, LANE)
        msk = near_mask(off, width)
        for g in range(KV_HEADS):
            accumulate(g, near_logits(g, off, lo, width, msk), va_ref[g, pl.ds(off, width), :])

    lax.fori_loop(0, n_far, far_acc, 0)

    @pl.when(i == 0)
    def _():
        near_acc(0, LANE, LANE)

    @pl.when(i >= 1)
    def _():
        near_acc(i - 1, 0, 2 * LANE)
    for g in range(KV_HEADS):
        acc = acc_ref[g]
        out = acc[:, 0:HEAD_DIM] / acc[:, HEAD_DIM:HEAD_DIM + 1]
        for hh in range(GQA):
            h = g * GQA + hh
            o_ref[:, h * HEAD_DIM:(h + 1) * HEAD_DIM] = out[hh * QB:(hh + 1) * QB].astype(o_ref.dtype)


def _dsa_prompt(p, kit, kt, va, base, S, n_rows):
    top = min(TOPK_MAX, S // 4)
    TA = min(512, S)
    kern = functools.partial(_dsa_prompt_kernel, top=top, TA=TA)
    return pl.pallas_call(
        kern,
        grid=(S // LANE,),
        in_specs=[pl.BlockSpec((LANE, 1024), lambda i: (i, C_Q // 1024)),
                  pl.BlockSpec((LANE, 1024), lambda i: (i, C_QI // 1024)),
                  pl.BlockSpec((LANE, LANE), lambda i: (i, C_MISC // LANE)),
                  _full((IDX_DIM, S)), _full((KV_HEADS, HEAD_DIM, S)), _full((KV_HEADS, S, LANE)),
                  _full((ATT_HEADS, 2 * LANE))],
        out_specs=pl.BlockSpec((LANE, MIX_W), lambda i: (i, 0)),
        out_shape=jax.ShapeDtypeStruct((n_rows, MIX_W), BF16),
        scratch_shapes=[pltpu.VMEM((LANE, S), I32),
                        pltpu.VMEM((KV_HEADS, GQA * LANE, LANE), F32),
                        pltpu.VMEM((KV_HEADS, GQA * LANE, LANE), F32),
                        pltpu.VMEM((KV_HEADS, GQA * LANE, 2 * LANE), F32)],
        compiler_params=_cparams(("arbitrary",)),
        name="dsa_prompt",
    )(p, p, p, kit, kt, va, base)


def _dsa_sample_kernel(pt_ref, q_ref, qi_ref, misc_ref, kn_ref, vn_ref, kin_ref, base_ref, *rest,
                       n_pages, n_valid, top):
    ki_pages = rest[0:n_pages]
    k_pages = rest[n_pages:2 * n_pages]
    v_pages = rest[2 * n_pages:3 * n_pages]
    o_ref = rest[3 * n_pages + 1]
    kib_ref, kb_ref, vb_ref, msk_ref = rest[3 * n_pages + 2:]
    TQ = SEQ_PAD
    lo = n_pages * LANE
    NK = lo + LANE
    wi = misc_ref[:, MISC_WI:MISC_WI + IDX_HEADS] * (IDX_HEADS ** -0.5 * IDX_DIM ** -0.5)
    qi = qi_ref[...]
    qis = jnp.concatenate([qi[:, h * IDX_DIM:(h + 1) * IDX_DIM] for h in range(IDX_HEADS)], axis=0)

    eye = (lax.broadcasted_iota(I32, (HEAD_DIM, HEAD_DIM), 0)
           == lax.broadcasted_iota(I32, (HEAD_DIM, HEAD_DIM), 1)).astype(BF16)

    def new_slot(x):
        xt = _mm_nt(eye, x)
        return jnp.concatenate([xt, jnp.zeros((HEAD_DIM, LANE - TQ), F32)], axis=1).astype(BF16)

    for t in range(n_pages):
        kib_ref[:, t * LANE:(t + 1) * LANE] = ki_pages[t][0, 0].astype(BF16)
        for g in range(KV_HEADS):
            kb_ref[g, :, t * LANE:(t + 1) * LANE] = k_pages[t][0, 0, g].astype(BF16)
            vb_ref[g, :, t * LANE:(t + 1) * LANE] = v_pages[t][0, 0, g].astype(BF16)
    kib_ref[:, lo:NK] = new_slot(kin_ref[...])
    kn = kn_ref[...]
    vn = vn_ref[...]
    for g in range(KV_HEADS):
        kb_ref[g, :, lo:NK] = new_slot(kn[:, g * HEAD_DIM:(g + 1) * HEAD_DIM])
        vb_ref[g, :, lo:NK] = new_slot(vn[:, g * HEAD_DIM:(g + 1) * HEAD_DIM])

    s = jnp.maximum(_mm(qis, kib_ref[...]), 0.0)
    acc = s[0:TQ] * wi[:, 0:1]
    for h in range(1, IDX_HEADS):
        acc = acc + s[h * TQ:(h + 1) * TQ] * wi[:, h:h + 1]
    trow = lax.broadcasted_iota(I32, (TQ, 1), 0)
    new = lax.broadcasted_iota(I32, (1, NK), 1) - lo
    keys = _score_keys(acc, new <= jnp.minimum(trow, n_valid - 1))

    def count_ge(cand):
        return jnp.sum(jnp.where(keys >= cand, 1.0, 0.0), axis=-1, keepdims=True)

    thr = _kth_threshold(count_ge, (TQ, 1), float(top), bits_per_step=4)
    msk_ref[...] = jnp.where(keys >= thr, 0.0, NEG_BIG)

    @pl.when(jnp.max(jnp.where(trow < n_valid, count_ge(thr), 0.0)) > top)
    def _():
        pos = new + lo
        tied = keys == thr

        def count_tied_before(p):
            return jnp.sum(jnp.where(tied, jnp.where(pos < p, 1.0, 0.0), 0.0), axis=-1, keepdims=True)

        need = top - count_ge(thr + 1)
        last = jnp.zeros((TQ, 1), I32)
        for bit in range(max(1, (NK - 1).bit_length()) - 1, -1, -1):
            cand = last | jnp.int32(1 << bit)
            last = jnp.where(count_tied_before(cand) < need, cand, last)
        msk_ref[...] = jnp.where(tied, jnp.where(pos > last, NEG_BIG, 0.0), msk_ref[...])

    msk = jnp.concatenate([msk_ref[...]] * ATT_HEADS, axis=0)

    q = q_ref[...] * HEAD_DIM ** -0.5
    zero = jnp.zeros((TQ, HEAD_DIM), F32)
    q_rows, nears = [], []
    for h in range(ATT_HEADS):
        qh = q[:, h * HEAD_DIM:(h + 1) * HEAD_DIM]
        q_rows.append(jnp.concatenate([qh if g == h // GQA else zero for g in range(KV_HEADS)], axis=1))
        nears.append(_toeplitz_bias(base_ref[h:h + 1, :], TQ) - base_ref[h:h + 1, 2 * LANE - 1:2 * LANE])
    qbd = jnp.concatenate(q_rows, axis=0)
    bias = jnp.concatenate([jnp.zeros((ATT_HEADS * TQ, lo - LANE), F32),
                            jnp.concatenate(nears, axis=0)], axis=1)
    s = _mm(qbd, kb_ref[...].reshape(KV_HEADS * HEAD_DIM, NK)) + bias + msk
    mx = jnp.max(s, axis=-1, keepdims=True)
    pr = jnp.exp(s - mx)
    pr = pr / jnp.sum(pr, axis=-1, keepdims=True)
    out = _mm_nt(pr, vb_ref[...].reshape(KV_HEADS * HEAD_DIM, NK))
    for h in range(ATT_HEADS):
        g = h // GQA
        o_ref[:, h * HEAD_DIM:(h + 1) * HEAD_DIM] = out[h * TQ:(h + 1) * TQ,
                                                        g * HEAD_DIM:(g + 1) * HEAD_DIM].astype(o_ref.dtype)


def _dsa_sample(p, row0, n_seq, kidx_ln, base, layer, cache_k, cache_v, cache_kidx, page_table, n_valid,
                y_buf):
    n_pages = page_table.shape[1]
    past = n_pages * PAGE_SIZE
    top = min(TOPK_MAX, (past + n_valid) // 4)
    rb0 = row0 // SEQ_PAD
    ck = jnp.transpose(cache_k, (0, 1, 3, 4, 2))
    cv = jnp.transpose(cache_v, (0, 1, 3, 4, 2))
    cki = jnp.transpose(cache_kidx, (0, 1, 3, 2))

    def rmap(col):
        return lambda b, pt: (rb0 + b, col)

    def pmap(pg):
        return lambda b, pt: (layer, pt[b, pg], 0, 0)

    def pmap5(pg):
        return lambda b, pt: (layer, pt[b, pg], 0, 0, 0)

    in_specs = [pl.BlockSpec((SEQ_PAD, 1024), rmap(C_Q // 1024)),
                pl.BlockSpec((SEQ_PAD, 1024), rmap(C_QI // 1024)),
                pl.BlockSpec((SEQ_PAD, LANE), rmap(C_MISC // LANE)),
                pl.BlockSpec((SEQ_PAD, 256), rmap(C_K // 256)),
                pl.BlockSpec((SEQ_PAD, 256), rmap(C_V // 256)),
                pl.BlockSpec((SEQ_PAD, IDX_DIM), lambda b, pt: (rb0 + b, 0)),
                pl.BlockSpec((ATT_HEADS, 2 * LANE), lambda b, pt: (0, 0))]
    in_specs += [pl.BlockSpec((1, 1, IDX_DIM, PAGE_SIZE), pmap(pg)) for pg in range(n_pages)]
    in_specs += [pl.BlockSpec((1, 1, KV_HEADS, HEAD_DIM, PAGE_SIZE), pmap5(pg)) for pg in range(n_pages)]
    in_specs += [pl.BlockSpec((1, 1, KV_HEADS, HEAD_DIM, PAGE_SIZE), pmap5(pg)) for pg in range(n_pages)]
    kern = functools.partial(_dsa_sample_kernel, n_pages=n_pages, n_valid=n_valid, top=top)
    nk = (n_pages + 1) * LANE
    return pl.pallas_call(
        kern,
        grid_spec=pltpu.PrefetchScalarGridSpec(
            num_scalar_prefetch=1,
            grid=(n_seq,),
            in_specs=in_specs + [pl.BlockSpec(memory_space=pl.ANY)],
            out_specs=pl.BlockSpec((SEQ_PAD, MIX_W), lambda b, pt: (rb0 + b, 0)),
            scratch_shapes=[pltpu.VMEM((IDX_DIM, nk), BF16),
                            pltpu.VMEM((KV_HEADS, HEAD_DIM, nk), BF16),
                            pltpu.VMEM((KV_HEADS, HEAD_DIM, nk), BF16),
                            pltpu.VMEM((SEQ_PAD, nk), F32)]),
        out_shape=jax.ShapeDtypeStruct(y_buf.shape, BF16),
        input_output_aliases={1 + len(in_specs): 0},
        compiler_params=_cparams(("arbitrary",)),
        name="dsa_sample",
    )(page_table, p, p, p, p, p, kidx_ln, base,
      *([cki] * n_pages), *([ck] * n_pages), *([cv] * n_pages), y_buf)


def _t5_base(t5_table):
    x = jnp.arange(2 * LANE, dtype=jnp.int32)
    rel = jnp.where(x <= LANE, LANE - x, T5_MAX_DIST)
    n = jnp.maximum(rel, 0)
    exact = T5_BUCKETS // 2
    nf = jnp.maximum(n, exact).astype(F32)
    large = exact + (jnp.log(nf / exact) / math.log(T5_MAX_DIST / exact)
                     * (T5_BUCKETS - exact)).astype(jnp.int32)
    bucket = jnp.where(n < exact, n, jnp.minimum(large, T5_BUCKETS - 1))
    return t5_table[bucket].astype(F32).T


def _relayout_w_in(w):
    widths = (M_INNER, CONV_DIM, M_HEADS, ATT_HEADS * HEAD_DIM, KV_HEADS * HEAD_DIM,
              KV_HEADS * HEAD_DIM, IDX_HEADS * IDX_DIM, IDX_DIM, IDX_HEADS, R_PROJ, 3 * D_MODEL)
    offs = [0]
    for wd in widths:
        offs.append(offs[-1] + wd)
    z, xbc, dt, q, k, v, qi, ki, wi, pr, gates = (w[:, offs[i]:offs[i + 1]] for i in range(len(widths)))
    zeros = lambda n: jnp.zeros((w.shape[0], n), w.dtype)
    lora = pr[:, 3 * R_WIDTH:]
    cols = [z, xbc, q, qi, pr[:, :3 * R_WIDTH], gates,
            lora, zeros(LORA_W - lora.shape[1]), k, v,
            ki, dt, wi, zeros(LANE - IDX_DIM - M_HEADS - IDX_HEADS)]
    out = jnp.concatenate(cols, axis=1).astype(BF16)
    assert out.shape[1] == P_W
    return out


def _pad_cols(x, n):
    return jnp.concatenate([x, jnp.zeros(x.shape[:-1] + (n - x.shape[-1],), x.dtype)], axis=-1)


def _shift_layout(x):
    return _pad_cols(x, 3 * R_WIDTH + LORA_W)


def kernel(x_prompt, x_sample, cache_k, cache_v, cache_kidx, page_table, state_ssm, state_conv, state_wkv, state_shift, ffn1_norm, ffn1_gate, ffn1_up, ffn1_down, mix_norm, w_in, conv_w, conv_b, dt_bias, a_log, d_skip, ssm_norm, kidx_ln_g, kidx_ln_b, t5_table, rwkv_mu, rwkv_w0, rwkv_w2, rwkv_a0, rwkv_a2, rwkv_g2, rwkv_kk, rwkv_ka, rwkv_rk, rwkv_gn_g, rwkv_gn_b, w_br_a, w_br_b, w_br_c, w_out, ffn2_norm, ffn2_gate, ffn2_up, ffn2_down, final_norm):
    bp, S, d = x_prompt.shape
    nb, T_dec, _ = x_sample.shape
    depth = w_in.shape[0]
    assert bp == 1 and T_dec <= SEQ_PAD and S % SSD_CHUNK == 0
    n_p = bp * S
    n_rows = n_p + nb * SEQ_PAD
    xs_pad = jnp.concatenate([x_sample, jnp.zeros((nb, SEQ_PAD - T_dec, d), x_sample.dtype)], axis=1)
    h = jnp.concatenate([x_prompt.reshape(n_p, d), xs_pad.reshape(nb * SEQ_PAD, d)], axis=0)

    base = _t5_base(t5_table)
    e16 = jnp.repeat(jnp.eye(M_HEADS, dtype=F32), M_HEAD_DIM, axis=1)
    row = lambda x: x.reshape(1, -1).astype(F32)
    rc = min(RWKV_RC_CHUNK, S)
    shift_all = _shift_layout(state_shift).reshape(depth, nb, 1, 3 * R_WIDTH + LORA_W)
    outs = {k_: [] for k_ in ("k_p", "v_p", "ki_p", "conv_p", "sh_p", "k_s", "v_s", "ki_s", "conv_s", "sh_s")}
    ssm_p = ssm_s = wkv_p = wkv_s = None
    for l in range(depth):
        bf = lambda x: x.astype(BF16)
        h = _ffn(h, row(ffn1_norm[l]), bf(ffn1_gate[l]), bf(ffn1_up[l]), bf(ffn1_down[l]))
        p = _inproj(h, row(mix_norm[l]), _relayout_w_in(w_in[l]))
        kidx = _kidx_ln(p, row(kidx_ln_g[l]), row(kidx_ln_b[l]))

        ssd_par = (conv_w[l], row(conv_b[l]), row(dt_bias[l]), row(a_log[l]),
                   row(jnp.repeat(d_skip[l], M_HEAD_DIM)), row(ssm_norm[l]), e16)
        stack = dict(n_rows=n_rows, depth=depth, out_layer=l)
        ya, ssm_p = _ssd(p, 0, bp, S // SSD_CHUNK, SSD_CHUNK, SSD_CHUNK, 0,
                         jnp.zeros((1, bp, CONV_W - 1, CONV_DIM), F32),
                         jnp.zeros((1, bp, M_HEADS, M_HEAD_DIM, M_STATE), F32), *ssd_par,
                         st_buf=ssm_p, **stack)
        ya, ssm_s = _ssd(p, n_p, nb, 1, SEQ_PAD, T_dec, l, state_conv, state_ssm, *ssd_par,
                         y_buf=ya, st_buf=ssm_s, **stack)

        g2p = jnp.concatenate([rwkv_g2[l], jnp.zeros((LORA_W - R_DECAY_LORA - R_A_LORA - R_G_LORA,
                                                      R_WIDTH), F32)], axis=0)
        rw_par = (row(_shift_layout(rwkv_mu[l])), row(rwkv_w0[l]), rwkv_w2[l], row(rwkv_a0[l]),
                  rwkv_a2[l], g2p, row(rwkv_kk[l]), row(rwkv_ka[l]), row(rwkv_rk[l]),
                  row(rwkv_gn_g[l]), row(rwkv_gn_b[l]))
        sw = 3 * R_WIDTH + LORA_W
        yc, wkv_p = _rwkv(p, 0, bp, S // rc, rc, rc, 0, jnp.zeros((1, bp, 1, sw), F32),
                          jnp.zeros((1, bp, R_HEADS, R_HEAD, R_HEAD), F32), *rw_par,
                          st_buf=wkv_p, **stack)
        yc, wkv_s = _rwkv(p, n_p, nb, 1, SEQ_PAD, T_dec, l, shift_all, state_wkv, *rw_par,
                          y_buf=yc, st_buf=wkv_s, **stack)

        kp = p[:n_p, C_K:C_K + 256]
        vp = p[:n_p, C_V:C_V + 256]
        kit = bf(kidx[:n_p].T)
        kt = bf(kp.reshape(n_p, KV_HEADS, HEAD_DIM).transpose(1, 2, 0))
        vg = vp.reshape(n_p, KV_HEADS, HEAD_DIM).transpose(1, 0, 2)
        va = bf(jnp.concatenate([vg, jnp.ones((KV_HEADS, n_p, 1), F32),
                                 jnp.zeros((KV_HEADS, n_p, LANE - HEAD_DIM - 1), F32)], axis=-1))
        yb = _dsa_prompt(p, kit, kt, va, base, S, n_rows)
        yb = _dsa_sample(p, n_p, nb, kidx, base, l, cache_k, cache_v, cache_kidx, page_table, T_dec, yb)
        h = _merge(h, p, ya, yb, yc, bf(w_br_a[l]), bf(w_br_b[l]), bf(w_br_c[l]), bf(w_out[l]))
        h = _ffn(h, row(ffn2_norm[l]), bf(ffn2_gate[l]), bf(ffn2_up[l]), bf(ffn2_down[l]))

        ps = p[n_p:].reshape(nb, SEQ_PAD, P_W)[:, :T_dec]
        xbc = lambda t: jnp.concatenate([t[..., C_XS:C_XS + 1024], t[..., C_BC:C_BC + 1024]], axis=-1)
        prj = lambda t: jnp.concatenate([t[..., C_RR:C_RR + 3 * R_WIDTH],
                                         t[..., C_LORA:C_LORA + R_PROJ - 3 * R_WIDTH]], axis=-1)
        outs["k_p"].append(kp.reshape(bp, S, KV_HEADS, HEAD_DIM))
        outs["v_p"].append(vp.reshape(bp, S, KV_HEADS, HEAD_DIM))
        outs["ki_p"].append(kidx[:n_p].reshape(bp, S, IDX_DIM))
        outs["conv_p"].append(xbc(p[n_p - (CONV_W - 1):n_p]).reshape(bp, CONV_W - 1, CONV_DIM))
        outs["sh_p"].append(prj(p[n_p - 1:n_p]).reshape(bp, R_PROJ))
        outs["k_s"].append(ps[..., C_K:C_K + 256].reshape(nb, T_dec, KV_HEADS, HEAD_DIM))
        outs["v_s"].append(ps[..., C_V:C_V + 256].reshape(nb, T_dec, KV_HEADS, HEAD_DIM))
        outs["ki_s"].append(kidx[n_p:].reshape(nb, SEQ_PAD, IDX_DIM)[:, :T_dec])
        conv_full = jnp.concatenate([state_conv[l], xbc(ps)], axis=1)
        outs["conv_s"].append(conv_full[:, T_dec:])
        outs["sh_s"].append(prj(ps[:, T_dec - 1]))

    y = _final_norm(h, row(final_norm))
    y_prompt = y[:n_p].reshape(bp, S, d)
    y_sample = y[n_p:].reshape(nb, SEQ_PAD, d)[:, :T_dec]
    stk = lambda name: jnp.stack(outs[name], axis=0)
    return (y_prompt, y_sample, stk("k_p"), stk("v_p"), stk("ki_p"), ssm_p, stk("conv_p"),
            wkv_p, stk("sh_p"), stk("k_s"), stk("v_s"), stk("ki_s"), ssm_s,
            stk("conv_s"), wkv_s, stk("sh_s"))
```

```python
import functools
import math

import jax
import jax.numpy as jnp
from jax import lax
from jax.experimental import pallas as pl
from jax.experimental.pallas import tpu as pltpu

F32 = jnp.float32
BF16 = jnp.bfloat16
I32 = jnp.int32

D_MODEL = 2048
MIX_W = D_MODEL // 2
M_HEAD_DIM = 64
M_INNER = MIX_W
M_HEADS = M_INNER // M_HEAD_DIM
M_GROUPS = 4
M_STATE = 128
CONV_W = 4
CONV_DIM = M_INNER + 2 * M_GROUPS * M_STATE
SSD_CHUNK = 128
HEAD_DIM = 64
ATT_HEADS = MIX_W // HEAD_DIM
KV_HEADS = 4
GQA = ATT_HEADS // KV_HEADS
IDX_HEADS = 16
IDX_DIM = 64
TOPK_MAX = 256
T5_BUCKETS = 32
T5_MAX_DIST = 128
R_HEAD = 64
R_WIDTH = MIX_W
R_HEADS = R_WIDTH // R_HEAD
R_DECAY_LORA = 64
R_A_LORA = 64
R_G_LORA = 160
R_PROJ = 3 * R_WIDTH + R_DECAY_LORA + R_A_LORA + R_G_LORA
GN_EPS = 64e-5
D_FF = 256 * ((8 * D_MODEL // 3 + 255) // 256)
NORM_EPS = 1e-6
PAGE_SIZE = 128

SEQ_PAD = 8
LANE = 128
SUBLANES = 8
MXU_DIM = 256
VMEM_LIMIT = 56 * 2**20
INT_MIN = -2**31
NEG_BIG = -1e30

C_Z, C_XS, C_BC, C_Q, C_QI, C_RR, C_RK, C_RV = (i * 1024 for i in range(8))
C_GATE = 8192
C_LORA = C_GATE + 3 * D_MODEL
LORA_W = 512
C_K = C_LORA + LORA_W
C_V = C_K + 256
C_MISC = C_V + 256
P_W = C_MISC + LANE
MISC_DT = IDX_DIM
MISC_WI = IDX_DIM + M_HEADS
RWKV_RC_CHUNK = 64

HI = lax.Precision.HIGHEST


def _cparams(sem):
    return pltpu.CompilerParams(dimension_semantics=sem, vmem_limit_bytes=VMEM_LIMIT)


def _mm(a, b):
    return jnp.dot(a.astype(BF16), b.astype(BF16), preferred_element_type=F32)


def _mm_nt(a, b):
    return lax.dot_general(a.astype(BF16), b.astype(BF16), (((1,), (1,)), ((), ())),
                           preferred_element_type=F32)


def _mm_tn(a, b):
    return lax.dot_general(a.astype(BF16), b.astype(BF16), (((0,), (0,)), ((), ())),
                           preferred_element_type=F32)


def _mm_hi(a, b):
    return jnp.dot(a, b, precision=HI, preferred_element_type=F32)


def _mm_nt_hi(a, b):
    return lax.dot_general(a, b, (((1,), (1,)), ((), ())), precision=HI, preferred_element_type=F32)


def _sigmoid(x):
    return jax.nn.sigmoid(x)


def _silu(x):
    return x * jax.nn.sigmoid(x)


def _full(shape):
    nd = len(shape)
    return pl.BlockSpec(shape, lambda *_: (0,) * nd)


def _ffn_kernel(x_ref, g_ref, wg_ref, wu_ref, wd_ref, o_ref, n_ref):
    f = pl.program_id(1)

    @pl.when(f == 0)
    def _():
        x = x_ref[...]
        ms = jnp.mean(x * x, axis=-1, keepdims=True)
        n_ref[...] = (x * lax.rsqrt(ms + NORM_EPS) * g_ref[...]).astype(BF16)
        o_ref[...] = x

    n = n_ref[...]
    a = jnp.dot(n, wg_ref[...], preferred_element_type=F32)
    b = jnp.dot(n, wu_ref[...], preferred_element_type=F32)
    hid = (0.5 * _silu(a) * b).astype(BF16)
    o_ref[...] += jnp.dot(hid, wd_ref[...], preferred_element_type=F32)


def _ffn(x, g, wg, wu, wd, tm=512, tf=512):
    m = x.shape[0]
    tm = min(tm, m)
    return pl.pallas_call(
        _ffn_kernel,
        grid=(m // tm, D_FF // tf),
        in_specs=[pl.BlockSpec((tm, D_MODEL), lambda i, f: (i, 0)),
                  pl.BlockSpec((1, D_MODEL), lambda i, f: (0, 0)),
                  pl.BlockSpec((D_MODEL, tf), lambda i, f: (0, f)),
                  pl.BlockSpec((D_MODEL, tf), lambda i, f: (0, f)),
                  pl.BlockSpec((tf, D_MODEL), lambda i, f: (f, 0))],
        out_specs=pl.BlockSpec((tm, D_MODEL), lambda i, f: (i, 0)),
        out_shape=jax.ShapeDtypeStruct((m, D_MODEL), F32),
        scratch_shapes=[pltpu.VMEM((tm, D_MODEL), BF16)],
        compiler_params=_cparams(("parallel", "arbitrary")),
        name="ffn",
    )(x, g, wg, wu, wd)


def _inproj_kernel(x_ref, g_ref, w_ref, o_ref, n_ref):
    j = pl.program_id(1)

    @pl.when(j == 0)
    def _():
        x = x_ref[...]
        ms = jnp.mean(x * x, axis=-1, keepdims=True)
        n_ref[...] = (x * lax.rsqrt(ms + NORM_EPS) * g_ref[...]).astype(BF16)

    o_ref[...] = jnp.dot(n_ref[...], w_ref[...], preferred_element_type=F32)


def _inproj(x, g, w, tm=1024, tn=1408):
    m = x.shape[0]
    tm = tm if m % tm == 0 else min(512, m)
    return pl.pallas_call(
        _inproj_kernel,
        grid=(m // tm, P_W // tn),
        in_specs=[pl.BlockSpec((tm, D_MODEL), lambda i, j: (i, 0)),
                  pl.BlockSpec((1, D_MODEL), lambda i, j: (0, 0)),
                  pl.BlockSpec((D_MODEL, tn), lambda i, j: (0, j))],
        out_specs=pl.BlockSpec((tm, tn), lambda i, j: (i, j)),
        out_shape=jax.ShapeDtypeStruct((m, P_W), F32),
        scratch_shapes=[pltpu.VMEM((tm, D_MODEL), BF16)],
        compiler_params=_cparams(("parallel", "arbitrary")),
        name="inproj",
    )(x, g, w)


def _merge_kernel(h_ref, ya_ref, yb_ref, yc_ref, ga_ref, gb_ref, gc_ref,
                  wa_ref, wb_ref, wc_ref, wo_ref, o_ref):
    j = pl.program_id(1)

    @pl.when(j == 0)
    def _():
        o_ref[...] = h_ref[...]

    merged = (_sigmoid(ga_ref[...]) * jnp.dot(ya_ref[...], wa_ref[...], preferred_element_type=F32)
              + _sigmoid(gb_ref[...]) * jnp.dot(yb_ref[...], wb_ref[...], preferred_element_type=F32)
              + _sigmoid(gc_ref[...]) * jnp.dot(yc_ref[...], wc_ref[...], preferred_element_type=F32))
    o_ref[...] += jnp.dot(merged.astype(BF16), wo_ref[...], preferred_element_type=F32)


def _merge(h, p, ya, yb, yc, wa, wb, wc, wo, tm=512, tk=512):
    m = h.shape[0]
    tm = min(tm, m)
    gb0 = C_GATE // tk
    nk = D_MODEL // tk
    yspec = pl.BlockSpec((tm, MIX_W), lambda i, j: (i, 0))
    wspec = pl.BlockSpec((MIX_W, tk), lambda i, j: (0, j))
    return pl.pallas_call(
        _merge_kernel,
        grid=(m // tm, nk),
        in_specs=[pl.BlockSpec((tm, D_MODEL), lambda i, j: (i, 0)), yspec, yspec, yspec,
                  pl.BlockSpec((tm, tk), lambda i, j: (i, gb0 + j)),
                  pl.BlockSpec((tm, tk), lambda i, j: (i, gb0 + nk + j)),
                  pl.BlockSpec((tm, tk), lambda i, j: (i, gb0 + 2 * nk + j)),
                  wspec, wspec, wspec,
                  pl.BlockSpec((tk, D_MODEL), lambda i, j: (j, 0))],
        out_specs=pl.BlockSpec((tm, D_MODEL), lambda i, j: (i, 0)),
        out_shape=jax.ShapeDtypeStruct((m, D_MODEL), F32),
        compiler_params=_cparams(("parallel", "arbitrary")),
        name="merge",
    )(h, ya, yb, yc, p, p, p, wa, wb, wc, wo)


def _final_norm_kernel(x_ref, g_ref, o_ref):
    x = x_ref[...]
    ms = jnp.mean(x * x, axis=-1, keepdims=True)
    o_ref[...] = x * lax.rsqrt(ms + NORM_EPS) * g_ref[...]


def _final_norm(x, g, tm=512):
    m = x.shape[0]
    tm = min(tm, m)
    return pl.pallas_call(
        _final_norm_kernel,
        grid=(m // tm,),
        in_specs=[pl.BlockSpec((tm, D_MODEL), lambda i: (i, 0)), _full((1, D_MODEL))],
        out_specs=pl.BlockSpec((tm, D_MODEL), lambda i: (i, 0)),
        out_shape=jax.ShapeDtypeStruct((m, D_MODEL), F32),
        compiler_params=_cparams(("parallel",)),
        name="final_norm",
    )(x, g)


def _kidx_ln_kernel(m_ref, g_ref, b_ref, o_ref):
    x = m_ref[:, 0:IDX_DIM]
    mu = jnp.mean(x, axis=-1, keepdims=True)
    var = jnp.mean(jnp.square(x - mu), axis=-1, keepdims=True)
    o_ref[...] = (x - mu) * lax.rsqrt(var + NORM_EPS) * g_ref[...] + b_ref[...]


def _kidx_ln(p, g, b, tm=512):
    m = p.shape[0]
    tm = min(tm, m)
    return pl.pallas_call(
        _kidx_ln_kernel,
        grid=(m // tm,),
        in_specs=[pl.BlockSpec((tm, LANE), lambda i: (i, C_MISC // LANE)),
                  _full((1, IDX_DIM)), _full((1, IDX_DIM))],
        out_specs=pl.BlockSpec((tm, IDX_DIM), lambda i: (i, 0)),
        out_shape=jax.ShapeDtypeStruct((m, IDX_DIM), F32),
        compiler_params=_cparams(("parallel",)),
        name="kidx_ln",
    )(p, g, b)


def _shifted(x, k, prev, rows):
    y = pltpu.roll(x, k, 0)
    for r in range(k):
        y = jnp.where(rows == r, prev[3 - k + r:4 - k + r, :], y)
    return y


def _ssd_kernel(*refs, T, n_valid, n_alias):
    (z_ref, xs_ref, bc_ref, misc_ref, cst_ref, st0_ref, cw_ref, cb_ref, dtb_ref,
     alog_ref, dexp_ref, norm_ref, e16_ref) = refs[:13]
    y_ref, stout_ref, prev_ref, st_ref, yd_ref = refs[13 + n_alias:]
    c = pl.program_id(1)
    nc = pl.num_programs(1)

    @pl.when(c == 0)
    def _():
        prev_ref[...] = cst_ref[0, 0]
        st_ref[...] = st0_ref[0, 0]

    rows = lax.broadcasted_iota(I32, (T, 1), 0)
    cw = cw_ref[...]
    prev = prev_ref[...]

    def conv(x, lo, hi):
        pv = prev[:, lo:hi]
        y = cb_ref[:, lo:hi] + x * cw[3:4, lo:hi]
        for k in (1, 2, 3):
            y = y + _shifted(x, k, pv, rows) * cw[3 - k:4 - k, lo:hi]
        return _silu(y)

    xs_raw = xs_ref[...]
    bc_raw = bc_ref[...]
    xs = conv(xs_raw, 0, M_INNER)
    bc = conv(bc_raw, M_INNER, CONV_DIM)
    prev_ref[:, 0:M_INNER] = xs_raw[T - 3:T, :]
    prev_ref[:, M_INNER:CONV_DIM] = bc_raw[T - 3:T, :]

    dt = jax.nn.softplus(misc_ref[:, MISC_DT:MISC_DT + M_HEADS] + dtb_ref[...])
    if n_valid < T:
        dt = jnp.where(rows < n_valid, dt, 0.0)
    ad = dt * (-jnp.exp(alog_ref[...]))
    ri = lax.broadcasted_iota(I32, (T, T), 0)
    ci = lax.broadcasted_iota(I32, (T, T), 1)
    lower = ri >= ci
    tril = lower.astype(F32)
    triu = (ri <= ci).astype(F32)
    eye16 = (lax.broadcasted_iota(I32, (M_HEADS, M_HEADS), 0)
             == lax.broadcasted_iota(I32, (M_HEADS, M_HEADS), 1)).astype(F32)
    cs = _mm_hi(tril, ad)
    cst = _mm_hi(_mm_nt_hi(eye16, ad), triu)
    e16 = e16_ref[...].astype(BF16)

    def expand(x):
        hi = x.astype(BF16)
        lo = (x - hi.astype(F32)).astype(BF16)
        return (jnp.dot(hi, e16, preferred_element_type=F32)
                + jnp.dot(lo, e16, preferred_element_type=F32))

    dt_e = expand(dt)
    ecs_e = expand(jnp.exp(cs))
    wl_e = expand(jnp.exp(cs[T - 1:T, :] - cs))
    xd = xs * dt_e
    xdw = (xd * wl_e).astype(BF16)
    xd = xd.astype(BF16)
    bcb = bc.astype(BF16)
    for g in range(M_GROUPS):
        bg = bcb[:, g * M_STATE:(g + 1) * M_STATE]
        cg = bcb[:, (M_GROUPS + g) * M_STATE:(M_GROUPS + g + 1) * M_STATE]
        cbm = _mm_nt(cg, bg)
        for hh in range(M_HEADS // M_GROUPS):
            h = g * (M_HEADS // M_GROUPS) + hh
            hs = slice(h * M_HEAD_DIM, (h + 1) * M_HEAD_DIM)
            diff = cs[:, h:h + 1] - cst[h:h + 1, :]
            lm = jnp.exp(jnp.where(lower, diff, -jnp.inf))
            yd = _mm(cbm * lm, xd[:, hs])
            st = st_ref[h]
            yo = _mm_nt(cg, st)
            yd_ref[:, hs] = yd + yo * ecs_e[:, hs]
            upd = _mm_tn(xdw[:, hs], bg)
            dec = jnp.exp(cst[h:h + 1, T - 1:T])
            st_ref[h] = st * dec + upd

    y = yd_ref[...] + xs * dexp_ref[...]
    y = y * _silu(z_ref[...])
    gw = M_INNER // M_GROUPS
    for g in range(M_GROUPS):
        yg = y[:, g * gw:(g + 1) * gw]
        ms = jnp.mean(yg * yg, axis=-1, keepdims=True)
        y_ref[:, g * gw:(g + 1) * gw] = (yg * lax.rsqrt(ms + NORM_EPS)
                                         * norm_ref[:, g * gw:(g + 1) * gw]).astype(y_ref.dtype)

    @pl.when(c == nc - 1)
    def _():
        stout_ref[0, 0] = st_ref[...]


def _alias_inputs(n_in, bufs):
    specs, args, aliases = [], [], {}
    for k, b in enumerate(bufs):
        if b is not None:
            aliases[n_in + len(args)] = k
            specs.append(pl.BlockSpec(memory_space=pl.ANY))
            args.append(b)
    return specs, args, aliases


def _ssd(p, row0, n_seq, n_chunks, T, n_valid, layer, conv_state, ssm0, cw, cb, dtb, alog, dexp, norm,
         e16, *, n_rows, depth, out_layer, y_buf=None, st_buf=None):
    rb0 = row0 // T

    def rmap(col):
        return lambda s, c: (rb0 + s * n_chunks + c, col)

    args = [p, p, p, p, conv_state, ssm0, cw, cb, dtb, alog, dexp, norm, e16]
    a_specs, a_args, aliases = _alias_inputs(len(args), (y_buf, st_buf))
    kern = functools.partial(_ssd_kernel, T=T, n_valid=n_valid, n_alias=len(a_args))
    return pl.pallas_call(
        kern,
        grid=(n_seq, n_chunks),
        in_specs=[pl.BlockSpec((T, 1024), rmap(C_Z // 1024)),
                  pl.BlockSpec((T, 1024), rmap(C_XS // 1024)),
                  pl.BlockSpec((T, 1024), rmap(C_BC // 1024)),
                  pl.BlockSpec((T, LANE), rmap(C_MISC // LANE)),
                  pl.BlockSpec((1, 1, CONV_W - 1, CONV_DIM), lambda s, c: (layer, s, 0, 0)),
                  pl.BlockSpec((1, 1, M_HEADS, M_HEAD_DIM, M_STATE), lambda s, c: (layer, s, 0, 0, 0)),
                  _full((CONV_W, CONV_DIM)), _full((1, CONV_DIM)), _full((1, M_HEADS)),
                  _full((1, M_HEADS)), _full((1, M_INNER)), _full((1, M_INNER)),
                  _full((M_HEADS, M_INNER))] + a_specs,
        out_specs=[pl.BlockSpec((T, M_INNER), rmap(0)),
                   pl.BlockSpec((1, 1, M_HEADS, M_HEAD_DIM, M_STATE), lambda s, c: (out_layer, s, 0, 0, 0))],
        out_shape=[jax.ShapeDtypeStruct((n_rows, M_INNER), BF16),
                   jax.ShapeDtypeStruct((depth, n_seq, M_HEADS, M_HEAD_DIM, M_STATE), F32)],
        scratch_shapes=[pltpu.VMEM((CONV_W - 1, CONV_DIM), F32),
                        pltpu.VMEM((M_HEADS, M_HEAD_DIM, M_STATE), F32),
                        pltpu.VMEM((T, M_INNER), F32)],
        input_output_aliases=aliases,
        compiler_params=_cparams(("parallel", "arbitrary")),
        name="ssd",
    )(*args, *a_args)


def _rwkv_kernel(*refs, T, n_valid, HG, n_alias):
    (r_ref, k_ref, v_ref, lo_ref, sh0_ref, s0_ref, mu_ref, w0_ref, w2_ref, a0_ref,
     a2_ref, g2_ref, kkw_ref, kaw_ref, rk_ref, gng_ref, gnb_ref) = refs[:17]
    (y_ref, sout_ref, last_ref, s_ref, yb_ref, xs_ref, hr_ref,
     gm_ref) = refs[17 + n_alias:]
    c = pl.program_id(1)
    nc = pl.num_programs(1)

    @pl.when(c == 0)
    def _():
        last_ref[...] = sh0_ref[0, 0]
        s_ref[...] = s0_ref[0, 0]

    rows = lax.broadcasted_iota(I32, (T, 1), 0)

    def mix(ref, lo, hi):
        x = ref[...]
        prev = jnp.where(rows == 0, last_ref[:, lo:hi], pltpu.roll(x, 1, 0))
        last_ref[:, lo:hi] = x[T - 1:T, :]
        return x + (prev - x) * mu_ref[:, lo:hi]

    r = mix(r_ref, 0, 1024)
    k = mix(k_ref, 1024, 2048)
    v = mix(v_ref, 2048, 3072)
    lo = mix(lo_ref, 3072, 3072 + LORA_W)
    xw = lo[:, 0:R_DECAY_LORA]
    xa = lo[:, R_DECAY_LORA:R_DECAY_LORA + R_A_LORA]
    xg = lo[:, R_DECAY_LORA + R_A_LORA:LORA_W]
    wl = -jax.nn.softplus(-(w0_ref[...] + _mm(jnp.tanh(xw), w2_ref[...]))) - 0.5
    ld = -jnp.exp(wl)
    a = _sigmoid(a0_ref[...] + _mm(xa, a2_ref[...]))
    gate = _mm(_sigmoid(xg), g2_ref[...])
    kk = k * kkw_ref[...]
    km = k * (1.0 + (a - 1.0) * kaw_ref[...])
    if n_valid < T:
        ok = rows < n_valid
        ld = jnp.where(ok, ld, 0.0)
        kk = jnp.where(ok, kk, 0.0)
        km = jnp.where(ok, km, 0.0)
        v = jnp.where(ok, v, 0.0)

    ri = lax.broadcasted_iota(I32, (T, T), 0)
    ci = lax.broadcasted_iota(I32, (T, T), 1)
    cum = _mm_hi((ri >= ci).astype(F32), ld)
    eg = jnp.exp(cum)
    egm = jnp.exp(cum - ld)
    ei = jnp.exp(-cum)
    el = jnp.exp(cum[T - 1:T, :] - cum)
    r_eg = r * eg
    k_ei = km * ei
    k_el = km * el
    rk_sum = r * km * rk_ref[...]
    gng = gng_ref[...]
    gnb = gnb_ref[...]

    R = HG * T
    GW = HG * R_HEAD
    rr = lax.broadcasted_iota(I32, (R, R), 0)
    cc = lax.broadcasted_iota(I32, (R, R), 1)
    same = (rr // T) == (cc // T)
    strict = jnp.logical_and(same, rr > cc)
    incl = jnp.logical_and(same, rr >= cc)
    head_of = (lax.broadcasted_iota(I32, (R, GW), 0) // T) == (lax.broadcasted_iota(I32, (R, GW), 1) // R_HEAD)
    blk_r = (rr % T) // SUBLANES
    blk_c = (cc % T) // SUBLANES

    n_groups = R_HEADS // HG

    def head_slices(gi):
        return [slice((gi * HG + hh) * R_HEAD, (gi * HG + hh + 1) * R_HEAD) for hh in range(HG)]

    def stack(x, gi):
        return jnp.concatenate([x[:, s_] for s_ in head_slices(gi)], axis=0)

    def unit_kk(gi):
        kks = stack(kk, gi)
        nrm = jnp.sqrt(jnp.sum(kks * kks, axis=-1, keepdims=True))
        return kks / jnp.maximum(nrm, 1e-12)

    for gi in range(n_groups):
        kks = unit_kk(gi)
        q2 = jnp.concatenate([kks * stack(egm, gi), stack(r_eg, gi)], axis=0)
        k2 = jnp.concatenate([stack(k_ei, gi), kks * stack(a, gi) * stack(ei, gi)], axis=0)
        gm = _mm_nt(q2, k2)
        gm_ref[gi] = gm
        scat = s_ref[gi * HG:(gi + 1) * HG].reshape(GW, R_HEAD)
        hmw = _mm_nt(q2, scat)
        hk = jnp.concatenate([hmw[hh * T:(hh + 1) * T, hh * R_HEAD:(hh + 1) * R_HEAD]
                              for hh in range(HG)], axis=0)
        hr_ref[gi] = jnp.concatenate([hmw[R + hh * T:R + (hh + 1) * T, hh * R_HEAD:(hh + 1) * R_HEAD]
                                      for hh in range(HG)], axis=0)
        xs_ref[gi] = hk + _mm(jnp.where(strict, gm[0:R, 0:R], 0.0), stack(v, gi))

    for jb in range(T // SUBLANES):
        for gi in range(n_groups):
            for hh in range(HG):
                r0 = hh * T + jb * SUBLANES
                xb = xs_ref[gi, r0:r0 + SUBLANES, :]
                lrow = gm_ref[gi, r0:r0 + SUBLANES, R:2 * R]
                for s in range(1, SUBLANES):
                    below = lax.broadcasted_iota(I32, (SUBLANES, 1), 0) >= s
                    xb = xb - jnp.where(below, lrow[:, r0 + s - 1:r0 + s], 0.0) * xb[s - 1:s, :]
                xs_ref[gi, r0:r0 + SUBLANES, :] = xb
            if jb + 1 < T // SUBLANES:
                later = jnp.where(jnp.logical_and(jnp.logical_and(same, blk_c == jb), blk_r > jb),
                                  gm_ref[gi, 0:R, R:2 * R], 0.0)
                xs_ref[gi] = xs_ref[gi] - _mm(later, xs_ref[gi])

    for gi in range(n_groups):
        hsl = head_slices(gi)
        kks = unit_kk(gi)
        bs = kks * stack(a, gi)
        vs = stack(v, gi)
        u = -xs_ref[gi]
        ak = jnp.where(incl, gm_ref[gi, R:2 * R, 0:R], 0.0)
        ab = jnp.where(incl, gm_ref[gi, R:2 * R, R:2 * R], 0.0)
        scat = s_ref[gi * HG:(gi + 1) * HG].reshape(GW, R_HEAD)
        y = hr_ref[gi] + _mm(jnp.concatenate([ak, ab], axis=1), jnp.concatenate([vs, u], axis=0))
        vw = jnp.where(head_of, jnp.concatenate([v[:, gi * GW:(gi + 1) * GW]] * HG, axis=0), 0.0)
        uw = jnp.where(head_of, jnp.concatenate([u] * HG, axis=1), 0.0)
        upd = _mm_tn(jnp.concatenate([vw, uw], axis=0),
                     jnp.concatenate([stack(k_el, gi), bs * stack(el, gi)], axis=0))
        g_last = jnp.concatenate([jnp.broadcast_to(eg[T - 1:T, s_], (R_HEAD, R_HEAD)) for s_ in hsl],
                                 axis=0)
        s_ref[gi * HG:(gi + 1) * HG] = (scat * g_last + upd).reshape(HG, R_HEAD, R_HEAD)
        mu = jnp.mean(y, axis=-1, keepdims=True)
        var = jnp.mean(jnp.square(y - mu), axis=-1, keepdims=True)
        gs = jnp.concatenate([jnp.broadcast_to(gng[:, s_], (T, R_HEAD)) for s_ in hsl], axis=0)
        gb = jnp.concatenate([jnp.broadcast_to(gnb[:, s_], (T, R_HEAD)) for s_ in hsl], axis=0)
        yn = (y - mu) * lax.rsqrt(var + GN_EPS) * gs + gb
        yn = yn + jnp.sum(stack(rk_sum, gi), axis=-1, keepdims=True) * vs
        for hh in range(HG):
            yb_ref[:, hsl[hh]] = yn[hh * T:(hh + 1) * T]

    y_ref[...] = (yb_ref[...] * gate).astype(y_ref.dtype)

    @pl.when(c == nc - 1)
    def _():
        sout_ref[0, 0] = s_ref[...]


def _rwkv(p, row0, n_seq, n_chunks, T, n_valid, layer, shift0, s0, mu, w0, w2, a0, a2, g2, kkw, kaw,
          rk, gng, gnb, *, n_rows, depth, out_layer, y_buf=None, st_buf=None):
    rb0 = row0 // T

    def rmap(col):
        return lambda s, c: (rb0 + s * n_chunks + c, col)

    sw = 3072 + LORA_W
    hg = min(R_HEADS, max(1, MXU_DIM // T))
    args = [p, p, p, p, shift0, s0, mu, w0, w2, a0, a2, g2, kkw, kaw, rk, gng, gnb]
    a_specs, a_args, aliases = _alias_inputs(len(args), (y_buf, st_buf))
    kern = functools.partial(_rwkv_kernel, T=T, n_valid=n_valid, HG=hg, n_alias=len(a_args))
    return pl.pallas_call(
        kern,
        grid=(n_seq, n_chunks),
        in_specs=[pl.BlockSpec((T, 1024), rmap(C_RR // 1024)),
                  pl.BlockSpec((T, 1024), rmap(C_RK // 1024)),
                  pl.BlockSpec((T, 1024), rmap(C_RV // 1024)),
                  pl.BlockSpec((T, LORA_W), rmap(C_LORA // LORA_W)),
                  pl.BlockSpec((1, 1, 1, sw), lambda s, c: (layer, s, 0, 0)),
                  pl.BlockSpec((1, 1, R_HEADS, R_HEAD, R_HEAD), lambda s, c: (layer, s, 0, 0, 0)),
                  _full((1, sw)), _full((1, R_WIDTH)), _full((R_DECAY_LORA, R_WIDTH)),
                  _full((1, R_WIDTH)), _full((R_A_LORA, R_WIDTH)),
                  _full((LORA_W - R_DECAY_LORA - R_A_LORA, R_WIDTH)),
                  _full((1, R_WIDTH)), _full((1, R_WIDTH)), _full((1, R_WIDTH)),
                  _full((1, R_WIDTH)), _full((1, R_WIDTH))] + a_specs,
        out_specs=[pl.BlockSpec((T, R_WIDTH), rmap(0)),
                   pl.BlockSpec((1, 1, R_HEADS, R_HEAD, R_HEAD), lambda s, c: (out_layer, s, 0, 0, 0))],
        out_shape=[jax.ShapeDtypeStruct((n_rows, R_WIDTH), BF16),
                   jax.ShapeDtypeStruct((depth, n_seq, R_HEADS, R_HEAD, R_HEAD), F32)],
        scratch_shapes=[pltpu.VMEM((1, sw), F32),
                        pltpu.VMEM((R_HEADS, R_HEAD, R_HEAD), F32),
                        pltpu.VMEM((T, R_WIDTH), F32),
                        pltpu.VMEM((R_HEADS // hg, hg * T, R_HEAD), F32),
                        pltpu.VMEM((R_HEADS // hg, hg * T, R_HEAD), F32),
                        pltpu.VMEM((R_HEADS // hg, 2 * hg * T, 2 * hg * T), F32)],
        input_output_aliases=aliases,
        compiler_params=_cparams(("parallel", "arbitrary")),
        name="rwkv",
    )(*args, *a_args)


def _score_keys(acc, causal):
    bits = pltpu.bitcast(acc, I32)
    key = bits ^ ((bits >> 31) & 0x7FFFFFFF)
    return jnp.where(causal, key, INT_MIN)


def _kth_threshold(count_ge, shape, top, bits_per_step=1):
    tb = jnp.zeros(shape, I32)
    for shift in range(32 - bits_per_step, -1, -bits_per_step):
        digit = jnp.zeros(shape, I32)
        for d in range(1, 1 << bits_per_step):
            inc = d << shift
            inc = inc - (1 << 32) if inc >= (1 << 31) else inc
            cnt = count_ge((tb | jnp.int32(inc)) ^ jnp.int32(INT_MIN))
            digit = digit + jnp.where(cnt >= top, 1, 0)
        tb = tb | (digit << shift)
    return jnp.maximum(tb ^ jnp.int32(INT_MIN), INT_MIN + 1)


def _toeplitz_bias(base_row, nrows):
    return pltpu.roll(jnp.broadcast_to(base_row, (nrows, 2 * LANE)), 0, 1, stride=1, stride_axis=0)


def _dsa_prompt_kernel(q_ref, qi_ref, misc_ref, kit_ref, kt_ref, va_ref, base_ref, o_ref,
                       key_ref, mx_ref, acc_ref, near_ref, *, top, TA):
    i = pl.program_id(0)
    QB = LANE
    q0 = i * QB
    n_a = (q0 + QB + TA - 1) // TA
    wi = misc_ref[:, MISC_WI:MISC_WI + IDX_HEADS] * (IDX_HEADS ** -0.5 * IDX_DIM ** -0.5)
    qi = qi_ref[...]
    qis = jnp.concatenate([qi[:, h * IDX_DIM:(h + 1) * IDX_DIM] for h in range(IDX_HEADS)],
                          axis=0).astype(BF16)
    rowq = q0 + lax.broadcasted_iota(I32, (QB, 1), 0)

    def score_tile(kt, carry):
        off = pl.multiple_of(kt * TA, TA)
        s = jnp.maximum(jnp.dot(qis, kit_ref[:, pl.ds(off, TA)], preferred_element_type=F32), 0.0)
        acc = s[0:QB] * wi[:, 0:1]
        for h in range(1, IDX_HEADS):
            acc = acc + s[h * QB:(h + 1) * QB] * wi[:, h:h + 1]
        col = off + lax.broadcasted_iota(I32, (1, TA), 1)
        key_ref[:, pl.ds(off, TA)] = _score_keys(acc, col <= rowq)
        return carry

    lax.fori_loop(0, n_a, score_tile, 0)

    def count_ge(cand):
        def body(kt, cnt):
            off = pl.multiple_of(kt * TA, TA)
            ge = jnp.where(key_ref[:, pl.ds(off, TA)] >= cand, 1.0, 0.0)
            for j in range(TA // LANE):
                cnt = cnt + ge[:, j * LANE:(j + 1) * LANE]
            return cnt
        cnt = lax.fori_loop(0, n_a, body, jnp.zeros((QB, LANE), F32))
        return jnp.sum(cnt, axis=-1, keepdims=True)

    thr = _kth_threshold(count_ge, (QB, 1), float(top))

    n_ge = count_ge(thr)

    @pl.when(jnp.max(n_ge) > top)
    def _():
        def count_tied_before(pos):
            def body(kt, cnt):
                off = pl.multiple_of(kt * TA, TA)
                col = off + lax.broadcasted_iota(I32, (1, TA), 1)
                hit = jnp.where(key_ref[:, pl.ds(off, TA)] == thr, jnp.where(col < pos, 1.0, 0.0), 0.0)
                for j in range(TA // LANE):
                    cnt = cnt + hit[:, j * LANE:(j + 1) * LANE]
                return cnt
            cnt = lax.fori_loop(0, n_a, body, jnp.zeros((QB, LANE), F32))
            return jnp.sum(cnt, axis=-1, keepdims=True)

        need = top - count_ge(thr + 1)
        last = jnp.zeros((QB, 1), I32)
        for bit in range(max(1, (kit_ref.shape[1] - 1).bit_length()) - 1, -1, -1):
            cand = last | jnp.int32(1 << bit)
            last = jnp.where(count_tied_before(cand) < need, cand, last)

        def drop(kt, carry):
            off = pl.multiple_of(kt * TA, TA)
            col = off + lax.broadcasted_iota(I32, (1, TA), 1)
            k = key_ref[:, pl.ds(off, TA)]
            key_ref[:, pl.ds(off, TA)] = jnp.where(k == thr, jnp.where(col > last, INT_MIN, k), k)
            return carry

        lax.fori_loop(0, n_a, drop, 0)

    @pl.when(i == 0)
    def _():
        for g in range(KV_HEADS):
            near_ref[g] = jnp.concatenate(
                [_toeplitz_bias(base_ref[h:h + 1, :], QB) - base_ref[h:h + 1, 2 * LANE - 1:2 * LANE]
                 for h in range(g * GQA, (g + 1) * GQA)], axis=0)

    q = q_ref[...]
    lim = (i - 1) * QB
    n_far = (jnp.maximum(lim, 0) + TA - 1) // TA
    qgs = []
    for g in range(KV_HEADS):
        heads = range(g * GQA, (g + 1) * GQA)
        qgs.append((jnp.concatenate([q[:, h * HEAD_DIM:(h + 1) * HEAD_DIM] for h in heads], axis=0)
                    * HEAD_DIM ** -0.5).astype(BF16))

    def far_logits(g, off, msk):
        s = jnp.dot(qgs[g], kt_ref[g, :, pl.ds(off, TA)], preferred_element_type=F32)
        return (s.reshape(GQA, QB, TA) + msk[None]).reshape(GQA * QB, TA)

    def far_mask(off):
        col = off + lax.broadcasted_iota(I32, (1, TA), 1)
        return jnp.where(key_ref[:, pl.ds(off, TA)] >= thr,
                         jnp.where(col < lim, 0.0, NEG_BIG), NEG_BIG)

    def near_logits(g, off, lo, width, msk):
        s = (jnp.dot(qgs[g], kt_ref[g, :, pl.ds(off, width)], preferred_element_type=F32)
             + near_ref[g, :, lo:lo + width])
        return (s.reshape(GQA, QB, width) + msk[None]).reshape(GQA * QB, width)

    def near_mask(off, width):
        return jnp.where(key_ref[:, pl.ds(off, width)] >= thr, 0.0, NEG_BIG)

    mx_ref[...] = jnp.full(mx_ref.shape, NEG_BIG, F32)
    acc_ref[...] = jnp.zeros(acc_ref.shape, F32)

    def accumulate(g, s, v_tile):
        width = s.shape[1]
        m = s[:, 0:LANE]
        for j in range(1, width // LANE):
            m = jnp.maximum(m, s[:, j * LANE:(j + 1) * LANE])
        m_old = mx_ref[g]
        m_new = jnp.maximum(m_old, jnp.broadcast_to(jnp.max(m, axis=-1, keepdims=True), m.shape))
        pr = jnp.exp((s - jnp.concatenate([m_new] * (width // LANE), axis=1)).astype(BF16))
        acc_ref[g] = (acc_ref[g] * jnp.exp(m_old - m_new)
                      + jnp.dot(pr, v_tile, preferred_element_type=F32))
        mx_ref[g] = m_new

    def far_acc(kt, carry):
        off = pl.multiple_of(kt * TA, TA)
        msk = far_mask(off)
        for g in range(KV_HEADS):
            accumulate(g, far_logits(g, off, msk), va_ref[g, pl.ds(off, TA), :])
        return carry

    def near_acc(kt, lo, width):
        off = pl.multiple_of(kt * LANE, LANE)
        msk = near_mask(off, width)
        for g in range(KV_HEADS):
            accumulate(g, near_logits(g, off, lo, width, msk), va_ref[g, pl.ds(off, width), :])

    lax.fori_loop(0, n_far, far_acc, 0)

    @pl.when(i == 0)
    def _():
        near_acc(0, LANE, LANE)

    @pl.when(i >= 1)
    def _():
        near_acc(i - 1, 0, 2 * LANE)
    for g in range(KV_HEADS):
        acc = acc_ref[g]
        out = acc[:, 0:HEAD_DIM] / acc[:, HEAD_DIM:HEAD_DIM + 1]
        for hh in range(GQA):
            h = g * GQA + hh
            o_ref[:, h * HEAD_DIM:(h + 1) * HEAD_DIM] = out[hh * QB:(hh + 1) * QB].astype(o_ref.dtype)


def _dsa_prompt(p, kit, kt, va, base, S, n_rows):
    top = min(TOPK_MAX, S // 4)
    TA = min(512, S)
    kern = functools.partial(_dsa_prompt_kernel, top=top, TA=TA)
    return pl.pallas_call(
        kern,
        grid=(S // LANE,),
        in_specs=[pl.BlockSpec((LANE, 1024), lambda i: (i, C_Q // 1024)),
                  pl.BlockSpec((LANE, 1024), lambda i: (i, C_QI // 1024)),
                  pl.BlockSpec((LANE, LANE), lambda i: (i, C_MISC // LANE)),
                  _full((IDX_DIM, S)), _full((KV_HEADS, HEAD_DIM, S)), _full((KV_HEADS, S, LANE)),
                  _full((ATT_HEADS, 2 * LANE))],
        out_specs=pl.BlockSpec((LANE, MIX_W), lambda i: (i, 0)),
        out_shape=jax.ShapeDtypeStruct((n_rows, MIX_W), BF16),
        scratch_shapes=[pltpu.VMEM((LANE, S), I32),
                        pltpu.VMEM((KV_HEADS, GQA * LANE, LANE), F32),
                        pltpu.VMEM((KV_HEADS, GQA * LANE, LANE), F32),
                        pltpu.VMEM((KV_HEADS, GQA * LANE, 2 * LANE), F32)],
        compiler_params=_cparams(("arbitrary",)),
        name="dsa_prompt",
    )(p, p, p, kit, kt, va, base)


def _dsa_sample_kernel(pt_ref, q_ref, qi_ref, misc_ref, kn_ref, vn_ref, kin_ref, base_ref, *rest,
                       n_pages, n_valid, top):
    ki_pages = rest[0:n_pages]
    k_pages = rest[n_pages:2 * n_pages]
    v_pages = rest[2 * n_pages:3 * n_pages]
    o_ref = rest[3 * n_pages + 1]
    kib_ref, kb_ref, vb_ref, msk_ref = rest[3 * n_pages + 2:]
    TQ = SEQ_PAD
    lo = n_pages * LANE
    NK = lo + LANE
    wi = misc_ref[:, MISC_WI:MISC_WI + IDX_HEADS] * (IDX_HEADS ** -0.5 * IDX_DIM ** -0.5)
    qi = qi_ref[...]
    qis = jnp.concatenate([qi[:, h * IDX_DIM:(h + 1) * IDX_DIM] for h in range(IDX_HEADS)], axis=0)

    eye = (lax.broadcasted_iota(I32, (HEAD_DIM, HEAD_DIM), 0)
           == lax.broadcasted_iota(I32, (HEAD_DIM, HEAD_DIM), 1)).astype(BF16)

    def new_slot(x):
        xt = _mm_nt(eye, x)
        return jnp.concatenate([xt, jnp.zeros((HEAD_DIM, LANE - TQ), F32)], axis=1).astype(BF16)

    for t in range(n_pages):
        kib_ref[:, t * LANE:(t + 1) * LANE] = ki_pages[t][0, 0].astype(BF16)
        for g in range(KV_HEADS):
            kb_ref[g, :, t * LANE:(t + 1) * LANE] = k_pages[t][0, 0, g].astype(BF16)
            vb_ref[g, :, t * LANE:(t + 1) * LANE] = v_pages[t][0, 0, g].astype(BF16)
    kib_ref[:, lo:NK] = new_slot(kin_ref[...])
    kn = kn_ref[...]
    vn = vn_ref[...]
    for g in range(KV_HEADS):
        kb_ref[g, :, lo:NK] = new_slot(kn[:, g * HEAD_DIM:(g + 1) * HEAD_DIM])
        vb_ref[g, :, lo:NK] = new_slot(vn[:, g * HEAD_DIM:(g + 1) * HEAD_DIM])

    s = jnp.maximum(_mm(qis, kib_ref[...]), 0.0)
    acc = s[0:TQ] * wi[:, 0:1]
    for h in range(1, IDX_HEADS):
        acc = acc + s[h * TQ:(h + 1) * TQ] * wi[:, h:h + 1]
    trow = lax.broadcasted_iota(I32, (TQ, 1), 0)
    new = lax.broadcasted_iota(I32, (1, NK), 1) - lo
    keys = _score_keys(acc, new <= jnp.minimum(trow, n_valid - 1))

    def count_ge(cand):
        return jnp.sum(jnp.where(keys >= cand, 1.0, 0.0), axis=-1, keepdims=True)

    thr = _kth_threshold(count_ge, (TQ, 1), float(top), bits_per_step=4)
    msk_ref[...] = jnp.where(keys >= thr, 0.0, NEG_BIG)

    @pl.when(jnp.max(jnp.where(trow < n_valid, count_ge(thr), 0.0)) > top)
    def _():
        pos = new + lo
        tied = keys == thr

        def count_tied_before(p):
            return jnp.sum(jnp.where(tied, jnp.where(pos < p, 1.0, 0.0), 0.0), axis=-1, keepdims=True)

        need = top - count_ge(thr + 1)
        last = jnp.zeros((TQ, 1), I32)
        for bit in range(max(1, (NK - 1).bit_length()) - 1, -1, -1):
            cand = last | jnp.int32(1 << bit)
            last = jnp.where(count_tied_before(cand) < need, cand, last)
        msk_ref[...] = jnp.where(tied, jnp.where(pos > last, NEG_BIG, 0.0), msk_ref[...])

    msk = jnp.concatenate([msk_ref[...]] * ATT_HEADS, axis=0)

    q = q_ref[...] * HEAD_DIM ** -0.5
    zero = jnp.zeros((TQ, HEAD_DIM), F32)
    q_rows, nears = [], []
    for h in range(ATT_HEADS):
        qh = q[:, h * HEAD_DIM:(h + 1) * HEAD_DIM]
        q_rows.append(jnp.concatenate([qh if g == h // GQA else zero for g in range(KV_HEADS)], axis=1))
        nears.append(_toeplitz_bias(base_ref[h:h + 1, :], TQ) - base_ref[h:h + 1, 2 * LANE - 1:2 * LANE])
    qbd = jnp.concatenate(q_rows, axis=0)
    bias = jnp.concatenate([jnp.zeros((ATT_HEADS * TQ, lo - LANE), F32),
                            jnp.concatenate(nears, axis=0)], axis=1)
    s = _mm(qbd, kb_ref[...].reshape(KV_HEADS * HEAD_DIM, NK)) + bias + msk
    mx = jnp.max(s, axis=-1, keepdims=True)
    pr = jnp.exp(s - mx)
    pr = pr / jnp.sum(pr, axis=-1, keepdims=True)
    out = _mm_nt(pr, vb_ref[...].reshape(KV_HEADS * HEAD_DIM, NK))
    for h in range(ATT_HEADS):
        g = h // GQA
        o_ref[:, h * HEAD_DIM:(h + 1) * HEAD_DIM] = out[h * TQ:(h + 1) * TQ,
                                                        g * HEAD_DIM:(g + 1) * HEAD_DIM].astype(o_ref.dtype)


def _dsa_sample(p, row0, n_seq, kidx_ln, base, layer, cache_k, cache_v, cache_kidx, page_table, n_valid,
                y_buf):
    n_pages = page_table.shape[1]
    past = n_pages * PAGE_SIZE
    top = min(TOPK_MAX, (past + n_valid) // 4)
    rb0 = row0 // SEQ_PAD
    ck = jnp.transpose(cache_k, (0, 1, 3, 4, 2))
    cv = jnp.transpose(cache_v, (0, 1, 3, 4, 2))
    cki = jnp.transpose(cache_kidx, (0, 1, 3, 2))

    def rmap(col):
        return lambda b, pt: (rb0 + b, col)

    def pmap(pg):
        return lambda b, pt: (layer, pt[b, pg], 0, 0)

    def pmap5(pg):
        return lambda b, pt: (layer, pt[b, pg], 0, 0, 0)

    in_specs = [pl.BlockSpec((SEQ_PAD, 1024), rmap(C_Q // 1024)),
                pl.BlockSpec((SEQ_PAD, 1024), rmap(C_QI // 1024)),
                pl.BlockSpec((SEQ_PAD, LANE), rmap(C_MISC // LANE)),
                pl.BlockSpec((SEQ_PAD, 256), rmap(C_K // 256)),
                pl.BlockSpec((SEQ_PAD, 256), rmap(C_V // 256)),
                pl.BlockSpec((SEQ_PAD, IDX_DIM), lambda b, pt: (rb0 + b, 0)),
                pl.BlockSpec((ATT_HEADS, 2 * LANE), lambda b, pt: (0, 0))]
    in_specs += [pl.BlockSpec((1, 1, IDX_DIM, PAGE_SIZE), pmap(pg)) for pg in range(n_pages)]
    in_specs += [pl.BlockSpec((1, 1, KV_HEADS, HEAD_DIM, PAGE_SIZE), pmap5(pg)) for pg in range(n_pages)]
    in_specs += [pl.BlockSpec((1, 1, KV_HEADS, HEAD_DIM, PAGE_SIZE), pmap5(pg)) for pg in range(n_pages)]
    kern = functools.partial(_dsa_sample_kernel, n_pages=n_pages, n_valid=n_valid, top=top)
    nk = (n_pages + 1) * LANE
    return pl.pallas_call(
        kern,
        grid_spec=pltpu.PrefetchScalarGridSpec(
            num_scalar_prefetch=1,
            grid=(n_seq,),
            in_specs=in_specs + [pl.BlockSpec(memory_space=pl.ANY)],
            out_specs=pl.BlockSpec((SEQ_PAD, MIX_W), lambda b, pt: (rb0 + b, 0)),
            scratch_shapes=[pltpu.VMEM((IDX_DIM, nk), BF16),
                            pltpu.VMEM((KV_HEADS, HEAD_DIM, nk), BF16),
                            pltpu.VMEM((KV_HEADS, HEAD_DIM, nk), BF16),
                            pltpu.VMEM((SEQ_PAD, nk), F32)]),
        out_shape=jax.ShapeDtypeStruct(y_buf.shape, BF16),
        input_output_aliases={1 + len(in_specs): 0},
        compiler_params=_cparams(("arbitrary",)),
        name="dsa_sample",
    )(page_table, p, p, p, p, p, kidx_ln, base,
      *([cki] * n_pages), *([ck] * n_pages), *([cv] * n_pages), y_buf)


def _t5_base(t5_table):
    x = jnp.arange(2 * LANE, dtype=jnp.int32)
    rel = jnp.where(x <= LANE, LANE - x, T5_MAX_DIST)
    n = jnp.maximum(rel, 0)
    exact = T5_BUCKETS // 2
    nf = jnp.maximum(n, exact).astype(F32)
    large = exact + (jnp.log(nf / exact) / math.log(T5_MAX_DIST / exact)
                     * (T5_BUCKETS - exact)).astype(jnp.int32)
    bucket = jnp.where(n < exact, n, jnp.minimum(large, T5_BUCKETS - 1))
    return t5_table[bucket].astype(F32).T


def _relayout_w_in(w):
    widths = (M_INNER, CONV_DIM, M_HEADS, ATT_HEADS * HEAD_DIM, KV_HEADS * HEAD_DIM,
              KV_HEADS * HEAD_DIM, IDX_HEADS * IDX_DIM, IDX_DIM, IDX_HEADS, R_PROJ, 3 * D_MODEL)
    offs = [0]
    for wd in widths:
        offs.append(offs[-1] + wd)
    z, xbc, dt, q, k, v, qi, ki, wi, pr, gates = (w[:, offs[i]:offs[i + 1]] for i in range(len(widths)))
    zeros = lambda n: jnp.zeros((w.shape[0], n), w.dtype)
    lora = pr[:, 3 * R_WIDTH:]
    cols = [z, xbc, q, qi, pr[:, :3 * R_WIDTH], gates,
            lora, zeros(LORA_W - lora.shape[1]), k, v,
            ki, dt, wi, zeros(LANE - IDX_DIM - M_HEADS - IDX_HEADS)]
    out = jnp.concatenate(cols, axis=1).astype(BF16)
    assert out.shape[1] == P_W
    return out


def _pad_cols(x, n):
    return jnp.concatenate([x, jnp.zeros(x.shape[:-1] + (n - x.shape[-1],), x.dtype)], axis=-1)


def _shift_layout(x):
    return _pad_cols(x, 3 * R_WIDTH + LORA_W)


def kernel(x_prompt, x_sample, cache_k, cache_v, cache_kidx, page_table, state_ssm, state_conv, state_wkv, state_shift, ffn1_norm, ffn1_gate, ffn1_up, ffn1_down, mix_norm, w_in, conv_w, conv_b, dt_bias, a_log, d_skip, ssm_norm, kidx_ln_g, kidx_ln_b, t5_table, rwkv_mu, rwkv_w0, rwkv_w2, rwkv_a0, rwkv_a2, rwkv_g2, rwkv_kk, rwkv_ka, rwkv_rk, rwkv_gn_g, rwkv_gn_b, w_br_a, w_br_b, w_br_c, w_out, ffn2_norm, ffn2_gate, ffn2_up, ffn2_down, final_norm):
    bp, S, d = x_prompt.shape
    nb, T_dec, _ = x_sample.shape
    depth = w_in.shape[0]
    assert bp == 1 and T_dec <= SEQ_PAD and S % SSD_CHUNK == 0
    n_p = bp * S
    n_rows = n_p + nb * SEQ_PAD
    xs_pad = jnp.concatenate([x_sample, jnp.zeros((nb, SEQ_PAD - T_dec, d), x_sample.dtype)], axis=1)
    h = jnp.concatenate([x_prompt.reshape(n_p, d), xs_pad.reshape(nb * SEQ_PAD, d)], axis=0)

    base = _t5_base(t5_table)
    e16 = jnp.repeat(jnp.eye(M_HEADS, dtype=F32), M_HEAD_DIM, axis=1)
    row = lambda x: x.reshape(1, -1).astype(F32)
    rc = min(RWKV_RC_CHUNK, S)
    shift_all = _shift_layout(state_shift).reshape(depth, nb, 1, 3 * R_WIDTH + LORA_W)
    outs = {k_: [] for k_ in ("k_p", "v_p", "ki_p", "conv_p", "sh_p", "k_s", "v_s", "ki_s", "conv_s", "sh_s")}
    ssm_p = ssm_s = wkv_p = wkv_s = None
    for l in range(depth):
        bf = lambda x: x.astype(BF16)
        h = _ffn(h, row(ffn1_norm[l]), bf(ffn1_gate[l]), bf(ffn1_up[l]), bf(ffn1_down[l]))
        p = _inproj(h, row(mix_norm[l]), _relayout_w_in(w_in[l]))
        kidx = _kidx_ln(p, row(kidx_ln_g[l]), row(kidx_ln_b[l]))

        ssd_par = (conv_w[l], row(conv_b[l]), row(dt_bias[l]), row(a_log[l]),
                   row(jnp.repeat(d_skip[l], M_HEAD_DIM)), row(ssm_norm[l]), e16)
        stack = dict(n_rows=n_rows, depth=depth, out_layer=l)
        ya, ssm_p = _ssd(p, 0, bp, S // SSD_CHUNK, SSD_CHUNK, SSD_CHUNK, 0,
                         jnp.zeros((1, bp, CONV_W - 1, CONV_DIM), F32),
                         jnp.zeros((1, bp, M_HEADS, M_HEAD_DIM, M_STATE), F32), *ssd_par,
                         st_buf=ssm_p, **stack)
        ya, ssm_s = _ssd(p, n_p, nb, 1, SEQ_PAD, T_dec, l, state_conv, state_ssm, *ssd_par,
                         y_buf=ya, st_buf=ssm_s, **stack)

        g2p = jnp.concatenate([rwkv_g2[l], jnp.zeros((LORA_W - R_DECAY_LORA - R_A_LORA - R_G_LORA,
                                                      R_WIDTH), F32)], axis=0)
        rw_par = (row(_shift_layout(rwkv_mu[l])), row(rwkv_w0[l]), rwkv_w2[l], row(rwkv_a0[l]),
                  rwkv_a2[l], g2p, row(rwkv_kk[l]), row(rwkv_ka[l]), row(rwkv_rk[l]),
                  row(rwkv_gn_g[l]), row(rwkv_gn_b[l]))
        sw = 3 * R_WIDTH + LORA_W
        yc, wkv_p = _rwkv(p, 0, bp, S // rc, rc, rc, 0, jnp.zeros((1, bp, 1, sw), F32),
                          jnp.zeros((1, bp, R_HEADS, R_HEAD, R_HEAD), F32), *rw_par,
                          st_buf=wkv_p, **stack)
        yc, wkv_s = _rwkv(p, n_p, nb, 1, SEQ_PAD, T_dec, l, shift_all, state_wkv, *rw_par,
                          y_buf=yc, st_buf=wkv_s, **stack)

        kp = p[:n_p, C_K:C_K + 256]
        vp = p[:n_p, C_V:C_V + 256]
        kit = bf(kidx[:n_p].T)
        kt = bf(kp.reshape(n_p, KV_HEADS, HEAD_DIM).transpose(1, 2, 0))
        vg = vp.reshape(n_p, KV_HEADS, HEAD_DIM).transpose(1, 0, 2)
        va = bf(jnp.concatenate([vg, jnp.ones((KV_HEADS, n_p, 1), F32),
                                 jnp.zeros((KV_HEADS, n_p, LANE - HEAD_DIM - 1), F32)], axis=-1))
        yb = _dsa_prompt(p, kit, kt, va, base, S, n_rows)
        yb = _dsa_sample(p, n_p, nb, kidx, base, l, cache_k, cache_v, cache_kidx, page_table, T_dec, yb)
        h = _merge(h, p, ya, yb, yc, bf(w_br_a[l]), bf(w_br_b[l]), bf(w_br_c[l]), bf(w_out[l]))
        h = _ffn(h, row(ffn2_norm[l]), bf(ffn2_gate[l]), bf(ffn2_up[l]), bf(ffn2_down[l]))

        ps = p[n_p:].reshape(nb, SEQ_PAD, P_W)[:, :T_dec]
        xbc = lambda t: jnp.concatenate([t[..., C_XS:C_XS + 1024], t[..., C_BC:C_BC + 1024]], axis=-1)
        prj = lambda t: jnp.concatenate([t[..., C_RR:C_RR + 3 * R_WIDTH],
                                         t[..., C_LORA:C_LORA + R_PROJ - 3 * R_WIDTH]], axis=-1)
        outs["k_p"].append(kp.reshape(bp, S, KV_HEADS, HEAD_DIM))
        outs["v_p"].append(vp.reshape(bp, S, KV_HEADS, HEAD_DIM))
        outs["ki_p"].append(kidx[:n_p].reshape(bp, S, IDX_DIM))
        outs["conv_p"].append(xbc(p[n_p - (CONV_W - 1):n_p]).reshape(bp, CONV_W - 1, CONV_DIM))
        outs["sh_p"].append(prj(p[n_p - 1:n_p]).reshape(bp, R_PROJ))
        outs["k_s"].append(ps[..., C_K:C_K + 256].reshape(nb, T_dec, KV_HEADS, HEAD_DIM))
        outs["v_s"].append(ps[..., C_V:C_V + 256].reshape(nb, T_dec, KV_HEADS, HEAD_DIM))
        outs["ki_s"].append(kidx[n_p:].reshape(nb, SEQ_PAD, IDX_DIM)[:, :T_dec])
        conv_full = jnp.concatenate([state_conv[l], xbc(ps)], axis=1)
        outs["conv_s"].append(conv_full[:, T_dec:])
        outs["sh_s"].append(prj(ps[:, T_dec - 1]))

    y = _final_norm(h, row(final_norm))
    y_prompt = y[:n_p].reshape(bp, S, d)
    y_sample = y[n_p:].reshape(nb, SEQ_PAD, d)[:, :T_dec]
    stk = lambda name: jnp.stack(outs[name], axis=0)
    return (y_prompt, y_sample, stk("k_p"), stk("v_p"), stk("ki_p"), ssm_p, stk("conv_p"),
            wkv_p, stk("sh_p"), stk("k_s"), stk("v_s"), stk("ki_s"), ssm_s,
            stk("conv_s"), wkv_s, stk("sh_s"))
```
